```python
import math
import jax, jax.numpy as jnp
from jax import lax
import numpy as np

D_MODEL = 1024
BATCH = 16
SEQ = 2048
DEPTH = 2
DEC_BATCH = 128
DEC_SEQ = 8
PAST_LEN = 16384
PAGE_SIZE = 128

MIX_WIDTH = D_MODEL
ATTN_WIDTH = MIX_WIDTH // 2
CONV_WIDTH = MIX_WIDTH - ATTN_WIDTH
HEAD_DIM = 64
N_Q_HEADS = ATTN_WIDTH // HEAD_DIM
N_KV_HEADS = max(1, N_Q_HEADS // 4)
Q_PER_KV = N_Q_HEADS // N_KV_HEADS
KV_WIDTH = N_KV_HEADS * HEAD_DIM
WINDOW = 128
ATTN_BLOCK = WINDOW
ATTN_SCALE = HEAD_DIM ** -0.5
CONV_K = 3
N_EXPERTS = 32
TOP_K = 4
D_FF = D_MODEL
SWIGLU_LIMIT = 7.0
SWIGLU_ALPHA = 1.702
MOE_BLOCK = 256
PLE_DIM = 256
LN_EPS = 1e-5
DN_ALPHA = (2.0 * DEPTH) ** 0.25
DN_BETA = (8.0 * DEPTH) ** -0.25
IN_COLS = ATTN_WIDTH + 2 * KV_WIDTH + 3 * CONV_WIDTH
SPLITS = [ATTN_WIDTH, ATTN_WIDTH + KV_WIDTH, ATTN_WIDTH + 2 * KV_WIDTH,
          ATTN_WIDTH + 2 * KV_WIDTH + CONV_WIDTH, ATTN_WIDTH + 2 * KV_WIDTH + 2 * CONV_WIDTH]

kernel_name = 'hymba_conv_swa_sink_moe_deepnorm_step'


def layer_norm(x, g, b):
    xf = x.astype(jnp.float32)
    mu = jnp.mean(xf, -1, keepdims=True)
    var = jnp.mean(jnp.square(xf - mu), -1, keepdims=True)
    y = (xf - mu) * lax.rsqrt(var + LN_EPS) * g.astype(jnp.float32) + b.astype(jnp.float32)
    return y.astype(x.dtype)


def sink_softmax(scores, mask, sink):
    s = jnp.where(mask, scores, -jnp.inf)
    m = jnp.maximum(jnp.max(s, -1, keepdims=True), sink)
    p = jnp.exp(s - m)
    return p / (jnp.sum(p, -1, keepdims=True) + jnp.exp(sink - m))


def swa_prompt(q, k, v, sinks):
    b, s = q.shape[:2]
    nb = s // ATTN_BLOCK
    qb = q.reshape(b, nb, ATTN_BLOCK, N_KV_HEADS, Q_PER_KV, HEAD_DIM)
    kb = k.reshape(b, nb, ATTN_BLOCK, N_KV_HEADS, HEAD_DIM)
    vb = v.reshape(b, nb, ATTN_BLOCK, N_KV_HEADS, HEAD_DIM)
    pad = jnp.zeros_like(kb[:, :1])
    kk = jnp.concatenate([jnp.concatenate([pad, kb[:, :-1]], 1), kb], 2)
    vv = jnp.concatenate([jnp.concatenate([pad, vb[:, :-1]], 1), vb], 2)
    scores = jnp.einsum('bnqkgd,bnskd->bnkgqs', qb, kk,
                        preferred_element_type=jnp.float32) * ATTN_SCALE
    blk = jnp.arange(nb)[:, None] * ATTN_BLOCK
    qpos = blk + jnp.arange(ATTN_BLOCK)[None, :]
    kpos = blk - ATTN_BLOCK + jnp.arange(2 * ATTN_BLOCK)[None, :]
    diff = qpos[:, :, None] - kpos[:, None, :]
    mask = (diff >= 0) & (diff < WINDOW) & (kpos[:, None, :] >= 0)
    mask = mask[None, :, None, None]
    sink = sinks.astype(jnp.float32).reshape(N_KV_HEADS, Q_PER_KV)[None, None, :, :, None, None]
    probs = sink_softmax(scores, mask, sink).astype(v.dtype)
    out = jnp.einsum('bnkgqs,bnskd->bnqkgd', probs, vv)
    return out.reshape(b, s, ATTN_WIDTH)


def swa_sample(q, k_new, v_new, k_buf, v_buf, sinks):
    b, L = q.shape[:2]
    w = k_buf.shape[1]
    kk = jnp.concatenate([k_buf, k_new], 1)
    vv = jnp.concatenate([v_buf, v_new], 1)
    qg = q.reshape(b, L, N_KV_HEADS, Q_PER_KV, HEAD_DIM)
    scores = jnp.einsum('bqkgd,bskd->bkgqs', qg, kk,
                        preferred_element_type=jnp.float32) * ATTN_SCALE
    qpos = jnp.arange(L)
    kpos = jnp.arange(w + L) - w
    diff = qpos[:, None] - kpos[None, :]
    mask = ((diff >= 0) & (diff < WINDOW))[None, None, None]
    sink = sinks.astype(jnp.float32).reshape(N_KV_HEADS, Q_PER_KV)[None, :, :, None, None]
    probs = sink_softmax(scores, mask, sink).astype(v_new.dtype)
    out = jnp.einsum('bkgqs,bskd->bqkgd', probs, vv).reshape(b, L, ATTN_WIDTH)
    return out, kk[:, -w:], vv[:, -w:]


def short_conv(u, buf, w):
    full = jnp.concatenate([buf, u], 1)
    L = u.shape[1]
    y = full[:, 0:L] * w[0]
    for t in range(1, CONV_K):
        y = y + full[:, t:t + L] * w[t]
    return y, full[:, -(CONV_K - 1):]


def clamped_swiglu(h):
    glu = jnp.minimum(h[:, ::2], SWIGLU_LIMIT)
    lin = jnp.clip(h[:, 1::2], -SWIGLU_LIMIT, SWIGLU_LIMIT)
    return glu * jax.nn.sigmoid(SWIGLU_ALPHA * glu) * (lin + 1.0)


def moe(x, w_r, b_r, w1, b1, w2, b2):
    shp = x.shape
    xt = x.reshape(-1, D_MODEL)
    T = xt.shape[0]
    logits = (xt @ w_r + b_r).astype(jnp.float32)
    top_v, top_i = lax.top_k(logits, TOP_K)
    gates = jax.nn.softmax(top_v, -1)
    A = T * TOP_K
    e = top_i.reshape(-1)
    tok = jnp.repeat(jnp.arange(T, dtype=jnp.int32), TOP_K)
    g = gates.reshape(-1)
    order = jnp.argsort(e)
    se, stok, sg = e[order], tok[order], g[order]
    counts = jnp.bincount(e, length=N_EXPERTS)
    padded = (counts + MOE_BLOCK - 1) // MOE_BLOCK * MOE_BLOCK
    pend = jnp.cumsum(padded)
    pstart = pend - padded
    sstart = jnp.cumsum(counts) - counts
    dest = pstart[se] + (jnp.arange(A) - sstart[se])
    n_blocks = -(-(A + N_EXPERTS * (MOE_BLOCK - 1)) // MOE_BLOCK)
    P = n_blocks * MOE_BLOCK
    row_tok = jnp.full((P,), T, jnp.int32).at[dest].set(stok)
    row_gate = jnp.zeros((P,), jnp.float32).at[dest].set(sg)
    blk_e = jnp.minimum(jnp.searchsorted(pend, jnp.arange(n_blocks) * MOE_BLOCK, side='right'),
                        N_EXPERTS - 1)
    xpad = jnp.concatenate([xt, jnp.zeros((1, D_MODEL), xt.dtype)], 0)

    def run_block(args):
        toks, ex = args
        h = xpad[toks] @ w1[ex] + b1[ex]
        return clamped_swiglu(h) @ w2[ex] + b2[ex]

    ys = lax.map(run_block, (row_tok.reshape(n_blocks, MOE_BLOCK), blk_e)).reshape(P, D_MODEL)
    ys = ys * row_gate[:, None].astype(ys.dtype)
    out = jnp.zeros((T + 1, D_MODEL), ys.dtype).at[row_tok].add(ys)[:T]
    return out.reshape(shp).astype(x.dtype)


def decoder_layer(x, p, k_buf, v_buf, conv_buf, w_in, b_in, conv_w, sinks, w_o, b_o, ln1_g, ln1_b,
                  w_gate, b_gate, w_ple, w_router, b_router, w1, b1, w2, b2, ln2_g, ln2_b):
    b, L, _ = x.shape
    z = x @ w_in + b_in
    q, k, v, gb, gc, h = jnp.split(z, SPLITS, -1)
    q = q.reshape(b, L, N_Q_HEADS, HEAD_DIM)
    k = k.reshape(b, L, N_KV_HEADS, HEAD_DIM)
    v = v.reshape(b, L, N_KV_HEADS, HEAD_DIM)
    if k_buf is None:
        attn = swa_prompt(q, k, v, sinks)
        new_k, new_v = k[:, -WINDOW:], v[:, -WINDOW:]
        conv_buf = jnp.zeros((b, CONV_K - 1, CONV_WIDTH), x.dtype)
    else:
        attn, new_k, new_v = swa_sample(q, k, v, k_buf, v_buf, sinks)
    conv_y, new_conv = short_conv(gc * h, conv_buf, conv_w)
    mix = jnp.concatenate([attn, gb * conv_y], -1) @ w_o + b_o
    x = layer_norm(DN_ALPHA * x + mix, ln1_g, ln1_b)
    gate = jax.nn.sigmoid(x @ w_gate + b_gate)
    ffn = moe(x, w_router, b_router, w1, b1, w2, b2)
    x = layer_norm(DN_ALPHA * x + ffn + gate * (p @ w_ple), ln2_g, ln2_b)
    return x, new_k, new_v, new_conv


def run_trunk(x, p, k_cache, v_cache, conv_cache, ln_emb_g, ln_emb_b, w_in, b_in, conv_w, sinks,
              w_o, b_o, ln1_g, ln1_b, w_gate, b_gate, w_ple, w_router, b_router, w1, b1, w2, b2,
              ln2_g, ln2_b):
    x = layer_norm(x, ln_emb_g, ln_emb_b)
    ks, vs, cs = [], [], []
    for i in range(DEPTH):
        kb = None if k_cache is None else k_cache[i]
        vb = None if v_cache is None else v_cache[i]
        cb = None if conv_cache is None else conv_cache[i]
        x, nk, nv, nc = decoder_layer(x, p[i], kb, vb, cb, w_in[i], b_in[i], conv_w[i], sinks[i],
                                      w_o[i], b_o[i], ln1_g[i], ln1_b[i], w_gate[i], b_gate[i],
                                      w_ple[i], w_router[i], b_router[i], w1[i], b1[i], w2[i], b2[i],
                                      ln2_g[i], ln2_b[i])
        ks.append(nk)
        vs.append(nv)
        cs.append(nc)
    return x, jnp.stack(ks), jnp.stack(vs), jnp.stack(cs)


def setup_inputs(seed: int = 0) -> dict:
    key = jax.random.key(seed)
    ks = jax.random.split(key, 32)
    f32 = jnp.float32
    w_cache = min(WINDOW, PAST_LEN)

    def nrm(k, shape, scale):
        return jax.random.normal(k, shape, f32) * scale

    return {
        'x_prompt': nrm(ks[0], (BATCH, SEQ, D_MODEL), 1.0),
        'x_sample': nrm(ks[1], (DEC_BATCH, DEC_SEQ, D_MODEL), 1.0),
        'cache_k': nrm(ks[2], (DEPTH, DEC_BATCH, w_cache, N_KV_HEADS, HEAD_DIM), 1.0),
        'cache_v': nrm(ks[3], (DEPTH, DEC_BATCH, w_cache, N_KV_HEADS, HEAD_DIM), 1.0),
        'state_conv': nrm(ks[4], (DEPTH, DEC_BATCH, CONV_K - 1, CONV_WIDTH), 1.0),
        'p_prompt': nrm(ks[5], (DEPTH, BATCH, SEQ, PLE_DIM), 1.0),
        'p_sample': nrm(ks[6], (DEPTH, DEC_BATCH, DEC_SEQ, PLE_DIM), 1.0),
        'ln_emb_g': 1.0 + nrm(ks[7], (D_MODEL,), 0.02),
        'ln_emb_b': nrm(ks[8], (D_MODEL,), 0.02),
        'w_in': nrm(ks[9], (DEPTH, D_MODEL, IN_COLS), D_MODEL ** -0.5),
        'b_in': nrm(ks[10], (DEPTH, IN_COLS), 0.02),
        'conv_w': nrm(ks[11], (DEPTH, CONV_K, CONV_WIDTH), CONV_K ** -0.5),
        'sinks': nrm(ks[12], (DEPTH, N_Q_HEADS), 0.5),
        'w_o': nrm(ks[13], (DEPTH, MIX_WIDTH, D_MODEL), MIX_WIDTH ** -0.5 * DN_BETA),
        'b_o': nrm(ks[14], (DEPTH, D_MODEL), 0.02),
        'ln1_g': 1.0 + nrm(ks[15], (DEPTH, D_MODEL), 0.02),
        'ln1_b': nrm(ks[16], (DEPTH, D_MODEL), 0.02),
        'w_gate': nrm(ks[17], (DEPTH, D_MODEL, D_MODEL), D_MODEL ** -0.5),
        'b_gate': nrm(ks[18], (DEPTH, D_MODEL), 0.02),
        'w_ple': nrm(ks[19], (DEPTH, PLE_DIM, D_MODEL), PLE_DIM ** -0.5 * DN_BETA),
        'w_router': nrm(ks[20], (DEPTH, D_MODEL, N_EXPERTS), D_MODEL ** -0.5),
        'b_router': nrm(ks[21], (DEPTH, N_EXPERTS), 0.01),
        'w1': nrm(ks[22], (DEPTH, N_EXPERTS, D_MODEL, 2 * D_FF), D_MODEL ** -0.5),
        'b1': nrm(ks[23], (DEPTH, N_EXPERTS, 2 * D_FF), 0.02),
        'w2': nrm(ks[24], (DEPTH, N_EXPERTS, D_FF, D_MODEL), D_FF ** -0.5 * DN_BETA),
        'b2': nrm(ks[25], (DEPTH, N_EXPERTS, D_MODEL), 0.02),
        'ln2_g': 1.0 + nrm(ks[26], (DEPTH, D_MODEL), 0.02),
        'ln2_b': nrm(ks[27], (DEPTH, D_MODEL), 0.02),
    }


def reference(x_prompt, x_sample, cache_k, cache_v, state_conv, p_prompt, p_sample, ln_emb_g,
              ln_emb_b, w_in, b_in, conv_w, sinks, w_o, b_o, ln1_g, ln1_b, w_gate, b_gate, w_ple,
              w_router, b_router, w1, b1, w2, b2, ln2_g, ln2_b):
    y_prompt, k_p, v_p, c_p = run_trunk(
        x_prompt, p_prompt, None, None, None, ln_emb_g, ln_emb_b, w_in, b_in, conv_w, sinks,
        w_o, b_o, ln1_g, ln1_b, w_gate, b_gate, w_ple, w_router, b_router, w1, b1, w2, b2,
        ln2_g, ln2_b)
    y_sample, k_s, v_s, c_s = run_trunk(
        x_sample, p_sample, cache_k, cache_v, state_conv, ln_emb_g, ln_emb_b, w_in, b_in, conv_w,
        sinks, w_o, b_o, ln1_g, ln1_b, w_gate, b_gate, w_ple, w_router, b_router, w1, b1, w2, b2,
        ln2_g, ln2_b)
    return (y_prompt, y_sample, k_p, v_p, c_p, k_s, v_s, c_s)
```

```python
import functools

import jax
import jax.numpy as jnp
from jax import lax
from jax.experimental import pallas as pl
from jax.experimental.pallas import tpu as pltpu

F32 = jnp.float32
BF16 = jnp.bfloat16
I32 = jnp.int32

D_MODEL = 1024
DEPTH = 2
HEAD_DIM = 64
N_Q_HEADS = 8
N_KV_HEADS = 2
ATTN_WIDTH = N_Q_HEADS * HEAD_DIM
KV_WIDTH = N_KV_HEADS * HEAD_DIM
CONV_WIDTH = D_MODEL - ATTN_WIDTH
WINDOW = 128
ATTN_SCALE = HEAD_DIM ** -0.5
CONV_K = 3
N_EXPERTS = 32
TOP_K = 4
D_FF = D_MODEL
SWIGLU_LIMIT = 7.0
SWIGLU_ALPHA = 1.702
PLE_DIM = 256
LN_EPS = 1e-5
DN_ALPHA = (2.0 * DEPTH) ** 0.25
IN_COLS = ATTN_WIDTH + 2 * KV_WIDTH + 3 * CONV_WIDTH
Q0, K0, V0, GB0, GC0, H0 = 0, 512, 640, 768, 1280, 1792

LANES = 128
SUBLANES = 8
VMEM_LIMIT = 48 * 1024 * 1024

TM = 512
TQ = 512
SEQ_GROUP = 8
TM_ROWS = 256
TM_MOE = 512


def _layer_norm(x, g, b):
    mu = jnp.mean(x, -1, keepdims=True)
    xc = x - mu
    var = jnp.mean(xc * xc, -1, keepdims=True)
    return xc * lax.rsqrt(var + LN_EPS) * g + b


def _div_pow2(x, n):
    assert n & (n - 1) == 0
    return lax.shift_right_arithmetic(x, n.bit_length() - 1)


def _mod_pow2(x, n):
    assert n & (n - 1) == 0
    return x & (n - 1)


def _cparams(sem):
    return pltpu.CompilerParams(dimension_semantics=sem, vmem_limit_bytes=VMEM_LIMIT)


def _full(shape):
    return pl.BlockSpec(shape, lambda *_: (0,) * len(shape))


def _inproj_kernel(*refs, apply_ln):
    if apply_ln:
        x_ref, g_ref, b_ref, w_ref, bi_ref, xn_ref, q_ref, k_ref, v_ref, gb_ref, u_ref = refs
    else:
        x_ref, w_ref, bi_ref, q_ref, k_ref, v_ref, gb_ref, u_ref = refs
    x = x_ref[...]
    if apply_ln:
        x = _layer_norm(x, g_ref[...], b_ref[...])
        xn_ref[...] = x
    xb = x.astype(BF16)

    def proj(lo, hi):
        return jnp.dot(xb, w_ref[:, lo:hi], preferred_element_type=F32) + bi_ref[:, lo:hi]

    q_ref[...] = (proj(Q0, K0) * ATTN_SCALE).astype(BF16)
    k_ref[...] = proj(K0, V0)
    v_ref[...] = proj(V0, GB0)
    gb_ref[...] = proj(GB0, GC0)
    u_ref[...] = proj(GC0, H0) * proj(H0, IN_COLS)


def _inproj(x, ln, w_in_b, b_in, apply_ln):
    t = x.shape[0]
    row = lambda w: pl.BlockSpec((TM, w), lambda i: (i, 0))
    in_specs = [row(D_MODEL)]
    args = [x]
    if apply_ln:
        in_specs += [_full((1, D_MODEL)), _full((1, D_MODEL))]
        args += [ln[0], ln[1]]
    in_specs += [_full((D_MODEL, IN_COLS)), _full((1, IN_COLS))]
    args += [w_in_b, b_in]
    out_shape, out_specs = [], []
    if apply_ln:
        out_shape.append(jax.ShapeDtypeStruct((t, D_MODEL), F32))
        out_specs.append(row(D_MODEL))
    out_shape += [jax.ShapeDtypeStruct((t, ATTN_WIDTH), BF16),
                  jax.ShapeDtypeStruct((t, KV_WIDTH), F32),
                  jax.ShapeDtypeStruct((t, KV_WIDTH), F32),
                  jax.ShapeDtypeStruct((t, CONV_WIDTH), F32),
                  jax.ShapeDtypeStruct((t, CONV_WIDTH), F32)]
    out_specs += [row(ATTN_WIDTH), row(KV_WIDTH), row(KV_WIDTH), row(CONV_WIDTH), row(CONV_WIDTH)]
    outs = pl.pallas_call(
        functools.partial(_inproj_kernel, apply_ln=apply_ln),
        grid=(t // TM,), in_specs=in_specs, out_specs=out_specs, out_shape=out_shape,
        compiler_params=_cparams(("parallel",)), name="inproj")(*args)
    if apply_ln:
        return outs
    return [x] + list(outs)


def _attend_column(qcol, kexp_b, vexp_b, mask, sink_a, sink_b):
    lane = lax.broadcasted_iota(I32, qcol.shape, 1)
    outs = []
    for half, sink in ((0, sink_a), (1, sink_b)):
        keep = (lane < HEAD_DIM) if half == 0 else (lane >= HEAD_DIM)
        qm = jnp.where(keep, qcol, jnp.zeros_like(qcol))
        s = lax.dot_general(qm, kexp_b, (((1,), (1,)), ((), ())), preferred_element_type=F32)
        s = jnp.where(mask, s, -jnp.inf)
        m = jnp.maximum(jnp.max(s, -1, keepdims=True), sink)
        p = jnp.exp(s - m)
        denom = jnp.sum(p, -1, keepdims=True) + jnp.exp(sink - m)
        o = jnp.dot(p.astype(BF16), vexp_b, preferred_element_type=F32)
        outs.append(o * (1.0 / denom))
    return jnp.where(lane < HEAD_DIM, outs[0], outs[1])


def _dup_heads(x):
    lane = lax.broadcasted_iota(I32, x.shape, 1)
    xr = pltpu.roll(x, HEAD_DIM, 1)
    low = lane < HEAD_DIM
    return (jnp.where(low, x, xr).astype(BF16), jnp.where(low, xr, x).astype(BF16))


def _prompt_mixer_kernel(sinks_ref, q_ref, kc_ref, kp_ref, vc_ref, vp_ref, uc_ref, up_ref,
                         gb_ref, cw_ref, o_ref):
    first = pl.program_id(1) == 0
    r_i = lax.broadcasted_iota(I32, (WINDOW, 2 * WINDOW), 0)
    c_i = lax.broadcasted_iota(I32, (WINDOW, 2 * WINDOW), 1)
    band = (c_i <= WINDOW + r_i) & (c_i > r_i)
    band_first = band & (c_i >= WINDOW * first.astype(I32))
    for s in range(TQ // WINDOW):
        rows = slice(s * WINDOW, (s + 1) * WINDOW)
        if s == 0:
            kprev, vprev = kp_ref[...], vp_ref[...]
        else:
            prev = slice((s - 1) * WINDOW, s * WINDOW)
            kprev, vprev = kc_ref[prev, :], vc_ref[prev, :]
        kk = jnp.concatenate([kprev, kc_ref[rows, :]], 0)
        vv = jnp.concatenate([vprev, vc_ref[rows, :]], 0)
        kexp = _dup_heads(kk)
        vexp = _dup_heads(vv)
        mask = band_first if s == 0 else band
        for col in range(ATTN_WIDTH // LANES):
            h = col // 2
            cols = slice(col * LANES, (col + 1) * LANES)
            out = _attend_column(q_ref[rows, cols], kexp[h], vexp[h], mask,
                                 sinks_ref[2 * col], sinks_ref[2 * col + 1])
            o_ref[rows, cols] = out.astype(BF16)

    u = uc_ref[...]
    up = up_ref[...]
    zero = jnp.zeros((1, CONV_WIDTH), F32)
    p1 = jnp.where(first, zero, up[SUBLANES - 1:SUBLANES, :])
    p2 = jnp.where(first, zero, up[SUBLANES - 2:SUBLANES - 1, :])
    row = lax.broadcasted_iota(I32, u.shape, 0)
    u1 = jnp.where(row == 0, p1, pltpu.roll(u, 1, 0))
    u2 = jnp.where(row == 0, p2, jnp.where(row == 1, p1, pltpu.roll(u, 2, 0)))
    cw = cw_ref[...]
    y = u2 * cw[0:1, :] + u1 * cw[1:2, :] + u * cw[2:3, :]
    o_ref[:, ATTN_WIDTH:] = (gb_ref[...] * y).astype(BF16)


def _prompt_mixer(sinks, q, k, v, u, gb, conv_w, batch, seq):
    nj = seq // TQ
    tile = lambda b, j: b * nj + j
    cur = lambda w: pl.BlockSpec((TQ, w), lambda b, j: (tile(b, j), 0))
    prev_kv = pl.BlockSpec((WINDOW, KV_WIDTH),
                           lambda b, j: (jnp.maximum(tile(b, j) * (TQ // WINDOW) - 1, 0), 0))
    prev_u = pl.BlockSpec((SUBLANES, CONV_WIDTH),
                          lambda b, j: (jnp.maximum(tile(b, j) * (TQ // SUBLANES) - 1, 0), 0))
    return pl.pallas_call(
        _prompt_mixer_kernel,
        grid=(batch, nj),
        in_specs=[pl.BlockSpec(memory_space=pltpu.SMEM),
                  cur(ATTN_WIDTH), cur(KV_WIDTH), prev_kv, cur(KV_WIDTH), prev_kv,
                  cur(CONV_WIDTH), prev_u, cur(CONV_WIDTH), _full((CONV_K, CONV_WIDTH))],
        out_specs=cur(D_MODEL),
        out_shape=jax.ShapeDtypeStruct((batch * seq, D_MODEL), BF16),
        compiler_params=_cparams(("parallel", "parallel")), name="prompt_mixer",
    )(sinks, q, k, k, v, v, u, u, gb, conv_w)


def _sample_mixer_kernel(sinks_ref, q_ref, kn_ref, vn_ref, kb_ref, vb_ref, u_ref, st_ref, gb_ref,
                         cw_ref, o_ref, nk_ref, nv_ref, *, dec_seq):
    g, w = kb_ref.shape[0], kb_ref.shape[1]
    rows = g * dec_seq
    n_cache = g * w
    n_keys = n_cache + 2 * rows
    kn, vn = kn_ref[...], vn_ref[...]
    pad = jnp.zeros((rows, KV_WIDTH), F32)
    kk = jnp.concatenate([kb_ref[...].reshape(n_cache, KV_WIDTH), kn, pad], 0)
    vv = jnp.concatenate([vb_ref[...].reshape(n_cache, KV_WIDTH), vn, pad], 0)
    kexp = _dup_heads(kk)
    vexp = _dup_heads(vv)

    r_i = lax.broadcasted_iota(I32, (rows, n_keys), 0)
    c_i = lax.broadcasted_iota(I32, (rows, n_keys), 1)
    r_seq, r_pos = _div_pow2(r_i, dec_seq), _mod_pow2(r_i, dec_seq)
    c_new = c_i - n_cache
    in_cache = ((c_i < n_cache) & (_div_pow2(c_i, w) == r_seq)
                & (_mod_pow2(c_i, w) > r_pos + (w - WINDOW)))
    in_new = ((c_new >= 0) & (c_new < rows) & (_div_pow2(c_new, dec_seq) == r_seq)
              & (_mod_pow2(c_new, dec_seq) <= r_pos))
    mask = in_cache | in_new
    for col in range(ATTN_WIDTH // LANES):
        h = col // 2
        cols = slice(col * LANES, (col + 1) * LANES)
        out = _attend_column(q_ref[:, cols], kexp[h], vexp[h], mask,
                             sinks_ref[2 * col], sinks_ref[2 * col + 1])
        o_ref[:, cols] = out.astype(BF16)

    nk_ref[:, 0:w - dec_seq, :] = kb_ref[:, dec_seq:w, :]
    nk_ref[:, w - dec_seq:w, :] = kn.reshape(g, dec_seq, KV_WIDTH)
    nv_ref[:, 0:w - dec_seq, :] = vb_ref[:, dec_seq:w, :]
    nv_ref[:, w - dec_seq:w, :] = vn.reshape(g, dec_seq, KV_WIDTH)

    u = u_ref[...]
    st = st_ref[...]
    pos = lax.broadcasted_iota(I32, u.shape, 0) % dec_seq
    u1 = jnp.where(pos == 0, pltpu.roll(st, rows - 1, 0), pltpu.roll(u, 1, 0))
    u2 = jnp.where(pos < 2, st, pltpu.roll(u, 2, 0))
    cw = cw_ref[...]
    y = u2 * cw[0:1, :] + u1 * cw[1:2, :] + u * cw[2:3, :]
    o_ref[:, ATTN_WIDTH:] = (gb_ref[...] * y).astype(BF16)


def _sample_mixer(sinks, q, k, v, u, gb, conv_w, cache_k, cache_v, state_rows, t_prompt):
    nseq, w = cache_k.shape[0], cache_k.shape[1]
    dec_seq = (q.shape[0] - t_prompt) // nseq
    rows = SEQ_GROUP * dec_seq
    off = t_prompt // rows
    tok = lambda wd: pl.BlockSpec((rows, wd), lambda i: (off + i, 0))
    local = lambda wd: pl.BlockSpec((rows, wd), lambda i: (i, 0))
    cache = pl.BlockSpec((SEQ_GROUP, w, KV_WIDTH), lambda i: (i, 0, 0))
    return pl.pallas_call(
        functools.partial(_sample_mixer_kernel, dec_seq=dec_seq),
        grid=(nseq // SEQ_GROUP,),
        in_specs=[pl.BlockSpec(memory_space=pltpu.SMEM),
                  tok(ATTN_WIDTH), tok(KV_WIDTH), tok(KV_WIDTH), cache, cache,
                  tok(CONV_WIDTH), local(CONV_WIDTH), tok(CONV_WIDTH),
                  _full((CONV_K, CONV_WIDTH))],
        out_specs=[local(D_MODEL), cache, cache],
        out_shape=[jax.ShapeDtypeStruct((nseq * dec_seq, D_MODEL), BF16),
                   jax.ShapeDtypeStruct(cache_k.shape, F32),
                   jax.ShapeDtypeStruct(cache_v.shape, F32)],
        compiler_params=_cparams(("parallel",)), name="sample_mixer",
    )(sinks, q, k, v, cache_k, cache_v, u, state_rows, gb, conv_w)


def _post_mixer_kernel(mixp_ref, mixs_ref, xn_ref, p_ref, wo_ref, bo_ref, g1_ref, b1_ref, wg_ref,
                       bg_ref, wp_ref, wrh_ref, wrl_ref, br_ref,
                       x1_ref, c_ref, eid_ref, gate_ref, rank_ref, cnt_ref, carry_ref, *, n_prompt):
    @pl.when(pl.program_id(0) == 0)
    def _():
        carry_ref[...] = jnp.zeros_like(carry_ref)

    mixed = jnp.where(pl.program_id(0) < n_prompt, mixp_ref[...], mixs_ref[...])
    mix = jnp.dot(mixed, wo_ref[...], preferred_element_type=F32) + bo_ref[...]
    x1 = _layer_norm(DN_ALPHA * xn_ref[...] + mix, g1_ref[...], b1_ref[...])
    x1_ref[...] = x1
    x1h = x1.astype(BF16)
    x1l = (x1 - x1h.astype(F32)).astype(BF16)
    gate = jax.nn.sigmoid(jnp.dot(x1h, wg_ref[...], preferred_element_type=F32) + bg_ref[...])
    ple = jnp.dot(p_ref[...], wp_ref[...], preferred_element_type=F32)
    c_ref[...] = DN_ALPHA * x1 + gate * ple

    nt = (((1,), (1,)), ((), ()))
    logits = (lax.dot_general(wrh_ref[...], x1h, nt, preferred_element_type=F32)
              + lax.dot_general(wrh_ref[...], x1l, nt, preferred_element_type=F32)
              + lax.dot_general(wrl_ref[...], x1h, nt, preferred_element_type=F32)
              + br_ref[...])
    tm = logits.shape[1]
    e_i = lax.broadcasted_iota(I32, logits.shape, 0).astype(F32)
    work = logits
    vals, sels = [], []
    for k in range(TOP_K):
        m = jnp.max(work, 0, keepdims=True)
        idx = jnp.min(jnp.where(work == m, e_i, float(N_EXPERTS)), 0, keepdims=True)
        sel = e_i == idx
        vals.append(m)
        sels.append(sel)
        eid_ref[k:k + 1, :] = idx.astype(I32)
        work = jnp.where(sel, -jnp.inf, work)
    exps = [jnp.exp(v - vals[0]) for v in vals]
    denom = exps[0] + exps[1] + exps[2] + exps[3]
    for k in range(TOP_K):
        gate_ref[k:k + 1, :] = exps[k] / denom

    chosen = jnp.where(sels[0] | sels[1] | sels[2] | sels[3], 1.0, 0.0)
    s_i = lax.broadcasted_iota(I32, (tm, tm), 0)
    t_i = lax.broadcasted_iota(I32, (tm, tm), 1)
    before = jnp.where(s_i < t_i, 1.0, 0.0).astype(BF16)
    pos = carry_ref[:, 0:1] + jnp.dot(chosen.astype(BF16), before, preferred_element_type=F32)
    for k in range(TOP_K):
        rank_ref[k:k + 1, :] = jnp.sum(jnp.where(sels[k], pos, 0.0), 0, keepdims=True).astype(I32)
    carry = carry_ref[...] + jnp.sum(chosen, 1, keepdims=True)
    carry_ref[...] = carry
    cnt_ref[...] = carry


def _post_mixer(mix_p, mix_s, xn, p_b, w_o, b_o, g1, b1, w_gate, b_gate, w_ple, wr_hi, wr_lo, b_r):
    t = xn.shape[0]
    n_p, n_s = mix_p.shape[0] // TM, mix_s.shape[0] // TM
    row = lambda w: pl.BlockSpec((TM, w), lambda i: (i, 0))
    meta = pl.BlockSpec((TOP_K, TM), lambda i: (0, i))
    vec = _full((1, D_MODEL))
    return pl.pallas_call(
        functools.partial(_post_mixer_kernel, n_prompt=n_p),
        grid=(t // TM,),
        in_specs=[pl.BlockSpec((TM, D_MODEL), lambda i: (jnp.minimum(i, n_p - 1), 0)),
                  pl.BlockSpec((TM, D_MODEL), lambda i: (jnp.clip(i - n_p, 0, n_s - 1), 0)),
                  row(D_MODEL), row(PLE_DIM),
                  _full((D_MODEL, D_MODEL)), vec, vec, vec,
                  _full((D_MODEL, D_MODEL)), vec, _full((PLE_DIM, D_MODEL)),
                  _full((N_EXPERTS, D_MODEL)), _full((N_EXPERTS, D_MODEL)), _full((N_EXPERTS, 1))],
        out_specs=[row(D_MODEL), row(D_MODEL), meta, meta, meta, _full((N_EXPERTS, LANES))],
        out_shape=[jax.ShapeDtypeStruct((t, D_MODEL), F32),
                   jax.ShapeDtypeStruct((t, D_MODEL), F32),
                   jax.ShapeDtypeStruct((TOP_K, t), I32),
                   jax.ShapeDtypeStruct((TOP_K, t), F32),
                   jax.ShapeDtypeStruct((TOP_K, t), I32),
                   jax.ShapeDtypeStruct((N_EXPERTS, LANES), F32)],
        scratch_shapes=[pltpu.VMEM((N_EXPERTS, LANES), F32)],
        compiler_params=_cparams(("arbitrary",)), name="post_mixer",
    )(mix_p, mix_s, xn, p_b, w_o, b_o, g1, b1, w_gate, b_gate, w_ple, wr_hi, wr_lo, b_r)


def _row_copy(src, dst, sem):
    return pltpu.make_async_copy(src, dst, sem)


def _dispatch_kernel(dest_ref, x_ref, xs_hbm, sem):
    tm = x_ref.shape[0]

    def issue(r, carry):
        for k in range(TOP_K):
            _row_copy(x_ref.at[pl.ds(r, 1)], xs_hbm.at[pl.ds(dest_ref[k, r], 1)], sem).start()
        return carry

    lax.fori_loop(0, tm, issue, 0)
    for k in range(TOP_K):
        _row_copy(x_ref, xs_hbm.at[pl.ds(0, tm)], sem).wait()


def _dispatch(dest_tiles, x1):
    t = x1.shape[0]
    return pl.pallas_call(
        _dispatch_kernel,
        grid=(t // TM_ROWS,),
        in_specs=[pl.BlockSpec((None, TOP_K, TM_ROWS), lambda i: (i, 0, 0), memory_space=pltpu.SMEM),
                  pl.BlockSpec((TM_ROWS, D_MODEL), lambda i: (i, 0))],
        out_specs=pl.BlockSpec(memory_space=pl.ANY),
        out_shape=jax.ShapeDtypeStruct((t * TOP_K, D_MODEL), F32),
        scratch_shapes=[pltpu.SemaphoreType.DMA],
        compiler_params=_cparams(("arbitrary",)), name="dispatch",
    )(dest_tiles, x1)


def _moe_kernel(tile_ref, exp_ref, lo_ref, hi_ref, first_ref,
                xs_ref, w1g_ref, w1l_ref, b1g_ref, b1l_ref, w2_ref, b2_ref, ys_ref):
    del tile_ref, exp_ref
    w = pl.program_id(0)
    lo, hi = lo_ref[w], hi_ref[w]

    @pl.when(hi > lo)
    def _():
        xb = xs_ref[...].astype(BF16)
        glu = jnp.dot(xb, w1g_ref[...], preferred_element_type=F32) + b1g_ref[...]
        lin = jnp.dot(xb, w1l_ref[...], preferred_element_type=F32) + b1l_ref[...]
        glu = jnp.minimum(glu, SWIGLU_LIMIT)
        lin = jnp.clip(lin, -SWIGLU_LIMIT, SWIGLU_LIMIT)
        act = glu * jax.nn.sigmoid(SWIGLU_ALPHA * glu) * (lin + 1.0)
        y = jnp.dot(act.astype(BF16), w2_ref[...], preferred_element_type=F32) + b2_ref[...]
        r = lax.broadcasted_iota(I32, y.shape, 0)
        mine = (r >= lo) & (r < hi)

        @pl.when(first_ref[w] == 1)
        def _():
            ys_ref[...] = jnp.where(mine, y, 0.0)

        @pl.when(first_ref[w] == 0)
        def _():
            ys_ref[...] = jnp.where(mine, y, ys_ref[...])


def _moe(sched, xs, w1g, w1l, b1g, b1l, w2, b2):
    a = xs.shape[0]
    n_items = sched[0].shape[0]
    wspec = pl.BlockSpec((None, D_MODEL, D_FF), lambda w, tile, ex, *_: (ex[w], 0, 0))
    bspec = pl.BlockSpec((None, 1, D_FF), lambda w, tile, ex, *_: (ex[w], 0, 0))
    rows = pl.BlockSpec((TM_MOE, D_MODEL), lambda w, tile, *_: (tile[w], 0))
    return pl.pallas_call(
        _moe_kernel,
        grid_spec=pltpu.PrefetchScalarGridSpec(
            num_scalar_prefetch=5, grid=(n_items,),
            in_specs=[rows, wspec, wspec, bspec, bspec, wspec, bspec],
            out_specs=rows),
        out_shape=jax.ShapeDtypeStruct((a, D_MODEL), F32),
        compiler_params=_cparams(("arbitrary",)), name="moe",
    )(*sched, xs, w1g, w1l, b1g, b1l, w2, b2)


def _moe_schedule(counts, n_rows):
    n_tiles = n_rows // TM_MOE
    n_items = n_tiles + N_EXPERTS - 1
    pend = jnp.cumsum(counts)
    pstart = pend - counts
    first_tile = pstart // TM_MOE
    last_tile = jnp.maximum(pend - 1, 0) // TM_MOE
    ntile = jnp.where(counts > 0, last_tile - first_tile + 1, 0)
    wend = jnp.cumsum(ntile)
    wstart = wend - ntile
    total = wend[-1]
    w = jnp.arange(n_items, dtype=I32)
    wv = jnp.minimum(w, total - 1)
    ex = jnp.minimum(jnp.searchsorted(wend, wv, side="right"), N_EXPERTS - 1).astype(I32)
    tile = (first_tile[ex] + wv - wstart[ex]).astype(I32)
    valid = w < total
    lo = jnp.where(valid, jnp.clip(pstart[ex] - tile * TM_MOE, 0, TM_MOE), 0).astype(I32)
    hi = jnp.where(valid, jnp.clip(pend[ex] - tile * TM_MOE, 0, TM_MOE), 0).astype(I32)
    prev_tile = jnp.concatenate([jnp.full((1,), -1, I32), tile[:-1]])
    first = (valid & (tile != prev_tile)).astype(I32)
    return (tile, ex, lo, hi, first), pstart


def _combine_kernel(dest_ref, c_ref, gate_ref, g2_ref, b2_ref, ys_hbm, o_ref, buf, sem):
    tm = c_ref.shape[0]

    def issue(r, carry):
        for k in range(TOP_K):
            _row_copy(ys_hbm.at[pl.ds(dest_ref[k, r], 1)], buf.at[k, pl.ds(r, 1)], sem).start()
        return carry

    lax.fori_loop(0, tm, issue, 0)
    for k in range(TOP_K):
        _row_copy(ys_hbm.at[pl.ds(0, tm)], buf.at[k], sem).wait()
    acc = c_ref[...]
    gates = gate_ref[...]
    for k in range(TOP_K):
        acc = acc + gates[:, k:k + 1] * buf[k]
    o_ref[...] = _layer_norm(acc, g2_ref[...], b2_ref[...])


def _combine(dest_tiles, c, gates_t, g2, b2, ys):
    t = c.shape[0]
    vec = _full((1, D_MODEL))
    return pl.pallas_call(
        _combine_kernel,
        grid=(t // TM_ROWS,),
        in_specs=[pl.BlockSpec((None, TOP_K, TM_ROWS), lambda i: (i, 0, 0), memory_space=pltpu.SMEM),
                  pl.BlockSpec((TM_ROWS, D_MODEL), lambda i: (i, 0)),
                  pl.BlockSpec((TM_ROWS, TOP_K), lambda i: (i, 0)),
                  vec, vec, pl.BlockSpec(memory_space=pl.ANY)],
        out_specs=pl.BlockSpec((TM_ROWS, D_MODEL), lambda i: (i, 0)),
        out_shape=jax.ShapeDtypeStruct((t, D_MODEL), F32),
        scratch_shapes=[pltpu.VMEM((TOP_K, TM_ROWS, D_MODEL), F32), pltpu.SemaphoreType.DMA],
        compiler_params=_cparams(("arbitrary",)), name="combine",
    )(dest_tiles, c, gates_t, g2, b2, ys)


def kernel(x_prompt, x_sample, cache_k, cache_v, state_conv, p_prompt, p_sample, ln_emb_g, ln_emb_b,
           w_in, b_in, conv_w, sinks, w_o, b_o, ln1_g, ln1_b, w_gate, b_gate, w_ple, w_router,
           b_router, w1, b1, w2, b2, ln2_g, ln2_b):
    batch, seq, _ = x_prompt.shape
    nseq, dec_seq, _ = x_sample.shape
    t_p, t_s = batch * seq, nseq * dec_seq
    t = t_p + t_s
    w_cache = cache_k.shape[2]
    vec = lambda a: a.reshape(1, -1)

    x = jnp.concatenate([x_prompt.reshape(t_p, D_MODEL), x_sample.reshape(t_s, D_MODEL)], 0)
    state_rows = jnp.pad(state_conv, ((0, 0), (0, 0), (0, dec_seq - (CONV_K - 1)), (0, 0)))
    state_rows = state_rows.reshape(DEPTH, t_s, CONV_WIDTH)

    ks_p, vs_p, cs_p, ks_s, vs_s, cs_s = [], [], [], [], [], []
    for l in range(DEPTH):
        w_in_b = w_in[l].astype(BF16)
        outs = _inproj(x, (vec(ln_emb_g), vec(ln_emb_b)), w_in_b, vec(b_in[l]), apply_ln=(l == 0))
        xn, q, k, v, gb, u = outs

        mix_p = _prompt_mixer(sinks[l], q, k, v, u, gb, conv_w[l], batch, seq)
        ck = cache_k[l].reshape(nseq, w_cache, KV_WIDTH)
        cv = cache_v[l].reshape(nseq, w_cache, KV_WIDTH)
        mix_s, nk_s, nv_s = _sample_mixer(sinks[l], q, k, v, u, gb, conv_w[l], ck, cv,
                                          state_rows[l], t_p)

        p_b = jnp.concatenate([p_prompt[l].reshape(t_p, PLE_DIM),
                               p_sample[l].reshape(t_s, PLE_DIM)], 0).astype(BF16)
        wr_t = w_router[l].T
        wr_hi = wr_t.astype(BF16)
        wr_lo = (wr_t - wr_hi.astype(F32)).astype(BF16)
        x1, c, eid, gates, ranks, cnt = _post_mixer(
            mix_p, mix_s, xn, p_b, w_o[l].astype(BF16), vec(b_o[l]), vec(ln1_g[l]), vec(ln1_b[l]),
            w_gate[l].astype(BF16), vec(b_gate[l]), w_ple[l].astype(BF16), wr_hi, wr_lo,
            b_router[l].reshape(N_EXPERTS, 1))

        counts = cnt[:, 0].astype(I32)
        sched, pstart = _moe_schedule(counts, t * TOP_K)
        dest = pstart[eid] + ranks
        dest_tiles = dest.reshape(TOP_K, t // TM_ROWS, TM_ROWS).transpose(1, 0, 2)

        xs = _dispatch(dest_tiles, x1)
        w1g = w1[l][:, :, 0::2].astype(BF16)
        w1l = w1[l][:, :, 1::2].astype(BF16)
        b1g = b1[l][:, None, 0::2]
        b1l = b1[l][:, None, 1::2]
        ys = _moe(sched, xs, w1g, w1l, b1g, b1l, w2[l].astype(BF16), b2[l][:, None, :])
        x = _combine(dest_tiles, c, gates.T, vec(ln2_g[l]), vec(ln2_b[l]), ys)

        kp = k[:t_p].reshape(batch, seq, N_KV_HEADS, HEAD_DIM)
        vp = v[:t_p].reshape(batch, seq, N_KV_HEADS, HEAD_DIM)
        ks_p.append(kp[:, seq - WINDOW:])
        vs_p.append(vp[:, seq - WINDOW:])
        cs_p.append(u[:t_p].reshape(batch, seq, CONV_WIDTH)[:, seq - (CONV_K - 1):])
        ks_s.append(nk_s.reshape(nseq, w_cache, N_KV_HEADS, HEAD_DIM))
        vs_s.append(nv_s.reshape(nseq, w_cache, N_KV_HEADS, HEAD_DIM))
        cs_s.append(u[t_p:].reshape(nseq, dec_seq, CONV_WIDTH)[:, dec_seq - (CONV_K - 1):])

    y_prompt = x[:t_p].reshape(batch, seq, D_MODEL)
    y_sample = x[t_p:].reshape(nseq, dec_seq, D_MODEL)
    return (y_prompt, y_sample, jnp.stack(ks_p), jnp.stack(vs_p), jnp.stack(cs_p),
            jnp.stack(ks_s), jnp.stack(vs_s), jnp.stack(cs_s))
```

```python
import functools

import jax
import jax.numpy as jnp
from jax import lax
from jax.experimental import pallas as pl
from jax.experimental.pallas import tpu as pltpu

F32 = jnp.float32
BF16 = jnp.bfloat16
I32 = jnp.int32

D_MODEL = 1024
DEPTH = 2
HEAD_DIM = 64
N_Q_HEADS = 8
N_KV_HEADS = 2
ATTN_WIDTH = N_Q_HEADS * HEAD_DIM
KV_WIDTH = N_KV_HEADS * HEAD_DIM
CONV_WIDTH = D_MODEL - ATTN_WIDTH
WINDOW = 128
ATTN_SCALE = HEAD_DIM ** -0.5
CONV_K = 3
N_EXPERTS = 32
TOP_K = 4
D_FF = D_MODEL
SWIGLU_LIMIT = 7.0
SWIGLU_ALPHA = 1.702
PLE_DIM = 256
LN_EPS = 1e-5
DN_ALPHA = (2.0 * DEPTH) ** 0.25
IN_COLS = ATTN_WIDTH + 2 * KV_WIDTH + 3 * CONV_WIDTH
Q0, K0, V0, GB0, GC0, H0 = 0, 512, 640, 768, 1280, 1792

LANES = 128
SUBLANES = 8
VMEM_LIMIT = 48 * 1024 * 1024

TM = 512
TQ = 512
SEQ_GROUP = 8
TM_ROWS = 256
TM_MOE = 512


def _layer_norm(x, g, b):
    mu = jnp.mean(x, -1, keepdims=True)
    xc = x - mu
    var = jnp.mean(xc * xc, -1, keepdims=True)
    return xc * lax.rsqrt(var + LN_EPS) * g + b


def _div_pow2(x, n):
    assert n & (n - 1) == 0
    return lax.shift_right_arithmetic(x, n.bit_length() - 1)


def _mod_pow2(x, n):
    assert n & (n - 1) == 0
    return x & (n - 1)


def _cparams(sem):
    return pltpu.CompilerParams(dimension_semantics=sem, vmem_limit_bytes=VMEM_LIMIT)


def _full(shape):
    return pl.BlockSpec(shape, lambda *_: (0,) * len(shape))


def _inproj_kernel(*refs, apply_ln):
    if apply_ln:
        x_ref, g_ref, b_ref, w_ref, bi_ref, xn_ref, q_ref, k_ref, v_ref, gb_ref, u_ref = refs
    else:
        x_ref, w_ref, bi_ref, q_ref, k_ref, v_ref, gb_ref, u_ref = refs
    x = x_ref[...]
    if apply_ln:
        x = _layer_norm(x, g_ref[...], b_ref[...])
        xn_ref[...] = x
    xb = x.astype(BF16)

    def proj(lo, hi):
        return jnp.dot(xb, w_ref[:, lo:hi], preferred_element_type=F32) + bi_ref[:, lo:hi]

    q_ref[...] = (proj(Q0, K0) * ATTN_SCALE).astype(BF16)
    k_ref[...] = proj(K0, V0)
    v_ref[...] = proj(V0, GB0)
    gb_ref[...] = proj(GB0, GC0)
    u_ref[...] = proj(GC0, H0) * proj(H0, IN_COLS)


def _inproj(x, ln, w_in_b, b_in, apply_ln):
    t = x.shape[0]
    row = lambda w: pl.BlockSpec((TM, w), lambda i: (i, 0))
    in_specs = [row(D_MODEL)]
    args = [x]
    if apply_ln:
        in_specs += [_full((1, D_MODEL)), _full((1, D_MODEL))]
        args += [ln[0], ln[1]]
    in_specs += [_full((D_MODEL, IN_COLS)), _full((1, IN_COLS))]
    args += [w_in_b, b_in]
    out_shape, out_specs = [], []
    if apply_ln:
        out_shape.append(jax.ShapeDtypeStruct((t, D_MODEL), F32))
        out_specs.append(row(D_MODEL))
    out_shape += [jax.ShapeDtypeStruct((t, ATTN_WIDTH), BF16),
                  jax.ShapeDtypeStruct((t, KV_WIDTH), F32),
                  jax.ShapeDtypeStruct((t, KV_WIDTH), F32),
                  jax.ShapeDtypeStruct((t, CONV_WIDTH), F32),
                  jax.ShapeDtypeStruct((t, CONV_WIDTH), F32)]
    out_specs += [row(ATTN_WIDTH), row(KV_WIDTH), row(KV_WIDTH), row(CONV_WIDTH), row(CONV_WIDTH)]
    outs = pl.pallas_call(
        functools.partial(_inproj_kernel, apply_ln=apply_ln),
        grid=(t // TM,), in_specs=in_specs, out_specs=out_specs, out_shape=out_shape,
        compiler_params=_cparams(("parallel",)), name="inproj")(*args)
    if apply_ln:
        return outs
    return [x] + list(outs)


def _attend_column(qcol, kexp_b, vexp_b, mask, sink_a, sink_b):
    lane = lax.broadcasted_iota(I32, qcol.shape, 1)
    outs = []
    for half, sink in ((0, sink_a), (1, sink_b)):
        keep = (lane < HEAD_DIM) if half == 0 else (lane >= HEAD_DIM)
        qm = jnp.where(keep, qcol, jnp.zeros_like(qcol))
        s = lax.dot_general(qm, kexp_b, (((1,), (1,)), ((), ())), preferred_element_type=F32)
        s = jnp.where(mask, s, -jnp.inf)
        m = jnp.maximum(jnp.max(s, -1, keepdims=True), sink)
        p = jnp.exp(s - m)
        denom = jnp.sum(p, -1, keepdims=True) + jnp.exp(sink - m)
        o = jnp.dot(p.astype(BF16), vexp_b, preferred_element_type=F32)
        outs.append(o * (1.0 / denom))
    return jnp.where(lane < HEAD_DIM, outs[0], outs[1])


def _dup_heads(x):
    lane = lax.broadcasted_iota(I32, x.shape, 1)
    xr = pltpu.roll(x, HEAD_DIM, 1)
    low = lane < HEAD_DIM
    return (jnp.where(low, x, xr).astype(BF16), jnp.where(low, xr, x).astype(BF16))


def _prompt_mixer_kernel(sinks_ref, q_ref, kc_ref, kp_ref, vc_ref, vp_ref, uc_ref, up_ref,
                         gb_ref, cw_ref, o_ref):
    first = pl.program_id(1) == 0
    r_i = lax.broadcasted_iota(I32, (WINDOW, 2 * WINDOW), 0)
    c_i = lax.broadcasted_iota(I32, (WINDOW, 2 * WINDOW), 1)
    band = (c_i <= WINDOW + r_i) & (c_i > r_i)
    band_first = band & (c_i >= WINDOW * first.astype(I32))
    for s in range(TQ // WINDOW):
        rows = slice(s * WINDOW, (s + 1) * WINDOW)
        if s == 0:
            kprev, vprev = kp_ref[...], vp_ref[...]
        else:
            prev = slice((s - 1) * WINDOW, s * WINDOW)
            kprev, vprev = kc_ref[prev, :], vc_ref[prev, :]
        kk = jnp.concatenate([kprev, kc_ref[rows, :]], 0)
        vv = jnp.concatenate([vprev, vc_ref[rows, :]], 0)
        kexp = _dup_heads(kk)
        vexp = _dup_heads(vv)
        mask = band_first if s == 0 else band
        for col in range(ATTN_WIDTH // LANES):
            h = col // 2
            cols = slice(col * LANES, (col + 1) * LANES)
            out = _attend_column(q_ref[rows, cols], kexp[h], vexp[h], mask,
                                 sinks_ref[2 * col], sinks_ref[2 * col + 1])
            o_ref[rows, cols] = out.astype(BF16)

    u = uc_ref[...]
    up = up_ref[...]
    zero = jnp.zeros((1, CONV_WIDTH), F32)
    p1 = jnp.where(first, zero, up[SUBLANES - 1:SUBLANES, :])
    p2 = jnp.where(first, zero, up[SUBLANES - 2:SUBLANES - 1, :])
    row = lax.broadcasted_iota(I32, u.shape, 0)
    u1 = jnp.where(row == 0, p1, pltpu.roll(u, 1, 0))
    u2 = jnp.where(row == 0, p2, jnp.where(row == 1, p1, pltpu.roll(u, 2, 0)))
    cw = cw_ref[...]
    y = u2 * cw[0:1, :] + u1 * cw[1:2, :] + u * cw[2:3, :]
    o_ref[:, ATTN_WIDTH:] = (gb_ref[...] * y).astype(BF16)


def _prompt_mixer(sinks, q, k, v, u, gb, conv_w, batch, seq):
    nj = seq // TQ
    tile = lambda b, j: b * nj + j
    cur = lambda w: pl.BlockSpec((TQ, w), lambda b, j: (tile(b, j), 0))
    prev_kv = pl.BlockSpec((WINDOW, KV_WIDTH),
                           lambda b, j: (jnp.maximum(tile(b, j) * (TQ // WINDOW) - 1, 0), 0))
    prev_u = pl.BlockSpec((SUBLANES, CONV_WIDTH),
                          lambda b, j: (jnp.maximum(tile(b, j) * (TQ // SUBLANES) - 1, 0), 0))
    return pl.pallas_call(
        _prompt_mixer_kernel,
        grid=(batch, nj),
        in_specs=[pl.BlockSpec(memory_space=pltpu.SMEM),
                  cur(ATTN_WIDTH), cur(KV_WIDTH), prev_kv, cur(KV_WIDTH), prev_kv,
                  cur(CONV_WIDTH), prev_u, cur(CONV_WIDTH), _full((CONV_K, CONV_WIDTH))],
        out_specs=cur(D_MODEL),
        out_shape=jax.ShapeDtypeStruct((batch * seq, D_MODEL), BF16),
        compiler_params=_cparams(("parallel", "parallel")), name="prompt_mixer",
    )(sinks, q, k, k, v, v, u, u, gb, conv_w)


def _sample_mixer_kernel(sinks_ref, q_ref, kn_ref, vn_ref, kb_ref, vb_ref, u_ref, st_ref, gb_ref,
                         cw_ref, o_ref, nk_ref, nv_ref, *, dec_seq):
    g, w = kb_ref.shape[0], kb_ref.shape[1]
    rows = g * dec_seq
    n_cache = g * w
    n_keys = n_cache + 2 * rows
    kn, vn = kn_ref[...], vn_ref[...]
    pad = jnp.zeros((rows, KV_WIDTH), F32)
    kk = jnp.concatenate([kb_ref[...].reshape(n_cache, KV_WIDTH), kn, pad], 0)
    vv = jnp.concatenate([vb_ref[...].reshape(n_cache, KV_WIDTH), vn, pad], 0)
    kexp = _dup_heads(kk)
    vexp = _dup_heads(vv)

    r_i = lax.broadcasted_iota(I32, (rows, n_keys), 0)
    c_i = lax.broadcasted_iota(I32, (rows, n_keys), 1)
    r_seq, r_pos = _div_pow2(r_i, dec_seq), _mod_pow2(r_i, dec_seq)
    c_new = c_i - n_cache
    in_cache = ((c_i < n_cache) & (_div_pow2(c_i, w) == r_seq)
                & (_mod_pow2(c_i, w) > r_pos + (w - WINDOW)))
    in_new = ((c_new >= 0) & (c_new < rows) & (_div_pow2(c_new, dec_seq) == r_seq)
              & (_mod_pow2(c_new, dec_seq) <= r_pos))
    mask = in_cache | in_new
    for col in range(ATTN_WIDTH // LANES):
        h = col // 2
        cols = slice(col * LANES, (col + 1) * LANES)
        out = _attend_column(q_ref[:, cols], kexp[h], vexp[h], mask,
                             sinks_ref[2 * col], sinks_ref[2 * col + 1])
        o_ref[:, cols] = out.astype(BF16)

    nk_ref[:, 0:w - dec_seq, :] = kb_ref[:, dec_seq:w, :]
    nk_ref[:, w - dec_seq:w, :] = kn.reshape(g, dec_seq, KV_WIDTH)
    nv_ref[:, 0:w - dec_seq, :] = vb_ref[:, dec_seq:w, :]
    nv_ref[:, w - dec_seq:w, :] = vn.reshape(g, dec_seq, KV_WIDTH)

    u = u_ref[...]
    st = st_ref[...]
    pos = lax.broadcasted_iota(I32, u.shape, 0) % dec_seq
    u1 = jnp.where(pos == 0, pltpu.roll(st, rows - 1, 0), pltpu.roll(u, 1, 0))
    u2 = jnp.where(pos < 2, st, pltpu.roll(u, 2, 0))
    cw = cw_ref[...]
    y = u2 * cw[0:1, :] + u1 * cw[1:2, :] + u * cw[2:3, :]
    o_ref[:, ATTN_WIDTH:] = (gb_ref[...] * y).astype(BF16)


def _sample_mixer(sinks, q, k, v, u, gb, conv_w, cache_k, cache_v, state_rows, t_prompt):
    nseq, w = cache_k.shape[0], cache_k.shape[1]
    dec_seq = (q.shape[0] - t_prompt) // nseq
    rows = SEQ_GROUP * dec_seq
    off = t_prompt // rows
    tok = lambda wd: pl.BlockSpec((rows, wd), lambda i: (off + i, 0))
    local = lambda wd: pl.BlockSpec((rows, wd), lambda i: (i, 0))
    cache = pl.BlockSpec((SEQ_GROUP, w, KV_WIDTH), lambda i: (i, 0, 0))
    return pl.pallas_call(
        functools.partial(_sample_mixer_kernel, dec_seq=dec_seq),
        grid=(nseq // SEQ_GROUP,),
        in_specs=[pl.BlockSpec(memory_space=pltpu.SMEM),
                  tok(ATTN_WIDTH), tok(KV_WIDTH), tok(KV_WIDTH), cache, cache,
                  tok(CONV_WIDTH), local(CONV_WIDTH), tok(CONV_WIDTH),
                  _full((CONV_K, CONV_WIDTH))],
        out_specs=[local(D_MODEL), cache, cache],
        out_shape=[jax.ShapeDtypeStruct((nseq * dec_seq, D_MODEL), BF16),
                   jax.ShapeDtypeStruct(cache_k.shape, F32),
                   jax.ShapeDtypeStruct(cache_v.shape, F32)],
        compiler_params=_cparams(("parallel",)), name="sample_mixer",
    )(sinks, q, k, v, cache_k, cache_v, u, state_rows, gb, conv_w)


def _post_mixer_kernel(mixp_ref, mixs_ref, xn_ref, p_ref, wo_ref, bo_ref, g1_ref, b1_ref, wg_ref,
                       bg_ref, wp_ref, wrh_ref, wrl_ref, br_ref,
                       x1_ref, c_ref, eid_ref, gate_ref, rank_ref, cnt_ref, carry_ref, *, n_prompt):
    @pl.when(pl.program_id(0) == 0)
    def _():
        carry_ref[...] = jnp.zeros_like(carry_ref)

    mixed = jnp.where(pl.program_id(0) < n_prompt, mixp_ref[...], mixs_ref[...])
    mix = jnp.dot(mixed, wo_ref[...], preferred_element_type=F32) + bo_ref[...]
    x1 = _layer_norm(DN_ALPHA * xn_ref[...] + mix, g1_ref[...], b1_ref[...])
    x1_ref[...] = x1
    x1h = x1.astype(BF16)
    x1l = (x1 - x1h.astype(F32)).astype(BF16)
    gate = jax.nn.sigmoid(jnp.dot(x1h, wg_ref[...], preferred_element_type=F32) + bg_ref[...])
    ple = jnp.dot(p_ref[...], wp_ref[...], preferred_element_type=F32)
    c_ref[...] = DN_ALPHA * x1 + gate * ple

    nt = (((1,), (1,)), ((), ()))
    logits = (lax.dot_general(wrh_ref[...], x1h, nt, preferred_element_type=F32)
              + lax.dot_general(wrh_ref[...], x1l, nt, preferred_element_type=F32)
              + lax.dot_general(wrl_ref[...], x1h, nt, preferred_element_type=F32)
              + br_ref[...])
    tm = logits.shape[1]
    e_i = lax.broadcasted_iota(I32, logits.shape, 0).astype(F32)
    work = logits
    vals, sels = [], []
    for k in range(TOP_K):
        m = jnp.max(work, 0, keepdims=True)
        idx = jnp.min(jnp.where(work == m, e_i, float(N_EXPERTS)), 0, keepdims=True)
        sel = e_i == idx
        vals.append(m)
        sels.append(sel)
        eid_ref[k:k + 1, :] = idx.astype(I32)
        work = jnp.where(sel, -jnp.inf, work)
    exps = [jnp.exp(v - vals[0]) for v in vals]
    denom = exps[0] + exps[1] + exps[2] + exps[3]
    for k in range(TOP_K):
        gate_ref[k:k + 1, :] = exps[k] / denom

    chosen = jnp.where(sels[0] | sels[1] | sels[2] | sels[3], 1.0, 0.0)
    s_i = lax.broadcasted_iota(I32, (tm, tm), 0)
    t_i = lax.broadcasted_iota(I32, (tm, tm), 1)
    before = jnp.where(s_i < t_i, 1.0, 0.0).astype(BF16)
    pos = carry_ref[:, 0:1] + jnp.dot(chosen.astype(BF16), before, preferred_element_type=F32)
    for k in range(TOP_K):
        rank_ref[k:k + 1, :] = jnp.sum(jnp.where(sels[k], pos, 0.0), 0, keepdims=True).astype(I32)
    carry = carry_ref[...] + jnp.sum(chosen, 1, keepdims=True)
    carry_ref[...] = carry
    cnt_ref[...] = carry


def _post_mixer(mix_p, mix_s, xn, p_b, w_o, b_o, g1, b1, w_gate, b_gate, w_ple, wr_hi, wr_lo, b_r):
    t = xn.shape[0]
    n_p, n_s = mix_p.shape[0] // TM, mix_s.shape[0] // TM
    row = lambda w: pl.BlockSpec((TM, w), lambda i: (i, 0))
    meta = pl.BlockSpec((TOP_K, TM), lambda i: (0, i))
    vec = _full((1, D_MODEL))
    return pl.pallas_call(
        functools.partial(_post_mixer_kernel, n_prompt=n_p),
        grid=(t // TM,),
        in_specs=[pl.BlockSpec((TM, D_MODEL), lambda i: (jnp.minimum(i, n_p - 1), 0)),
                  pl.BlockSpec((TM, D_MODEL), lambda i: (jnp.clip(i - n_p, 0, n_s - 1), 0)),
                  row(D_MODEL), row(PLE_DIM),
                  _full((D_MODEL, D_MODEL)), vec, vec, vec,
                  _full((D_MODEL, D_MODEL)), vec, _full((PLE_DIM, D_MODEL)),
                  _full((N_EXPERTS, D_MODEL)), _full((N_EXPERTS, D_MODEL)), _full((N_EXPERTS, 1))],
        out_specs=[row(D_MODEL), row(D_MODEL), meta, meta, meta, _full((N_EXPERTS, LANES))],
        out_shape=[jax.ShapeDtypeStruct((t, D_MODEL), F32),
                   jax.ShapeDtypeStruct((t, D_MODEL), F32),
                   jax.ShapeDtypeStruct((TOP_K, t), I32),
                   jax.ShapeDtypeStruct((TOP_K, t), F32),
                   jax.ShapeDtypeStruct((TOP_K, t), I32),
                   jax.ShapeDtypeStruct((N_EXPERTS, LANES), F32)],
        scratch_shapes=[pltpu.VMEM((N_EXPERTS, LANES), F32)],
        compiler_params=_cparams(("arbitrary",)), name="post_mixer",
    )(mix_p, mix_s, xn, p_b, w_o, b_o, g1, b1, w_gate, b_gate, w_ple, wr_hi, wr_lo, b_r)


def _row_copy(src, dst, sem):
    return pltpu.make_async_copy(src, dst, sem)


def _dispatch_kernel(dest_ref, x_ref, xs_hbm, sem):
    tm = x_ref.shape[0]

    def issue(r, carry):
        for k in range(TOP_K):
            _row_copy(x_ref.at[pl.ds(r, 1)], xs_hbm.at[pl.ds(dest_ref[k, r], 1)], sem).start()
        return carry

    lax.fori_loop(0, tm, issue, 0)
    for k in range(TOP_K):
        _row_copy(x_ref, xs_hbm.at[pl.ds(0, tm)], sem).wait()


def _dispatch(dest_tiles, x1):
    t = x1.shape[0]
    return pl.pallas_call(
        _dispatch_kernel,
        grid=(t // TM_ROWS,),
        in_specs=[pl.BlockSpec((None, TOP_K, TM_ROWS), lambda i: (i, 0, 0), memory_space=pltpu.SMEM),
                  pl.BlockSpec((TM_ROWS, D_MODEL), lambda i: (i, 0))],
        out_specs=pl.BlockSpec(memory_space=pl.ANY),
        out_shape=jax.ShapeDtypeStruct((t * TOP_K, D_MODEL), F32),
        scratch_shapes=[pltpu.SemaphoreType.DMA],
        compiler_params=_cparams(("arbitrary",)), name="dispatch",
    )(dest_tiles, x1)


def _moe_kernel(tile_ref, exp_ref, lo_ref, hi_ref, first_ref,
                xs_ref, w1_ref, b1_ref, w2_ref, b2_ref, ys_ref, act_ref):
    del tile_ref, exp_ref
    w = pl.program_id(0)
    lo, hi = lo_ref[w], hi_ref[w]

    @pl.when(hi > lo)
    def _():
        xb = xs_ref[...].astype(BF16)
        even = (lax.broadcasted_iota(I32, (xb.shape[0], LANES), 1) & 1) == 0
        for m in range(D_FF // LANES):
            cols = slice(2 * m * LANES, 2 * (m + 1) * LANES)
            h = jnp.dot(xb, w1_ref[:, cols], preferred_element_type=F32) + b1_ref[:, cols]
            ha, hb = h[:, :LANES], h[:, LANES:]
            glu = jnp.where(even, ha, pltpu.roll(hb, 1, 1))
            lin = jnp.where(even, pltpu.roll(ha, LANES - 1, 1), hb)
            glu = jnp.minimum(glu, SWIGLU_LIMIT)
            lin = jnp.clip(lin, -SWIGLU_LIMIT, SWIGLU_LIMIT)
            act = glu * jax.nn.sigmoid(SWIGLU_ALPHA * glu) * (lin + 1.0)
            act_ref[:, m * LANES:(m + 1) * LANES] = act.astype(BF16)
        y = jnp.dot(act_ref[...], w2_ref[...], preferred_element_type=F32) + b2_ref[...]
        r = lax.broadcasted_iota(I32, y.shape, 0)
        mine = (r >= lo) & (r < hi)

        @pl.when(first_ref[w] == 1)
        def _():
            ys_ref[...] = jnp.where(mine, y, 0.0)

        @pl.when(first_ref[w] == 0)
        def _():
            ys_ref[...] = jnp.where(mine, y, ys_ref[...])


def _permute_w2_rows(w2):
    e = w2.shape[0]
    half = LANES // 2
    w = w2.reshape(e, D_FF // LANES, 2, half, D_MODEL)
    return w.transpose(0, 1, 3, 2, 4).reshape(e, D_FF, D_MODEL)


def _moe(sched, xs, w1, b1, w2p, b2):
    a = xs.shape[0]
    n_items = sched[0].shape[0]
    by_expert = lambda r, c: pl.BlockSpec((None, r, c), lambda w, tile, ex, *_: (ex[w], 0, 0))
    rows = pl.BlockSpec((TM_MOE, D_MODEL), lambda w, tile, *_: (tile[w], 0))
    return pl.pallas_call(
        _moe_kernel,
        grid_spec=pltpu.PrefetchScalarGridSpec(
            num_scalar_prefetch=5, grid=(n_items,),
            in_specs=[rows, by_expert(D_MODEL, 2 * D_FF), by_expert(1, 2 * D_FF),
                      by_expert(D_FF, D_MODEL), by_expert(1, D_MODEL)],
            out_specs=rows,
            scratch_shapes=[pltpu.VMEM((TM_MOE, D_FF), BF16)]),
        out_shape=jax.ShapeDtypeStruct((a, D_MODEL), F32),
        compiler_params=_cparams(("arbitrary",)), name="moe",
    )(*sched, xs, w1, b1, w2p, b2)


def _moe_schedule(counts, n_rows):
    n_tiles = n_rows // TM_MOE
    n_items = n_tiles + N_EXPERTS - 1
    pend = jnp.cumsum(counts)
    pstart = pend - counts
    first_tile = pstart // TM_MOE
    last_tile = jnp.maximum(pend - 1, 0) // TM_MOE
    ntile = jnp.where(counts > 0, last_tile - first_tile + 1, 0)
    wend = jnp.cumsum(ntile)
    wstart = wend - ntile
    total = wend[-1]
    w = jnp.arange(n_items, dtype=I32)
    wv = jnp.minimum(w, total - 1)
    ex = jnp.minimum(jnp.sum(wend[None, :] <= wv[:, None], 1), N_EXPERTS - 1).astype(I32)
    tile = (first_tile[ex] + wv - wstart[ex]).astype(I32)
    valid = w < total
    lo = jnp.where(valid, jnp.clip(pstart[ex] - tile * TM_MOE, 0, TM_MOE), 0).astype(I32)
    hi = jnp.where(valid, jnp.clip(pend[ex] - tile * TM_MOE, 0, TM_MOE), 0).astype(I32)
    prev_tile = jnp.concatenate([jnp.full((1,), -1, I32), tile[:-1]])
    first = (valid & (tile != prev_tile)).astype(I32)
    return (tile, ex, lo, hi, first), pstart


def _combine_kernel(dest_ref, c_ref, gate_ref, g2_ref, b2_ref, ys_hbm, o_ref, buf, sem):
    tm = c_ref.shape[0]

    def issue(r, carry):
        for k in range(TOP_K):
            _row_copy(ys_hbm.at[pl.ds(dest_ref[k, r], 1)], buf.at[k, pl.ds(r, 1)], sem).start()
        return carry

    lax.fori_loop(0, tm, issue, 0)
    for k in range(TOP_K):
        _row_copy(ys_hbm.at[pl.ds(0, tm)], buf.at[k], sem).wait()
    acc = c_ref[...]
    gates = gate_ref[...]
    for k in range(TOP_K):
        acc = acc + gates[:, k:k + 1] * buf[k]
    o_ref[...] = _layer_norm(acc, g2_ref[...], b2_ref[...])


def _combine(dest_tiles, c, gates_t, g2, b2, ys):
    t = c.shape[0]
    vec = _full((1, D_MODEL))
    return pl.pallas_call(
        _combine_kernel,
        grid=(t // TM_ROWS,),
        in_specs=[pl.BlockSpec((None, TOP_K, TM_ROWS), lambda i: (i, 0, 0), memory_space=pltpu.SMEM),
                  pl.BlockSpec((TM_ROWS, D_MODEL), lambda i: (i, 0)),
                  pl.BlockSpec((TM_ROWS, TOP_K), lambda i: (i, 0)),
                  vec, vec, pl.BlockSpec(memory_space=pl.ANY)],
        out_specs=pl.BlockSpec((TM_ROWS, D_MODEL), lambda i: (i, 0)),
        out_shape=jax.ShapeDtypeStruct((t, D_MODEL), F32),
        scratch_shapes=[pltpu.VMEM((TOP_K, TM_ROWS, D_MODEL), F32), pltpu.SemaphoreType.DMA],
        compiler_params=_cparams(("arbitrary",)), name="combine",
    )(dest_tiles, c, gates_t, g2, b2, ys)


def kernel(x_prompt, x_sample, cache_k, cache_v, state_conv, p_prompt, p_sample, ln_emb_g, ln_emb_b,
           w_in, b_in, conv_w, sinks, w_o, b_o, ln1_g, ln1_b, w_gate, b_gate, w_ple, w_router,
           b_router, w1, b1, w2, b2, ln2_g, ln2_b):
    batch, seq, _ = x_prompt.shape
    nseq, dec_seq, _ = x_sample.shape
    t_p, t_s = batch * seq, nseq * dec_seq
    t = t_p + t_s
    w_cache = cache_k.shape[2]
    vec = lambda a: a.reshape(1, -1)

    x = jnp.concatenate([x_prompt.reshape(t_p, D_MODEL), x_sample.reshape(t_s, D_MODEL)], 0)
    state_rows = jnp.pad(state_conv, ((0, 0), (0, 0), (0, dec_seq - (CONV_K - 1)), (0, 0)))
    state_rows = state_rows.reshape(DEPTH, t_s, CONV_WIDTH)

    ks_p, vs_p, cs_p, ks_s, vs_s, cs_s = [], [], [], [], [], []
    for l in range(DEPTH):
        w_in_b = w_in[l].astype(BF16)
        outs = _inproj(x, (vec(ln_emb_g), vec(ln_emb_b)), w_in_b, vec(b_in[l]), apply_ln=(l == 0))
        xn, q, k, v, gb, u = outs

        mix_p = _prompt_mixer(sinks[l], q, k, v, u, gb, conv_w[l], batch, seq)
        ck = cache_k[l].reshape(nseq, w_cache, KV_WIDTH)
        cv = cache_v[l].reshape(nseq, w_cache, KV_WIDTH)
        mix_s, nk_s, nv_s = _sample_mixer(sinks[l], q, k, v, u, gb, conv_w[l], ck, cv,
                                          state_rows[l], t_p)

        p_b = jnp.concatenate([p_prompt[l].reshape(t_p, PLE_DIM),
                               p_sample[l].reshape(t_s, PLE_DIM)], 0).astype(BF16)
        wr_t = w_router[l].T
        wr_hi = wr_t.astype(BF16)
        wr_lo = (wr_t - wr_hi.astype(F32)).astype(BF16)
        x1, c, eid, gates, ranks, cnt = _post_mixer(
            mix_p, mix_s, xn, p_b, w_o[l].astype(BF16), vec(b_o[l]), vec(ln1_g[l]), vec(ln1_b[l]),
            w_gate[l].astype(BF16), vec(b_gate[l]), w_ple[l].astype(BF16), wr_hi, wr_lo,
            b_router[l].reshape(N_EXPERTS, 1))

        counts = cnt[:, 0].astype(I32)
        sched, pstart = _moe_schedule(counts, t * TOP_K)
        onehot = eid[:, :, None] == jnp.arange(N_EXPERTS, dtype=I32)
        dest = ranks + jnp.sum(jnp.where(onehot, pstart, 0), -1)
        dest_tiles = dest.reshape(TOP_K, t // TM_ROWS, TM_ROWS).transpose(1, 0, 2)

        xs = _dispatch(dest_tiles, x1)
        ys = _moe(sched, xs, w1[l].astype(BF16), b1[l][:, None, :],
                  _permute_w2_rows(w2[l]).astype(BF16), b2[l][:, None, :])
        x = _combine(dest_tiles, c, gates.T, vec(ln2_g[l]), vec(ln2_b[l]), ys)

        kp = k[:t_p].reshape(batch, seq, N_KV_HEADS, HEAD_DIM)
        vp = v[:t_p].reshape(batch, seq, N_KV_HEADS, HEAD_DIM)
        ks_p.append(kp[:, seq - WINDOW:])
        vs_p.append(vp[:, seq - WINDOW:])
        cs_p.append(u[:t_p].reshape(batch, seq, CONV_WIDTH)[:, seq - (CONV_K - 1):])
        ks_s.append(nk_s.reshape(nseq, w_cache, N_KV_HEADS, HEAD_DIM))
        vs_s.append(nv_s.reshape(nseq, w_cache, N_KV_HEADS, HEAD_DIM))
        cs_s.append(u[t_p:].reshape(nseq, dec_seq, CONV_WIDTH)[:, dec_seq - (CONV_K - 1):])

    y_prompt = x[:t_p].reshape(batch, seq, D_MODEL)
    y_sample = x[t_p:].reshape(nseq, dec_seq, D_MODEL)
    return (y_prompt, y_sample, jnp.stack(ks_p), jnp.stack(vs_p), jnp.stack(cs_p),
            jnp.stack(ks_s), jnp.stack(vs_s), jnp.stack(cs_s))
```

```python
import functools

import jax
import jax.numpy as jnp
from jax import lax
from jax.experimental import pallas as pl
from jax.experimental.pallas import tpu as pltpu

F32 = jnp.float32
BF16 = jnp.bfloat16
I32 = jnp.int32

D_MODEL = 1024
DEPTH = 2
HEAD_DIM = 64
N_Q_HEADS = 8
N_KV_HEADS = 2
ATTN_WIDTH = N_Q_HEADS * HEAD_DIM
KV_WIDTH = N_KV_HEADS * HEAD_DIM
CONV_WIDTH = D_MODEL - ATTN_WIDTH
WINDOW = 128
ATTN_SCALE = HEAD_DIM ** -0.5
CONV_K = 3
N_EXPERTS = 32
TOP_K = 4
D_FF = D_MODEL
SWIGLU_LIMIT = 7.0
SWIGLU_ALPHA = 1.702
PLE_DIM = 256
LN_EPS = 1e-5
DN_ALPHA = (2.0 * DEPTH) ** 0.25
IN_COLS = ATTN_WIDTH + 2 * KV_WIDTH + 3 * CONV_WIDTH
Q0, K0, V0, GB0, GC0, H0 = 0, 512, 640, 768, 1280, 1792

LANES = 128
SUBLANES = 8
VMEM_LIMIT = 48 * 1024 * 1024

TM = 512
TQ = 512
SEQ_GROUP = 8
TM_DISPATCH = 512
TM_COMBINE = 256
TM_MOE = 512


def _layer_norm(x, g, b):
    mu = jnp.mean(x, -1, keepdims=True)
    xc = x - mu
    var = jnp.mean(xc * xc, -1, keepdims=True)
    return xc * lax.rsqrt(var + LN_EPS) * g + b


def _div_pow2(x, n):
    assert n & (n - 1) == 0
    return lax.shift_right_arithmetic(x, n.bit_length() - 1)


def _mod_pow2(x, n):
    assert n & (n - 1) == 0
    return x & (n - 1)


def _cparams(sem):
    return pltpu.CompilerParams(dimension_semantics=sem, vmem_limit_bytes=VMEM_LIMIT)


def _full(shape):
    return pl.BlockSpec(shape, lambda *_: (0,) * len(shape))


def _inproj_kernel(*refs, apply_ln):
    if apply_ln:
        x_ref, g_ref, b_ref, w_ref, bi_ref, xn_ref, q_ref, k_ref, v_ref, gb_ref, u_ref = refs
    else:
        x_ref, w_ref, bi_ref, q_ref, k_ref, v_ref, gb_ref, u_ref = refs
    x = x_ref[...]
    if apply_ln:
        x = _layer_norm(x, g_ref[...], b_ref[...])
        xn_ref[...] = x
    xb = x.astype(BF16)

    def proj(lo, hi):
        return jnp.dot(xb, w_ref[:, lo:hi], preferred_element_type=F32) + bi_ref[:, lo:hi]

    q_ref[...] = (proj(Q0, K0) * ATTN_SCALE).astype(BF16)
    k_ref[...] = proj(K0, V0)
    v_ref[...] = proj(V0, GB0)
    gb_ref[...] = proj(GB0, GC0)
    u_ref[...] = proj(GC0, H0) * proj(H0, IN_COLS)


def _inproj(x, ln, w_in_b, b_in, apply_ln):
    t = x.shape[0]
    row = lambda w: pl.BlockSpec((TM, w), lambda i: (i, 0))
    in_specs = [row(D_MODEL)]
    args = [x]
    if apply_ln:
        in_specs += [_full((1, D_MODEL)), _full((1, D_MODEL))]
        args += [ln[0], ln[1]]
    in_specs += [_full((D_MODEL, IN_COLS)), _full((1, IN_COLS))]
    args += [w_in_b, b_in]
    out_shape, out_specs = [], []
    if apply_ln:
        out_shape.append(jax.ShapeDtypeStruct((t, D_MODEL), F32))
        out_specs.append(row(D_MODEL))
    out_shape += [jax.ShapeDtypeStruct((t, ATTN_WIDTH), BF16),
                  jax.ShapeDtypeStruct((t, KV_WIDTH), F32),
                  jax.ShapeDtypeStruct((t, KV_WIDTH), F32),
                  jax.ShapeDtypeStruct((t, CONV_WIDTH), F32),
                  jax.ShapeDtypeStruct((t, CONV_WIDTH), F32)]
    out_specs += [row(ATTN_WIDTH), row(KV_WIDTH), row(KV_WIDTH), row(CONV_WIDTH), row(CONV_WIDTH)]
    outs = pl.pallas_call(
        functools.partial(_inproj_kernel, apply_ln=apply_ln),
        grid=(t // TM,), in_specs=in_specs, out_specs=out_specs, out_shape=out_shape,
        compiler_params=_cparams(("parallel",)), name="inproj")(*args)
    if apply_ln:
        return outs
    return [x] + list(outs)


def _attend_column(qcol, kexp_b, vexp_b, mask, sink_a, sink_b):
    lane = lax.broadcasted_iota(I32, qcol.shape, 1)
    outs = []
    for half, sink in ((0, sink_a), (1, sink_b)):
        keep = (lane < HEAD_DIM) if half == 0 else (lane >= HEAD_DIM)
        qm = jnp.where(keep, qcol, jnp.zeros_like(qcol))
        s = lax.dot_general(qm, kexp_b, (((1,), (1,)), ((), ())), preferred_element_type=F32)
        s = jnp.where(mask, s, -jnp.inf)
        m = jnp.maximum(jnp.max(s, -1, keepdims=True), sink)
        p = jnp.exp(s - m)
        denom = jnp.sum(p, -1, keepdims=True) + jnp.exp(sink - m)
        o = jnp.dot(p.astype(BF16), vexp_b, preferred_element_type=F32)
        outs.append(o * (1.0 / denom))
    return jnp.where(lane < HEAD_DIM, outs[0], outs[1])


def _dup_heads(x):
    lane = lax.broadcasted_iota(I32, x.shape, 1)
    xr = pltpu.roll(x, HEAD_DIM, 1)
    low = lane < HEAD_DIM
    return (jnp.where(low, x, xr).astype(BF16), jnp.where(low, xr, x).astype(BF16))


def _prompt_mixer_kernel(sinks_ref, q_ref, kc_ref, kp_ref, vc_ref, vp_ref, uc_ref, up_ref,
                         gb_ref, cw_ref, o_ref):
    first = pl.program_id(1) == 0
    r_i = lax.broadcasted_iota(I32, (WINDOW, 2 * WINDOW), 0)
    c_i = lax.broadcasted_iota(I32, (WINDOW, 2 * WINDOW), 1)
    band = (c_i <= WINDOW + r_i) & (c_i > r_i)
    band_first = band & (c_i >= WINDOW * first.astype(I32))
    for s in range(TQ // WINDOW):
        rows = slice(s * WINDOW, (s + 1) * WINDOW)
        if s == 0:
            kprev, vprev = kp_ref[...], vp_ref[...]
        else:
            prev = slice((s - 1) * WINDOW, s * WINDOW)
            kprev, vprev = kc_ref[prev, :], vc_ref[prev, :]
        kk = jnp.concatenate([kprev, kc_ref[rows, :]], 0)
        vv = jnp.concatenate([vprev, vc_ref[rows, :]], 0)
        kexp = _dup_heads(kk)
        vexp = _dup_heads(vv)
        mask = band_first if s == 0 else band
        for col in range(ATTN_WIDTH // LANES):
            h = col // 2
            cols = slice(col * LANES, (col + 1) * LANES)
            out = _attend_column(q_ref[rows, cols], kexp[h], vexp[h], mask,
                                 sinks_ref[2 * col], sinks_ref[2 * col + 1])
            o_ref[rows, cols] = out.astype(BF16)

    u = uc_ref[...]
    up = up_ref[...]
    zero = jnp.zeros((1, CONV_WIDTH), F32)
    p1 = jnp.where(first, zero, up[SUBLANES - 1:SUBLANES, :])
    p2 = jnp.where(first, zero, up[SUBLANES - 2:SUBLANES - 1, :])
    row = lax.broadcasted_iota(I32, u.shape, 0)
    u1 = jnp.where(row == 0, p1, pltpu.roll(u, 1, 0))
    u2 = jnp.where(row == 0, p2, jnp.where(row == 1, p1, pltpu.roll(u, 2, 0)))
    cw = cw_ref[...]
    y = u2 * cw[0:1, :] + u1 * cw[1:2, :] + u * cw[2:3, :]
    o_ref[:, ATTN_WIDTH:] = (gb_ref[...] * y).astype(BF16)


def _prompt_mixer(sinks, q, k, v, u, gb, conv_w, batch, seq):
    nj = seq // TQ
    tile = lambda b, j: b * nj + j
    cur = lambda w: pl.BlockSpec((TQ, w), lambda b, j: (tile(b, j), 0))
    prev_kv = pl.BlockSpec((WINDOW, KV_WIDTH),
                           lambda b, j: (jnp.maximum(tile(b, j) * (TQ // WINDOW) - 1, 0), 0))
    prev_u = pl.BlockSpec((SUBLANES, CONV_WIDTH),
                          lambda b, j: (jnp.maximum(tile(b, j) * (TQ // SUBLANES) - 1, 0), 0))
    return pl.pallas_call(
        _prompt_mixer_kernel,
        grid=(batch, nj),
        in_specs=[pl.BlockSpec(memory_space=pltpu.SMEM),
                  cur(ATTN_WIDTH), cur(KV_WIDTH), prev_kv, cur(KV_WIDTH), prev_kv,
                  cur(CONV_WIDTH), prev_u, cur(CONV_WIDTH), _full((CONV_K, CONV_WIDTH))],
        out_specs=cur(D_MODEL),
        out_shape=jax.ShapeDtypeStruct((batch * seq, D_MODEL), BF16),
        compiler_params=_cparams(("parallel", "parallel")), name="prompt_mixer",
    )(sinks, q, k, k, v, v, u, u, gb, conv_w)


def _sample_mixer_kernel(sinks_ref, q_ref, kn_ref, vn_ref, kb_ref, vb_ref, u_ref, st_ref, gb_ref,
                         cw_ref, o_ref, nk_ref, nv_ref, *, dec_seq):
    g, w = kb_ref.shape[0], kb_ref.shape[1]
    rows = g * dec_seq
    n_cache = g * w
    n_keys = n_cache + 2 * rows
    kn, vn = kn_ref[...], vn_ref[...]
    pad = jnp.zeros((rows, KV_WIDTH), F32)
    kk = jnp.concatenate([kb_ref[...].reshape(n_cache, KV_WIDTH), kn, pad], 0)
    vv = jnp.concatenate([vb_ref[...].reshape(n_cache, KV_WIDTH), vn, pad], 0)
    kexp = _dup_heads(kk)
    vexp = _dup_heads(vv)

    r_i = lax.broadcasted_iota(I32, (rows, n_keys), 0)
    c_i = lax.broadcasted_iota(I32, (rows, n_keys), 1)
    r_seq, r_pos = _div_pow2(r_i, dec_seq), _mod_pow2(r_i, dec_seq)
    c_new = c_i - n_cache
    in_cache = ((c_i < n_cache) & (_div_pow2(c_i, w) == r_seq)
                & (_mod_pow2(c_i, w) > r_pos + (w - WINDOW)))
    in_new = ((c_new >= 0) & (c_new < rows) & (_div_pow2(c_new, dec_seq) == r_seq)
              & (_mod_pow2(c_new, dec_seq) <= r_pos))
    mask = in_cache | in_new
    for col in range(ATTN_WIDTH // LANES):
        h = col // 2
        cols = slice(col * LANES, (col + 1) * LANES)
        out = _attend_column(q_ref[:, cols], kexp[h], vexp[h], mask,
                             sinks_ref[2 * col], sinks_ref[2 * col + 1])
        o_ref[:, cols] = out.astype(BF16)

    nk_ref[:, 0:w - dec_seq, :] = kb_ref[:, dec_seq:w, :]
    nk_ref[:, w - dec_seq:w, :] = kn.reshape(g, dec_seq, KV_WIDTH)
    nv_ref[:, 0:w - dec_seq, :] = vb_ref[:, dec_seq:w, :]
    nv_ref[:, w - dec_seq:w, :] = vn.reshape(g, dec_seq, KV_WIDTH)

    u = u_ref[...]
    st = st_ref[...]
    pos = lax.broadcasted_iota(I32, u.shape, 0) % dec_seq
    u1 = jnp.where(pos == 0, pltpu.roll(st, rows - 1, 0), pltpu.roll(u, 1, 0))
    u2 = jnp.where(pos < 2, st, pltpu.roll(u, 2, 0))
    cw = cw_ref[...]
    y = u2 * cw[0:1, :] + u1 * cw[1:2, :] + u * cw[2:3, :]
    o_ref[:, ATTN_WIDTH:] = (gb_ref[...] * y).astype(BF16)


def _sample_mixer(sinks, q, k, v, u, gb, conv_w, cache_k, cache_v, state_rows, t_prompt):
    nseq, w = cache_k.shape[0], cache_k.shape[1]
    dec_seq = (q.shape[0] - t_prompt) // nseq
    rows = SEQ_GROUP * dec_seq
    off = t_prompt // rows
    tok = lambda wd: pl.BlockSpec((rows, wd), lambda i: (off + i, 0))
    local = lambda wd: pl.BlockSpec((rows, wd), lambda i: (i, 0))
    cache = pl.BlockSpec((SEQ_GROUP, w, KV_WIDTH), lambda i: (i, 0, 0))
    return pl.pallas_call(
        functools.partial(_sample_mixer_kernel, dec_seq=dec_seq),
        grid=(nseq // SEQ_GROUP,),
        in_specs=[pl.BlockSpec(memory_space=pltpu.SMEM),
                  tok(ATTN_WIDTH), tok(KV_WIDTH), tok(KV_WIDTH), cache, cache,
                  tok(CONV_WIDTH), local(CONV_WIDTH), tok(CONV_WIDTH),
                  _full((CONV_K, CONV_WIDTH))],
        out_specs=[local(D_MODEL), cache, cache],
        out_shape=[jax.ShapeDtypeStruct((nseq * dec_seq, D_MODEL), BF16),
                   jax.ShapeDtypeStruct(cache_k.shape, F32),
                   jax.ShapeDtypeStruct(cache_v.shape, F32)],
        compiler_params=_cparams(("parallel",)), name="sample_mixer",
    )(sinks, q, k, v, cache_k, cache_v, u, state_rows, gb, conv_w)


def _post_mixer_kernel(mixp_ref, mixs_ref, xn_ref, p_ref, wo_ref, bo_ref, g1_ref, b1_ref, wg_ref,
                       bg_ref, wp_ref, wrh_ref, wrl_ref, br_ref,
                       x1_ref, c_ref, eid_ref, gate_ref, rank_ref, cnt_ref, carry_ref, *, n_prompt):
    @pl.when(pl.program_id(0) == 0)
    def _():
        carry_ref[...] = jnp.zeros_like(carry_ref)

    mixed = jnp.where(pl.program_id(0) < n_prompt, mixp_ref[...], mixs_ref[...])
    mix = jnp.dot(mixed, wo_ref[...], preferred_element_type=F32) + bo_ref[...]
    x1 = _layer_norm(DN_ALPHA * xn_ref[...] + mix, g1_ref[...], b1_ref[...])
    x1_ref[...] = x1
    x1h = x1.astype(BF16)
    x1l = (x1 - x1h.astype(F32)).astype(BF16)
    gate = jax.nn.sigmoid(jnp.dot(x1h, wg_ref[...], preferred_element_type=F32) + bg_ref[...])
    ple = jnp.dot(p_ref[...], wp_ref[...], preferred_element_type=F32)
    c_ref[...] = DN_ALPHA * x1 + gate * ple

    nt = (((1,), (1,)), ((), ()))
    logits = (lax.dot_general(wrh_ref[...], x1h, nt, preferred_element_type=F32)
              + lax.dot_general(wrh_ref[...], x1l, nt, preferred_element_type=F32)
              + lax.dot_general(wrl_ref[...], x1h, nt, preferred_element_type=F32)
              + br_ref[...])
    tm = logits.shape[1]
    e_i = lax.broadcasted_iota(I32, logits.shape, 0).astype(F32)
    work = logits
    vals, sels = [], []
    for k in range(TOP_K):
        m = jnp.max(work, 0, keepdims=True)
        idx = jnp.min(jnp.where(work == m, e_i, float(N_EXPERTS)), 0, keepdims=True)
        sel = e_i == idx
        vals.append(m)
        sels.append(sel)
        eid_ref[k:k + 1, :] = idx.astype(I32)
        work = jnp.where(sel, -jnp.inf, work)
    exps = [jnp.exp(v - vals[0]) for v in vals]
    denom = exps[0] + exps[1] + exps[2] + exps[3]
    for k in range(TOP_K):
        gate_ref[k:k + 1, :] = exps[k] / denom

    chosen = jnp.where(sels[0] | sels[1] | sels[2] | sels[3], 1.0, 0.0)
    s_i = lax.broadcasted_iota(I32, (tm, tm), 0)
    t_i = lax.broadcasted_iota(I32, (tm, tm), 1)
    before = jnp.where(s_i < t_i, 1.0, 0.0).astype(BF16)
    pos = carry_ref[:, 0:1] + jnp.dot(chosen.astype(BF16), before, preferred_element_type=F32)
    for k in range(TOP_K):
        rank_ref[k:k + 1, :] = jnp.sum(jnp.where(sels[k], pos, 0.0), 0, keepdims=True).astype(I32)
    carry = carry_ref[...] + jnp.sum(chosen, 1, keepdims=True)
    carry_ref[...] = carry
    cnt_ref[...] = carry


def _post_mixer(mix_p, mix_s, xn, p_b, w_o, b_o, g1, b1, w_gate, b_gate, w_ple, wr_hi, wr_lo, b_r):
    t = xn.shape[0]
    n_p, n_s = mix_p.shape[0] // TM, mix_s.shape[0] // TM
    row = lambda w: pl.BlockSpec((TM, w), lambda i: (i, 0))
    meta = pl.BlockSpec((TOP_K, TM), lambda i: (0, i))
    vec = _full((1, D_MODEL))
    return pl.pallas_call(
        functools.partial(_post_mixer_kernel, n_prompt=n_p),
        grid=(t // TM,),
        in_specs=[pl.BlockSpec((TM, D_MODEL), lambda i: (jnp.minimum(i, n_p - 1), 0)),
                  pl.BlockSpec((TM, D_MODEL), lambda i: (jnp.clip(i - n_p, 0, n_s - 1), 0)),
                  row(D_MODEL), row(PLE_DIM),
                  _full((D_MODEL, D_MODEL)), vec, vec, vec,
                  _full((D_MODEL, D_MODEL)), vec, _full((PLE_DIM, D_MODEL)),
                  _full((N_EXPERTS, D_MODEL)), _full((N_EXPERTS, D_MODEL)), _full((N_EXPERTS, 1))],
        out_specs=[row(D_MODEL), row(D_MODEL), meta, meta, meta, _full((N_EXPERTS, LANES))],
        out_shape=[jax.ShapeDtypeStruct((t, D_MODEL), F32),
                   jax.ShapeDtypeStruct((t, D_MODEL), F32),
                   jax.ShapeDtypeStruct((TOP_K, t), I32),
                   jax.ShapeDtypeStruct((TOP_K, t), F32),
                   jax.ShapeDtypeStruct((TOP_K, t), I32),
                   jax.ShapeDtypeStruct((N_EXPERTS, LANES), F32)],
        scratch_shapes=[pltpu.VMEM((N_EXPERTS, LANES), F32)],
        compiler_params=_cparams(("arbitrary",)), name="post_mixer",
    )(mix_p, mix_s, xn, p_b, w_o, b_o, g1, b1, w_gate, b_gate, w_ple, wr_hi, wr_lo, b_r)


def _row_copy(src, dst, sem):
    return pltpu.make_async_copy(src, dst, sem)


def _for_each_row_slot(groups, fn):
    def group(j, carry):
        for s in range(SUBLANES):
            for k in range(TOP_K):
                fn(k, j, s, s * TOP_K + k)
        return carry

    lax.fori_loop(0, groups, group, 0)


def _hbm_row(ref, d):
    return ref.at[lax.shift_right_logical(d, SUBLANES.bit_length() - 1),
                  pl.ds(d & (SUBLANES - 1), 1)]


def _grouped_rows(x):
    return x.reshape(x.shape[0] // SUBLANES, SUBLANES, x.shape[1])


def _dispatch_kernel(dest_ref, x_ref, xs_hbm, sem):
    groups = x_ref.shape[0]
    tm = groups * SUBLANES

    def scatter_row(k, j, s, n):
        d = dest_ref[0, k * tm + j * SUBLANES + s]
        _row_copy(x_ref.at[j, pl.ds(s, 1)], _hbm_row(xs_hbm, d), sem).start(priority=n % 2)

    _for_each_row_slot(groups, scatter_row)
    for k in range(TOP_K):
        _row_copy(x_ref, xs_hbm.at[pl.ds(0, groups)], sem).wait()


def _dest_tiles(dest, tm):
    t = dest.shape[1]
    return dest.reshape(TOP_K, t // tm, tm).transpose(1, 0, 2).reshape(t // tm, 1, TOP_K * tm)


def _dispatch(dest, x1):
    t = x1.shape[0]
    tm = TM_DISPATCH
    xs = pl.pallas_call(
        _dispatch_kernel,
        grid=(t // tm,),
        in_specs=[pl.BlockSpec((None, 1, TOP_K * tm), lambda i: (i, 0, 0), memory_space=pltpu.SMEM),
                  pl.BlockSpec((tm // SUBLANES, SUBLANES, D_MODEL), lambda i: (i, 0, 0))],
        out_specs=pl.BlockSpec(memory_space=pl.ANY),
        out_shape=jax.ShapeDtypeStruct((t * TOP_K // SUBLANES, SUBLANES, D_MODEL), F32),
        scratch_shapes=[pltpu.SemaphoreType.DMA],
        compiler_params=_cparams(("arbitrary",)), name="dispatch",
    )(_dest_tiles(dest, tm), _grouped_rows(x1))
    return xs.reshape(t * TOP_K, D_MODEL)


def _moe_kernel(tile_ref, exp_ref, lo_ref, hi_ref, first_ref,
                xs_ref, w1_ref, b1_ref, w2_ref, b2_ref, ys_ref, act_ref):
    del tile_ref, exp_ref
    w = pl.program_id(0)
    lo, hi = lo_ref[w], hi_ref[w]

    @pl.when(hi > lo)
    def _():
        xb = xs_ref[...].astype(BF16)
        even = (lax.broadcasted_iota(I32, (xb.shape[0], LANES), 1) & 1) == 0
        for m in range(D_FF // LANES):
            cols = slice(2 * m * LANES, 2 * (m + 1) * LANES)
            h = jnp.dot(xb, w1_ref[:, cols], preferred_element_type=F32) + b1_ref[:, cols]
            ha, hb = h[:, :LANES], h[:, LANES:]
            glu = jnp.where(even, ha, pltpu.roll(hb, 1, 1))
            lin = jnp.where(even, pltpu.roll(ha, LANES - 1, 1), hb)
            glu = jnp.minimum(glu, SWIGLU_LIMIT)
            lin = jnp.clip(lin, -SWIGLU_LIMIT, SWIGLU_LIMIT)
            act = glu * jax.nn.sigmoid(SWIGLU_ALPHA * glu) * (lin + 1.0)
            act_ref[:, m * LANES:(m + 1) * LANES] = act.astype(BF16)
        y = jnp.dot(act_ref[...], w2_ref[...], preferred_element_type=F32) + b2_ref[...]
        r = lax.broadcasted_iota(I32, y.shape, 0)
        mine = (r >= lo) & (r < hi)

        @pl.when(first_ref[w] == 1)
        def _():
            ys_ref[...] = jnp.where(mine, y, 0.0)

        @pl.when(first_ref[w] == 0)
        def _():
            ys_ref[...] = jnp.where(mine, y, ys_ref[...])


def _permute_w2_rows(w2):
    e = w2.shape[0]
    half = LANES // 2
    w = w2.reshape(e, D_FF // LANES, 2, half, D_MODEL)
    return w.transpose(0, 1, 3, 2, 4).reshape(e, D_FF, D_MODEL)


def _moe(sched, xs, w1, b1, w2p, b2):
    a = xs.shape[0]
    n_items = sched[0].shape[0]
    by_expert = lambda r, c: pl.BlockSpec((None, r, c), lambda w, tile, ex, *_: (ex[w], 0, 0))
    rows = pl.BlockSpec((TM_MOE, D_MODEL), lambda w, tile, *_: (tile[w], 0))
    return pl.pallas_call(
        _moe_kernel,
        grid_spec=pltpu.PrefetchScalarGridSpec(
            num_scalar_prefetch=5, grid=(n_items,),
            in_specs=[rows, by_expert(D_MODEL, 2 * D_FF), by_expert(1, 2 * D_FF),
                      by_expert(D_FF, D_MODEL), by_expert(1, D_MODEL)],
            out_specs=rows,
            scratch_shapes=[pltpu.VMEM((TM_MOE, D_FF), BF16)]),
        out_shape=jax.ShapeDtypeStruct((a, D_MODEL), F32),
        compiler_params=_cparams(("arbitrary",)), name="moe",
    )(*sched, xs, w1, b1, w2p, b2)


def _moe_schedule(counts, n_rows):
    n_tiles = n_rows // TM_MOE
    n_items = n_tiles + N_EXPERTS - 1
    pend = jnp.cumsum(counts)
    pstart = pend - counts
    first_tile = pstart // TM_MOE
    last_tile = jnp.maximum(pend - 1, 0) // TM_MOE
    ntile = jnp.where(counts > 0, last_tile - first_tile + 1, 0)
    wend = jnp.cumsum(ntile)
    wstart = wend - ntile
    total = wend[-1]
    w = jnp.arange(n_items, dtype=I32)
    wv = jnp.minimum(w, total - 1)
    ex = jnp.minimum(jnp.sum(wend[None, :] <= wv[:, None], 1), N_EXPERTS - 1).astype(I32)
    tile = (first_tile[ex] + wv - wstart[ex]).astype(I32)
    valid = w < total
    lo = jnp.where(valid, jnp.clip(pstart[ex] - tile * TM_MOE, 0, TM_MOE), 0).astype(I32)
    hi = jnp.where(valid, jnp.clip(pend[ex] - tile * TM_MOE, 0, TM_MOE), 0).astype(I32)
    prev_tile = jnp.concatenate([jnp.full((1,), -1, I32), tile[:-1]])
    first = (valid & (tile != prev_tile)).astype(I32)
    return (tile, ex, lo, hi, first), pstart


def _combine_kernel(dcur_ref, dnext_ref, c_ref, gate_ref, g2_ref, b2_ref, ys_hbm, o_ref, buf, sem):
    tm = c_ref.shape[0]
    groups = tm // SUBLANES
    i = pl.program_id(0)

    def gather_tile(dest_ref, slot):
        def gather_row(k, j, s, n):
            d = dest_ref[0, k * tm + j * SUBLANES + s]
            _row_copy(_hbm_row(ys_hbm, d), buf.at[slot, k, j, pl.ds(s, 1)],
                      sem.at[slot]).start(priority=n % 2)

        _for_each_row_slot(groups, gather_row)

    @pl.when(i == 0)
    def _():
        gather_tile(dcur_ref, 0)

    @pl.when(i + 1 < pl.num_programs(0))
    def _():
        gather_tile(dnext_ref, (i + 1) % 2)

    slot = i % 2
    for k in range(TOP_K):
        _row_copy(ys_hbm.at[pl.ds(0, groups)], buf.at[slot, k], sem.at[slot]).wait()
    acc = c_ref[...]
    gates = gate_ref[...]
    for k in range(TOP_K):
        acc = acc + gates[:, k:k + 1] * buf[slot, k].reshape(tm, D_MODEL)
    o_ref[...] = _layer_norm(acc, g2_ref[...], b2_ref[...])


def _combine(dest, c, gates_t, g2, b2, ys):
    t = c.shape[0]
    tm = TM_COMBINE
    n = t // tm
    vec = _full((1, D_MODEL))
    dest_spec = lambda f: pl.BlockSpec((None, 1, TOP_K * tm), lambda i: (f(i), 0, 0),
                                       memory_space=pltpu.SMEM)
    dest_tiles = _dest_tiles(dest, tm)
    return pl.pallas_call(
        _combine_kernel,
        grid=(n,),
        in_specs=[dest_spec(lambda i: i), dest_spec(lambda i: jnp.minimum(i + 1, n - 1)),
                  pl.BlockSpec((tm, D_MODEL), lambda i: (i, 0)),
                  pl.BlockSpec((tm, TOP_K), lambda i: (i, 0)),
                  vec, vec, pl.BlockSpec(memory_space=pl.ANY)],
        out_specs=pl.BlockSpec((tm, D_MODEL), lambda i: (i, 0)),
        out_shape=jax.ShapeDtypeStruct((t, D_MODEL), F32),
        scratch_shapes=[pltpu.VMEM((2, TOP_K, tm // SUBLANES, SUBLANES, D_MODEL), F32),
                        pltpu.SemaphoreType.DMA((2,))],
        compiler_params=_cparams(("arbitrary",)), name="combine",
    )(dest_tiles, dest_tiles, c, gates_t, g2, b2, _grouped_rows(ys))


def kernel(x_prompt, x_sample, cache_k, cache_v, state_conv, p_prompt, p_sample, ln_emb_g, ln_emb_b,
           w_in, b_in, conv_w, sinks, w_o, b_o, ln1_g, ln1_b, w_gate, b_gate, w_ple, w_router,
           b_router, w1, b1, w2, b2, ln2_g, ln2_b):
    batch, seq, _ = x_prompt.shape
    nseq, dec_seq, _ = x_sample.shape
    t_p, t_s = batch * seq, nseq * dec_seq
    t = t_p + t_s
    w_cache = cache_k.shape[2]
    vec = lambda a: a.reshape(1, -1)

    x = jnp.concatenate([x_prompt.reshape(t_p, D_MODEL), x_sample.reshape(t_s, D_MODEL)], 0)
    state_rows = jnp.pad(state_conv, ((0, 0), (0, 0), (0, dec_seq - (CONV_K - 1)), (0, 0)))
    state_rows = state_rows.reshape(DEPTH, t_s, CONV_WIDTH)

    ks_p, vs_p, cs_p, ks_s, vs_s, cs_s = [], [], [], [], [], []
    for l in range(DEPTH):
        w_in_b = w_in[l].astype(BF16)
        outs = _inproj(x, (vec(ln_emb_g), vec(ln_emb_b)), w_in_b, vec(b_in[l]), apply_ln=(l == 0))
        xn, q, k, v, gb, u = outs

        mix_p = _prompt_mixer(sinks[l], q, k, v, u, gb, conv_w[l], batch, seq)
        ck = cache_k[l].reshape(nseq, w_cache, KV_WIDTH)
        cv = cache_v[l].reshape(nseq, w_cache, KV_WIDTH)
        mix_s, nk_s, nv_s = _sample_mixer(sinks[l], q, k, v, u, gb, conv_w[l], ck, cv,
                                          state_rows[l], t_p)

        p_b = jnp.concatenate([p_prompt[l].reshape(t_p, PLE_DIM),
                               p_sample[l].reshape(t_s, PLE_DIM)], 0).astype(BF16)
        wr_t = w_router[l].T
        wr_hi = wr_t.astype(BF16)
        wr_lo = (wr_t - wr_hi.astype(F32)).astype(BF16)
        x1, c, eid, gates, ranks, cnt = _post_mixer(
            mix_p, mix_s, xn, p_b, w_o[l].astype(BF16), vec(b_o[l]), vec(ln1_g[l]), vec(ln1_b[l]),
            w_gate[l].astype(BF16), vec(b_gate[l]), w_ple[l].astype(BF16), wr_hi, wr_lo,
            b_router[l].reshape(N_EXPERTS, 1))

        counts = cnt[:, 0].astype(I32)
        sched, pstart = _moe_schedule(counts, t * TOP_K)
        onehot = eid[:, :, None] == jnp.arange(N_EXPERTS, dtype=I32)
        dest = ranks + jnp.sum(jnp.where(onehot, pstart, 0), -1)

        xs = _dispatch(dest, x1)
        ys = _moe(sched, xs, w1[l].astype(BF16), b1[l][:, None, :],
                  _permute_w2_rows(w2[l]).astype(BF16), b2[l][:, None, :])
        x = _combine(dest, c, gates.T, vec(ln2_g[l]), vec(ln2_b[l]), ys)

        kp = k[:t_p].reshape(batch, seq, N_KV_HEADS, HEAD_DIM)
        vp = v[:t_p].reshape(batch, seq, N_KV_HEADS, HEAD_DIM)
        ks_p.append(kp[:, seq - WINDOW:])
        vs_p.append(vp[:, seq - WINDOW:])
        cs_p.append(u[:t_p].reshape(batch, seq, CONV_WIDTH)[:, seq - (CONV_K - 1):])
        ks_s.append(nk_s.reshape(nseq, w_cache, N_KV_HEADS, HEAD_DIM))
        vs_s.append(nv_s.reshape(nseq, w_cache, N_KV_HEADS, HEAD_DIM))
        cs_s.append(u[t_p:].reshape(nseq, dec_seq, CONV_WIDTH)[:, dec_seq - (CONV_K - 1):])

    y_prompt = x[:t_p].reshape(batch, seq, D_MODEL)
    y_sample = x[t_p:].reshape(nseq, dec_seq, D_MODEL)
    return (y_prompt, y_sample, jnp.stack(ks_p), jnp.stack(vs_p), jnp.stack(cs_p),
            jnp.stack(ks_s), jnp.stack(vs_s), jnp.stack(cs_s))
```

```python
import functools

import jax
import jax.numpy as jnp
from jax import lax
from jax.experimental import pallas as pl
from jax.experimental.pallas import tpu as pltpu

F32 = jnp.float32
BF16 = jnp.bfloat16
I32 = jnp.int32

D_MODEL = 1024
DEPTH = 2
HEAD_DIM = 64
N_Q_HEADS = 8
N_KV_HEADS = 2
ATTN_WIDTH = N_Q_HEADS * HEAD_DIM
KV_WIDTH = N_KV_HEADS * HEAD_DIM
CONV_WIDTH = D_MODEL - ATTN_WIDTH
WINDOW = 128
ATTN_SCALE = HEAD_DIM ** -0.5
CONV_K = 3
N_EXPERTS = 32
TOP_K = 4
D_FF = D_MODEL
SWIGLU_LIMIT = 7.0
SWIGLU_ALPHA = 1.702
PLE_DIM = 256
LN_EPS = 1e-5
DN_ALPHA = (2.0 * DEPTH) ** 0.25
IN_COLS = ATTN_WIDTH + 2 * KV_WIDTH + 3 * CONV_WIDTH
Q0, K0, V0, GB0, GC0, H0 = 0, 512, 640, 768, 1280, 1792

LANES = 128
SUBLANES = 8
VMEM_LIMIT = 48 * 1024 * 1024
VMEM_LIMIT_MOE = 56 * 1024 * 1024

TM = 512
TQ = 512
SEQ_GROUP = 8
TM_DISPATCH = 512
TM_COMBINE = 256
TM_MOE = 512


def _layer_norm(x, g, b):
    mu = jnp.mean(x, -1, keepdims=True)
    xc = x - mu
    var = jnp.mean(xc * xc, -1, keepdims=True)
    return xc * lax.rsqrt(var + LN_EPS) * g + b


def _div_pow2(x, n):
    assert n & (n - 1) == 0
    return lax.shift_right_arithmetic(x, n.bit_length() - 1)


def _mod_pow2(x, n):
    assert n & (n - 1) == 0
    return x & (n - 1)


def _cparams(sem):
    return pltpu.CompilerParams(dimension_semantics=sem, vmem_limit_bytes=VMEM_LIMIT)


def _full(shape):
    return pl.BlockSpec(shape, lambda *_: (0,) * len(shape))


def _stream_specs(n_p, n_s, tm, width):
    return [pl.BlockSpec((tm, width), lambda i: (jnp.minimum(i, n_p - 1), 0)),
            pl.BlockSpec((tm, width), lambda i: (jnp.clip(i - n_p, 0, n_s - 1), 0))]


def _stream_tile(n_p, prompt_ref, sample_ref):
    return jnp.where(pl.program_id(0) < n_p, prompt_ref[...], sample_ref[...])


def _inproj_kernel(*refs, n_prompt):
    if n_prompt is not None:
        xp_ref, xs_ref, g_ref, b_ref, w_ref, bi_ref, xn_ref, q_ref, k_ref, v_ref, gb_ref, u_ref = refs
        x = _layer_norm(_stream_tile(n_prompt, xp_ref, xs_ref), g_ref[...], b_ref[...])
        xn_ref[...] = x
    else:
        x_ref, w_ref, bi_ref, q_ref, k_ref, v_ref, gb_ref, u_ref = refs
        x = x_ref[...]
    xb = x.astype(BF16)

    def proj(lo, hi):
        return jnp.dot(xb, w_ref[:, lo:hi], preferred_element_type=F32) + bi_ref[:, lo:hi]

    q_ref[...] = (proj(Q0, K0) * ATTN_SCALE).astype(BF16)
    k_ref[...] = proj(K0, V0)
    v_ref[...] = proj(V0, GB0)
    gb_ref[...] = proj(GB0, GC0)
    u_ref[...] = proj(GC0, H0) * proj(H0, IN_COLS)


def _inproj(x, ln, w_in_b, b_in):
    apply_ln = isinstance(x, tuple)
    row = lambda w: pl.BlockSpec((TM, w), lambda i: (i, 0))
    n_p = None
    if apply_ln:
        n_p, n_s = x[0].shape[0] // TM, x[1].shape[0] // TM
        t = x[0].shape[0] + x[1].shape[0]
        in_specs = _stream_specs(n_p, n_s, TM, D_MODEL) + [_full((1, D_MODEL)), _full((1, D_MODEL))]
        args = [x[0], x[1], ln[0], ln[1]]
    else:
        t = x.shape[0]
        in_specs = [row(D_MODEL)]
        args = [x]
    in_specs += [_full((D_MODEL, IN_COLS)), _full((1, IN_COLS))]
    args += [w_in_b, b_in]
    out_shape, out_specs = [], []
    if apply_ln:
        out_shape.append(jax.ShapeDtypeStruct((t, D_MODEL), F32))
        out_specs.append(row(D_MODEL))
    out_shape += [jax.ShapeDtypeStruct((t, ATTN_WIDTH), BF16),
                  jax.ShapeDtypeStruct((t, KV_WIDTH), F32),
                  jax.ShapeDtypeStruct((t, KV_WIDTH), F32),
                  jax.ShapeDtypeStruct((t, CONV_WIDTH), F32),
                  jax.ShapeDtypeStruct((t, CONV_WIDTH), F32)]
    out_specs += [row(ATTN_WIDTH), row(KV_WIDTH), row(KV_WIDTH), row(CONV_WIDTH), row(CONV_WIDTH)]
    outs = pl.pallas_call(
        functools.partial(_inproj_kernel, n_prompt=n_p),
        grid=(t // TM,), in_specs=in_specs, out_specs=out_specs, out_shape=out_shape,
        compiler_params=_cparams(("arbitrary",)), name="inproj")(*args)
    if apply_ln:
        return outs
    return [x] + list(outs)


def _attend_column(qcol, kexp_b, vexp_b, mask, sink_a, sink_b):
    lane = lax.broadcasted_iota(I32, qcol.shape, 1)
    outs = []
    for half, sink in ((0, sink_a), (1, sink_b)):
        keep = (lane < HEAD_DIM) if half == 0 else (lane >= HEAD_DIM)
        qm = jnp.where(keep, qcol, jnp.zeros_like(qcol))
        s = lax.dot_general(qm, kexp_b, (((1,), (1,)), ((), ())), preferred_element_type=F32)
        s = jnp.where(mask, s, -jnp.inf)
        m = jnp.maximum(jnp.max(s, -1, keepdims=True), sink)
        p = jnp.exp(s - m)
        denom = jnp.sum(p, -1, keepdims=True) + jnp.exp(sink - m)
        o = jnp.dot(p.astype(BF16), vexp_b, preferred_element_type=F32)
        outs.append(o * (1.0 / denom))
    return jnp.where(lane < HEAD_DIM, outs[0], outs[1])


def _dup_heads(x):
    lane = lax.broadcasted_iota(I32, x.shape, 1)
    xr = pltpu.roll(x, HEAD_DIM, 1)
    low = lane < HEAD_DIM
    return (jnp.where(low, x, xr).astype(BF16), jnp.where(low, xr, x).astype(BF16))


def _prompt_mixer_kernel(sinks_ref, q_ref, kc_ref, kp_ref, vc_ref, vp_ref, uc_ref, up_ref,
                         gb_ref, cw_ref, o_ref):
    first = pl.program_id(1) == 0
    r_i = lax.broadcasted_iota(I32, (WINDOW, 2 * WINDOW), 0)
    c_i = lax.broadcasted_iota(I32, (WINDOW, 2 * WINDOW), 1)
    band = (c_i <= WINDOW + r_i) & (c_i > r_i)
    band_first = band & (c_i >= WINDOW * first.astype(I32))
    for s in range(TQ // WINDOW):
        rows = slice(s * WINDOW, (s + 1) * WINDOW)
        if s == 0:
            kprev, vprev = kp_ref[...], vp_ref[...]
        else:
            prev = slice((s - 1) * WINDOW, s * WINDOW)
            kprev, vprev = kc_ref[prev, :], vc_ref[prev, :]
        kk = jnp.concatenate([kprev, kc_ref[rows, :]], 0)
        vv = jnp.concatenate([vprev, vc_ref[rows, :]], 0)
        kexp = _dup_heads(kk)
        vexp = _dup_heads(vv)
        mask = band_first if s == 0 else band
        for col in range(ATTN_WIDTH // LANES):
            h = col // 2
            cols = slice(col * LANES, (col + 1) * LANES)
            out = _attend_column(q_ref[rows, cols], kexp[h], vexp[h], mask,
                                 sinks_ref[2 * col], sinks_ref[2 * col + 1])
            o_ref[rows, cols] = out.astype(BF16)

    u = uc_ref[...]
    up = up_ref[...]
    zero = jnp.zeros((1, CONV_WIDTH), F32)
    p1 = jnp.where(first, zero, up[SUBLANES - 1:SUBLANES, :])
    p2 = jnp.where(first, zero, up[SUBLANES - 2:SUBLANES - 1, :])
    row = lax.broadcasted_iota(I32, u.shape, 0)
    u1 = jnp.where(row == 0, p1, pltpu.roll(u, 1, 0))
    u2 = jnp.where(row == 0, p2, jnp.where(row == 1, p1, pltpu.roll(u, 2, 0)))
    cw = cw_ref[...]
    y = u2 * cw[0:1, :] + u1 * cw[1:2, :] + u * cw[2:3, :]
    o_ref[:, ATTN_WIDTH:] = (gb_ref[...] * y).astype(BF16)


def _prompt_mixer(sinks, q, k, v, u, gb, conv_w, batch, seq):
    nj = seq // TQ
    tile = lambda b, j: b * nj + j
    cur = lambda w: pl.BlockSpec((TQ, w), lambda b, j: (tile(b, j), 0))
    prev_kv = pl.BlockSpec((WINDOW, KV_WIDTH),
                           lambda b, j: (jnp.maximum(tile(b, j) * (TQ // WINDOW) - 1, 0), 0))
    prev_u = pl.BlockSpec((SUBLANES, CONV_WIDTH),
                          lambda b, j: (jnp.maximum(tile(b, j) * (TQ // SUBLANES) - 1, 0), 0))
    return pl.pallas_call(
        _prompt_mixer_kernel,
        grid=(batch, nj),
        in_specs=[pl.BlockSpec(memory_space=pltpu.SMEM),
                  cur(ATTN_WIDTH), cur(KV_WIDTH), prev_kv, cur(KV_WIDTH), prev_kv,
                  cur(CONV_WIDTH), prev_u, cur(CONV_WIDTH), _full((CONV_K, CONV_WIDTH))],
        out_specs=cur(D_MODEL),
        out_shape=jax.ShapeDtypeStruct((batch * seq, D_MODEL), BF16),
        compiler_params=_cparams(("parallel", "parallel")), name="prompt_mixer",
    )(sinks, q, k, k, v, v, u, u, gb, conv_w)


def _sample_mixer_kernel(sinks_ref, q_ref, kn_ref, vn_ref, kb_ref, vb_ref, u_ref, st_ref, gb_ref,
                         cw_ref, o_ref, nk_ref, nv_ref, *, dec_seq):
    g, w = kb_ref.shape[0], kb_ref.shape[1]
    rows = g * dec_seq
    n_cache = g * w
    n_keys = n_cache + 2 * rows
    kn, vn = kn_ref[...], vn_ref[...]
    pad = jnp.zeros((rows, KV_WIDTH), F32)
    kk = jnp.concatenate([kb_ref[...].reshape(n_cache, KV_WIDTH), kn, pad], 0)
    vv = jnp.concatenate([vb_ref[...].reshape(n_cache, KV_WIDTH), vn, pad], 0)
    kexp = _dup_heads(kk)
    vexp = _dup_heads(vv)

    r_i = lax.broadcasted_iota(I32, (rows, n_keys), 0)
    c_i = lax.broadcasted_iota(I32, (rows, n_keys), 1)
    r_seq, r_pos = _div_pow2(r_i, dec_seq), _mod_pow2(r_i, dec_seq)
    c_new = c_i - n_cache
    in_cache = ((c_i < n_cache) & (_div_pow2(c_i, w) == r_seq)
                & (_mod_pow2(c_i, w) > r_pos + (w - WINDOW)))
    in_new = ((c_new >= 0) & (c_new < rows) & (_div_pow2(c_new, dec_seq) == r_seq)
              & (_mod_pow2(c_new, dec_seq) <= r_pos))
    mask = in_cache | in_new
    for col in range(ATTN_WIDTH // LANES):
        h = col // 2
        cols = slice(col * LANES, (col + 1) * LANES)
        out = _attend_column(q_ref[:, cols], kexp[h], vexp[h], mask,
                             sinks_ref[2 * col], sinks_ref[2 * col + 1])
        o_ref[:, cols] = out.astype(BF16)

    nk_ref[:, 0:w - dec_seq, :] = kb_ref[:, dec_seq:w, :]
    nk_ref[:, w - dec_seq:w, :] = kn.reshape(g, dec_seq, KV_WIDTH)
    nv_ref[:, 0:w - dec_seq, :] = vb_ref[:, dec_seq:w, :]
    nv_ref[:, w - dec_seq:w, :] = vn.reshape(g, dec_seq, KV_WIDTH)

    u = u_ref[...]
    st = st_ref[...]
    pos = lax.broadcasted_iota(I32, u.shape, 0) % dec_seq
    u1 = jnp.where(pos == 0, pltpu.roll(st, rows - 1, 0), pltpu.roll(u, 1, 0))
    u2 = jnp.where(pos < 2, st, pltpu.roll(u, 2, 0))
    cw = cw_ref[...]
    y = u2 * cw[0:1, :] + u1 * cw[1:2, :] + u * cw[2:3, :]
    o_ref[:, ATTN_WIDTH:] = (gb_ref[...] * y).astype(BF16)


def _sample_mixer(sinks, q, k, v, u, gb, conv_w, cache_k, cache_v, state_rows, t_prompt):
    nseq, w = cache_k.shape[0], cache_k.shape[1]
    dec_seq = (q.shape[0] - t_prompt) // nseq
    rows = SEQ_GROUP * dec_seq
    off = t_prompt // rows
    tok = lambda wd: pl.BlockSpec((rows, wd), lambda i: (off + i, 0))
    local = lambda wd: pl.BlockSpec((rows, wd), lambda i: (i, 0))
    cache = pl.BlockSpec((SEQ_GROUP, w, KV_WIDTH), lambda i: (i, 0, 0))
    return pl.pallas_call(
        functools.partial(_sample_mixer_kernel, dec_seq=dec_seq),
        grid=(nseq // SEQ_GROUP,),
        in_specs=[pl.BlockSpec(memory_space=pltpu.SMEM),
                  tok(ATTN_WIDTH), tok(KV_WIDTH), tok(KV_WIDTH), cache, cache,
                  tok(CONV_WIDTH), local(CONV_WIDTH), tok(CONV_WIDTH),
                  _full((CONV_K, CONV_WIDTH))],
        out_specs=[local(D_MODEL), cache, cache],
        out_shape=[jax.ShapeDtypeStruct((nseq * dec_seq, D_MODEL), BF16),
                   jax.ShapeDtypeStruct(cache_k.shape, F32),
                   jax.ShapeDtypeStruct(cache_v.shape, F32)],
        compiler_params=_cparams(("parallel",)), name="sample_mixer",
    )(sinks, q, k, v, cache_k, cache_v, u, state_rows, gb, conv_w)


def _post_mixer_kernel(mixp_ref, mixs_ref, xn_ref, pp_ref, ps_ref, wo_ref, bo_ref, g1_ref, b1_ref,
                       wg_ref, bg_ref, wp_ref, wrh_ref, wrl_ref, br_ref,
                       x1_ref, c_ref, eid_ref, gate_ref, rank_ref, cnt_ref, carry_ref, *, n_prompt):
    @pl.when(pl.program_id(0) == 0)
    def _():
        carry_ref[...] = jnp.zeros_like(carry_ref)

    mixed = _stream_tile(n_prompt, mixp_ref, mixs_ref)
    mix = jnp.dot(mixed, wo_ref[...], preferred_element_type=F32) + bo_ref[...]
    x1 = _layer_norm(DN_ALPHA * xn_ref[...] + mix, g1_ref[...], b1_ref[...])
    x1_ref[...] = x1
    x1h = x1.astype(BF16)
    x1l = (x1 - x1h.astype(F32)).astype(BF16)
    gate = jax.nn.sigmoid(jnp.dot(x1h, wg_ref[...], preferred_element_type=F32) + bg_ref[...])
    p_tile = _stream_tile(n_prompt, pp_ref, ps_ref).astype(BF16)
    ple = jnp.dot(p_tile, wp_ref[...], preferred_element_type=F32)
    c_ref[...] = DN_ALPHA * x1 + gate * ple

    nt = (((1,), (1,)), ((), ()))
    logits = (lax.dot_general(wrh_ref[...], x1h, nt, preferred_element_type=F32)
              + lax.dot_general(wrh_ref[...], x1l, nt, preferred_element_type=F32)
              + lax.dot_general(wrl_ref[...], x1h, nt, preferred_element_type=F32)
              + br_ref[...])
    tm = logits.shape[1]
    e_i = lax.broadcasted_iota(I32, logits.shape, 0).astype(F32)
    work = logits
    vals, sels = [], []
    for k in range(TOP_K):
        m = jnp.max(work, 0, keepdims=True)
        idx = jnp.min(jnp.where(work == m, e_i, float(N_EXPERTS)), 0, keepdims=True)
        sel = e_i == idx
        vals.append(m)
        sels.append(sel)
        eid_ref[k:k + 1, :] = idx.astype(I32)
        work = jnp.where(sel, -jnp.inf, work)
    exps = [jnp.exp(v - vals[0]) for v in vals]
    denom = exps[0] + exps[1] + exps[2] + exps[3]
    for k in range(TOP_K):
        gate_ref[k:k + 1, :] = exps[k] / denom

    chosen = jnp.where(sels[0] | sels[1] | sels[2] | sels[3], 1.0, 0.0)
    s_i = lax.broadcasted_iota(I32, (tm, tm), 0)
    t_i = lax.broadcasted_iota(I32, (tm, tm), 1)
    before = jnp.where(s_i < t_i, 1.0, 0.0).astype(BF16)
    pos = carry_ref[:, 0:1] + jnp.dot(chosen.astype(BF16), before, preferred_element_type=F32)
    for k in range(TOP_K):
        rank_ref[k:k + 1, :] = jnp.sum(jnp.where(sels[k], pos, 0.0), 0, keepdims=True).astype(I32)
    carry = carry_ref[...] + jnp.sum(chosen, 1, keepdims=True)
    carry_ref[...] = carry
    cnt_ref[...] = carry


def _post_mixer(mix_p, mix_s, xn, p_p, p_s, w_o, b_o, g1, b1, w_gate, b_gate, w_ple, wr_hi, wr_lo,
                b_r):
    t = xn.shape[0]
    n_p, n_s = mix_p.shape[0] // TM, mix_s.shape[0] // TM
    row = lambda w: pl.BlockSpec((TM, w), lambda i: (i, 0))
    meta = pl.BlockSpec((TOP_K, TM), lambda i: (0, i))
    vec = _full((1, D_MODEL))
    return pl.pallas_call(
        functools.partial(_post_mixer_kernel, n_prompt=n_p),
        grid=(t // TM,),
        in_specs=_stream_specs(n_p, n_s, TM, D_MODEL) + [row(D_MODEL)]
        + _stream_specs(n_p, n_s, TM, PLE_DIM)
        + [_full((D_MODEL, D_MODEL)), vec, vec, vec,
           _full((D_MODEL, D_MODEL)), vec, _full((PLE_DIM, D_MODEL)),
           _full((N_EXPERTS, D_MODEL)), _full((N_EXPERTS, D_MODEL)), _full((N_EXPERTS, 1))],
        out_specs=[row(D_MODEL), row(D_MODEL), meta, meta, meta, _full((N_EXPERTS, LANES))],
        out_shape=[jax.ShapeDtypeStruct((t, D_MODEL), F32),
                   jax.ShapeDtypeStruct((t, D_MODEL), F32),
                   jax.ShapeDtypeStruct((TOP_K, t), I32),
                   jax.ShapeDtypeStruct((TOP_K, t), F32),
                   jax.ShapeDtypeStruct((TOP_K, t), I32),
                   jax.ShapeDtypeStruct((N_EXPERTS, LANES), F32)],
        scratch_shapes=[pltpu.VMEM((N_EXPERTS, LANES), F32)],
        compiler_params=_cparams(("arbitrary",)), name="post_mixer",
    )(mix_p, mix_s, xn, p_p, p_s, w_o, b_o, g1, b1, w_gate, b_gate, w_ple, wr_hi, wr_lo, b_r)


def _row_copy(src, dst, sem):
    return pltpu.make_async_copy(src, dst, sem)


def _for_each_row_slot(groups, fn):
    def group(j, carry):
        for s in range(SUBLANES):
            for k in range(TOP_K):
                fn(k, j, s, s * TOP_K + k)
        return carry

    lax.fori_loop(0, groups, group, 0)


def _hbm_row(ref, d):
    return ref.at[lax.shift_right_logical(d, SUBLANES.bit_length() - 1),
                  pl.ds(d & (SUBLANES - 1), 1)]


def _grouped_rows(x):
    return x.reshape(x.shape[0] // SUBLANES, SUBLANES, x.shape[1])


def _dispatch_kernel(dest_ref, x_ref, xs_hbm, sem):
    groups = x_ref.shape[0]
    tm = groups * SUBLANES

    def scatter_row(k, j, s, n):
        d = dest_ref[0, k * tm + j * SUBLANES + s]
        _row_copy(x_ref.at[j, pl.ds(s, 1)], _hbm_row(xs_hbm, d), sem).start(priority=n % 2)

    _for_each_row_slot(groups, scatter_row)
    for k in range(TOP_K):
        _row_copy(x_ref, xs_hbm.at[pl.ds(0, groups)], sem).wait()


def _dest_tiles(dest, tm):
    t = dest.shape[1]
    return dest.reshape(TOP_K, t // tm, tm).transpose(1, 0, 2).reshape(t // tm, 1, TOP_K * tm)


def _dispatch(dest, x1):
    t = x1.shape[0]
    tm = TM_DISPATCH
    xs = pl.pallas_call(
        _dispatch_kernel,
        grid=(t // tm,),
        in_specs=[pl.BlockSpec((None, 1, TOP_K * tm), lambda i: (i, 0, 0), memory_space=pltpu.SMEM),
                  pl.BlockSpec((tm // SUBLANES, SUBLANES, D_MODEL), lambda i: (i, 0, 0))],
        out_specs=pl.BlockSpec(memory_space=pl.ANY),
        out_shape=jax.ShapeDtypeStruct((t * TOP_K // SUBLANES, SUBLANES, D_MODEL), F32),
        scratch_shapes=[pltpu.SemaphoreType.DMA],
        compiler_params=_cparams(("arbitrary",)), name="dispatch",
    )(_dest_tiles(dest, tm), _grouped_rows(x1))
    return xs.reshape(t * TOP_K, D_MODEL)


def _moe_kernel(tile_ref, exp_ref, lo_ref, hi_ref, first_ref, newexp_ref,
                xs_ref, w1_ref, b1_ref, w2_ref, b2_ref, ys_ref, act_ref, w1b_ref):
    del tile_ref, exp_ref
    w = pl.program_id(0)
    lo, hi = lo_ref[w], hi_ref[w]

    @pl.when(newexp_ref[w] == 1)
    def _():
        w1b_ref[...] = w1_ref[...].astype(BF16)

    @pl.when(hi > lo)
    def _():
        xb = xs_ref[...].astype(BF16)
        even = (lax.broadcasted_iota(I32, (xb.shape[0], LANES), 1) & 1) == 0
        for m in range(D_FF // LANES):
            cols = slice(2 * m * LANES, 2 * (m + 1) * LANES)
            h = jnp.dot(xb, w1b_ref[:, cols], preferred_element_type=F32) + b1_ref[:, cols]
            ha, hb = h[:, :LANES], h[:, LANES:]
            glu = jnp.where(even, ha, pltpu.roll(hb, 1, 1))
            lin = jnp.where(even, pltpu.roll(ha, LANES - 1, 1), hb)
            glu = jnp.minimum(glu, SWIGLU_LIMIT)
            lin = jnp.clip(lin, -SWIGLU_LIMIT, SWIGLU_LIMIT)
            act = glu * jax.nn.sigmoid(SWIGLU_ALPHA * glu) * (lin + 1.0)
            act_ref[:, m * LANES:(m + 1) * LANES] = act.astype(BF16)
        y = jnp.dot(act_ref[...], w2_ref[...], preferred_element_type=F32) + b2_ref[...]
        r = lax.broadcasted_iota(I32, y.shape, 0)
        mine = (r >= lo) & (r < hi)

        @pl.when(first_ref[w] == 1)
        def _():
            ys_ref[...] = jnp.where(mine, y, 0.0)

        @pl.when(first_ref[w] == 0)
        def _():
            ys_ref[...] = jnp.where(mine, y, ys_ref[...])


def _permute_w2_rows(w2):
    e = w2.shape[0]
    half = LANES // 2
    w = w2.reshape(e, D_FF // LANES, 2, half, D_MODEL)
    return w.transpose(0, 1, 3, 2, 4).reshape(e, D_FF, D_MODEL)


def _moe(sched, xs, w1, b1, w2p, b2):
    a = xs.shape[0]
    n_items = sched[0].shape[0]
    by_expert = lambda r, c: pl.BlockSpec((None, r, c), lambda w, tile, ex, *_: (ex[w], 0, 0))
    rows = pl.BlockSpec((TM_MOE, D_MODEL), lambda w, tile, *_: (tile[w], 0))
    return pl.pallas_call(
        _moe_kernel,
        grid_spec=pltpu.PrefetchScalarGridSpec(
            num_scalar_prefetch=6, grid=(n_items,),
            in_specs=[rows, by_expert(D_MODEL, 2 * D_FF), by_expert(1, 2 * D_FF),
                      by_expert(D_FF, D_MODEL), by_expert(1, D_MODEL)],
            out_specs=rows,
            scratch_shapes=[pltpu.VMEM((TM_MOE, D_FF), BF16),
                            pltpu.VMEM((D_MODEL, 2 * D_FF), BF16)]),
        out_shape=jax.ShapeDtypeStruct((a, D_MODEL), F32),
        compiler_params=pltpu.CompilerParams(dimension_semantics=("arbitrary",),
                                             vmem_limit_bytes=VMEM_LIMIT_MOE),
        name="moe",
    )(*sched, xs, w1, b1, w2p, b2)


def _moe_schedule(counts, n_rows):
    n_tiles = n_rows // TM_MOE
    n_items = n_tiles + N_EXPERTS - 1
    pend = jnp.cumsum(counts)
    pstart = pend - counts
    first_tile = pstart // TM_MOE
    last_tile = jnp.maximum(pend - 1, 0) // TM_MOE
    ntile = jnp.where(counts > 0, last_tile - first_tile + 1, 0)
    wend = jnp.cumsum(ntile)
    wstart = wend - ntile
    total = wend[-1]
    w = jnp.arange(n_items, dtype=I32)
    wv = jnp.minimum(w, total - 1)
    ex = jnp.minimum(jnp.sum(wend[None, :] <= wv[:, None], 1), N_EXPERTS - 1).astype(I32)
    tile = (first_tile[ex] + wv - wstart[ex]).astype(I32)
    valid = w < total
    lo = jnp.where(valid, jnp.clip(pstart[ex] - tile * TM_MOE, 0, TM_MOE), 0).astype(I32)
    hi = jnp.where(valid, jnp.clip(pend[ex] - tile * TM_MOE, 0, TM_MOE), 0).astype(I32)
    prev_tile = jnp.concatenate([jnp.full((1,), -1, I32), tile[:-1]])
    first = (valid & (tile != prev_tile)).astype(I32)
    prev_ex = jnp.concatenate([jnp.full((1,), -1, I32), ex[:-1]])
    newexp = (valid & (ex != prev_ex)).astype(I32)
    return (tile, ex, lo, hi, first, newexp), pstart


def _combine_kernel(dcur_ref, dnext_ref, c_ref, gate_ref, g2_ref, b2_ref, ys_hbm, *rest, n_prompt):
    *outs, buf, sem = rest
    tm = c_ref.shape[0]
    groups = tm // SUBLANES
    i = pl.program_id(0)

    def gather_tile(dest_ref, slot):
        def gather_row(k, j, s, n):
            d = dest_ref[0, k * tm + j * SUBLANES + s]
            _row_copy(_hbm_row(ys_hbm, d), buf.at[slot, k, j, pl.ds(s, 1)],
                      sem.at[slot]).start(priority=n % 2)

        _for_each_row_slot(groups, gather_row)

    @pl.when(i == 0)
    def _():
        gather_tile(dcur_ref, 0)

    @pl.when(i + 1 < pl.num_programs(0))
    def _():
        gather_tile(dnext_ref, (i + 1) % 2)

    slot = i % 2
    for k in range(TOP_K):
        _row_copy(ys_hbm.at[pl.ds(0, groups)], buf.at[slot, k], sem.at[slot]).wait()
    acc = c_ref[...]
    gates = gate_ref[...]
    for k in range(TOP_K):
        acc = acc + gates[:, k:k + 1] * buf[slot, k].reshape(tm, D_MODEL)
    y = _layer_norm(acc, g2_ref[...], b2_ref[...])
    if n_prompt is None:
        outs[0][...] = y
    else:
        @pl.when(i < n_prompt)
        def _():
            outs[0][...] = y

        @pl.when(i >= n_prompt)
        def _():
            outs[1][...] = y


def _combine(dest, c, gates_t, g2, b2, ys, t_prompt=None):
    t = c.shape[0]
    tm = TM_COMBINE
    n = t // tm
    vec = _full((1, D_MODEL))
    if t_prompt is None:
        n_p = None
        out_specs = pl.BlockSpec((tm, D_MODEL), lambda i: (i, 0))
        out_shape = jax.ShapeDtypeStruct((t, D_MODEL), F32)
    else:
        n_p = t_prompt // tm
        out_specs = _stream_specs(n_p, n - n_p, tm, D_MODEL)
        out_shape = [jax.ShapeDtypeStruct((t_prompt, D_MODEL), F32),
                     jax.ShapeDtypeStruct((t - t_prompt, D_MODEL), F32)]
    dest_spec = lambda f: pl.BlockSpec((None, 1, TOP_K * tm), lambda i: (f(i), 0, 0),
                                       memory_space=pltpu.SMEM)
    dest_tiles = _dest_tiles(dest, tm)
    return pl.pallas_call(
        functools.partial(_combine_kernel, n_prompt=n_p),
        grid=(n,),
        in_specs=[dest_spec(lambda i: i), dest_spec(lambda i: jnp.minimum(i + 1, n - 1)),
                  pl.BlockSpec((tm, D_MODEL), lambda i: (i, 0)),
                  pl.BlockSpec((tm, TOP_K), lambda i: (i, 0)),
                  vec, vec, pl.BlockSpec(memory_space=pl.ANY)],
        out_specs=out_specs,
        out_shape=out_shape,
        scratch_shapes=[pltpu.VMEM((2, TOP_K, tm // SUBLANES, SUBLANES, D_MODEL), F32),
                        pltpu.SemaphoreType.DMA((2,))],
        compiler_params=_cparams(("arbitrary",)), name="combine",
    )(dest_tiles, dest_tiles, c, gates_t, g2, b2, _grouped_rows(ys))


def kernel(x_prompt, x_sample, cache_k, cache_v, state_conv, p_prompt, p_sample, ln_emb_g, ln_emb_b,
           w_in, b_in, conv_w, sinks, w_o, b_o, ln1_g, ln1_b, w_gate, b_gate, w_ple, w_router,
           b_router, w1, b1, w2, b2, ln2_g, ln2_b):
    batch, seq, _ = x_prompt.shape
    nseq, dec_seq, _ = x_sample.shape
    t_p, t_s = batch * seq, nseq * dec_seq
    t = t_p + t_s
    w_cache = cache_k.shape[2]
    vec = lambda a: a.reshape(1, -1)

    x = (x_prompt.reshape(t_p, D_MODEL), x_sample.reshape(t_s, D_MODEL))
    state_rows = jnp.pad(state_conv, ((0, 0), (0, 0), (0, dec_seq - (CONV_K - 1)), (0, 0)))
    state_rows = state_rows.reshape(DEPTH, t_s, CONV_WIDTH)

    ks_p, vs_p, cs_p, ks_s, vs_s, cs_s = [], [], [], [], [], []
    for l in range(DEPTH):
        w_in_b = w_in[l].astype(BF16)
        xn, q, k, v, gb, u = _inproj(x, (vec(ln_emb_g), vec(ln_emb_b)), w_in_b, vec(b_in[l]))

        mix_p = _prompt_mixer(sinks[l], q, k, v, u, gb, conv_w[l], batch, seq)
        ck = cache_k[l].reshape(nseq, w_cache, KV_WIDTH)
        cv = cache_v[l].reshape(nseq, w_cache, KV_WIDTH)
        mix_s, nk_s, nv_s = _sample_mixer(sinks[l], q, k, v, u, gb, conv_w[l], ck, cv,
                                          state_rows[l], t_p)

        wr_t = w_router[l].T
        wr_hi = wr_t.astype(BF16)
        wr_lo = (wr_t - wr_hi.astype(F32)).astype(BF16)
        x1, c, eid, gates, ranks, cnt = _post_mixer(
            mix_p, mix_s, xn, p_prompt[l].reshape(t_p, PLE_DIM), p_sample[l].reshape(t_s, PLE_DIM),
            w_o[l].astype(BF16), vec(b_o[l]), vec(ln1_g[l]), vec(ln1_b[l]),
            w_gate[l].astype(BF16), vec(b_gate[l]), w_ple[l].astype(BF16), wr_hi, wr_lo,
            b_router[l].reshape(N_EXPERTS, 1))

        counts = cnt[:, 0].astype(I32)
        sched, pstart = _moe_schedule(counts, t * TOP_K)
        onehot = eid[:, :, None] == jnp.arange(N_EXPERTS, dtype=I32)
        dest = ranks + jnp.sum(jnp.where(onehot, pstart, 0), -1)

        xs = _dispatch(dest, x1)
        ys = _moe(sched, xs, w1[l], b1[l][:, None, :],
                  _permute_w2_rows(w2[l].astype(BF16)), b2[l][:, None, :])
        x = _combine(dest, c, gates.T, vec(ln2_g[l]), vec(ln2_b[l]), ys,
                     t_prompt=t_p if l == DEPTH - 1 else None)

        kp = k[:t_p].reshape(batch, seq, N_KV_HEADS, HEAD_DIM)
        vp = v[:t_p].reshape(batch, seq, N_KV_HEADS, HEAD_DIM)
        ks_p.append(kp[:, seq - WINDOW:])
        vs_p.append(vp[:, seq - WINDOW:])
        cs_p.append(u[:t_p].reshape(batch, seq, CONV_WIDTH)[:, seq - (CONV_K - 1):])
        ks_s.append(nk_s.reshape(nseq, w_cache, N_KV_HEADS, HEAD_DIM))
        vs_s.append(nv_s.reshape(nseq, w_cache, N_KV_HEADS, HEAD_DIM))
        cs_s.append(u[t_p:].reshape(nseq, dec_seq, CONV_WIDTH)[:, dec_seq - (CONV_K - 1):])

    y_prompt = x[0].reshape(batch, seq, D_MODEL)
    y_sample = x[1].reshape(nseq, dec_seq, D_MODEL)
    return (y_prompt, y_sample, jnp.stack(ks_p), jnp.stack(vs_p), jnp.stack(cs_p),
            jnp.stack(ks_s), jnp.stack(vs_s), jnp.stack(cs_s))
```

```python
import functools

import jax
import jax.numpy as jnp
from jax import lax
from jax.experimental import pallas as pl
from jax.experimental.pallas import tpu as pltpu

F32 = jnp.float32
BF16 = jnp.bfloat16
I32 = jnp.int32

D_MODEL = 1024
DEPTH = 2
HEAD_DIM = 64
N_Q_HEADS = 8
N_KV_HEADS = 2
ATTN_WIDTH = N_Q_HEADS * HEAD_DIM
KV_WIDTH = N_KV_HEADS * HEAD_DIM
CONV_WIDTH = D_MODEL - ATTN_WIDTH
WINDOW = 128
ATTN_SCALE = HEAD_DIM ** -0.5
CONV_K = 3
N_EXPERTS = 32
TOP_K = 4
D_FF = D_MODEL
SWIGLU_LIMIT = 7.0
SWIGLU_ALPHA = 1.702
PLE_DIM = 256
LN_EPS = 1e-5
DN_ALPHA = (2.0 * DEPTH) ** 0.25
IN_COLS = ATTN_WIDTH + 2 * KV_WIDTH + 3 * CONV_WIDTH
Q0, K0, V0, GB0, GC0, H0 = 0, 512, 640, 768, 1280, 1792

LANES = 128
SUBLANES = 8
VMEM_LIMIT = 48 * 1024 * 1024
VMEM_LIMIT_MOE = 56 * 1024 * 1024

TM = 512
TQ = 512
SEQ_GROUP = 8
TM_DISPATCH = 512
TM_COMBINE = 256
TM_MOE = 512


def _layer_norm(x, g, b):
    mu = jnp.mean(x, -1, keepdims=True)
    xc = x - mu
    var = jnp.mean(xc * xc, -1, keepdims=True)
    return xc * lax.rsqrt(var + LN_EPS) * g + b


def _div_pow2(x, n):
    assert n & (n - 1) == 0
    return lax.shift_right_arithmetic(x, n.bit_length() - 1)


def _mod_pow2(x, n):
    assert n & (n - 1) == 0
    return x & (n - 1)


def _cparams(sem):
    return pltpu.CompilerParams(dimension_semantics=sem, vmem_limit_bytes=VMEM_LIMIT)


def _full(shape):
    return pl.BlockSpec(shape, lambda *_: (0,) * len(shape))


def _stream_specs(n_p, n_s, tm, width, layer=None):
    prompt_tile = lambda i: jnp.minimum(i, n_p - 1)
    sample_tile = lambda i: jnp.clip(i - n_p, 0, n_s - 1)
    if layer is None:
        return [pl.BlockSpec((tm, width), lambda i: (prompt_tile(i), 0)),
                pl.BlockSpec((tm, width), lambda i: (sample_tile(i), 0))]
    return [pl.BlockSpec((None, tm, width), lambda i: (layer, prompt_tile(i), 0)),
            pl.BlockSpec((None, tm, width), lambda i: (layer, sample_tile(i), 0))]


def _stream_tile(n_p, prompt_ref, sample_ref):
    return jnp.where(pl.program_id(0) < n_p, prompt_ref[...], sample_ref[...])


def _inproj_kernel(*refs, n_prompt):
    if n_prompt is not None:
        xp_ref, xs_ref, g_ref, b_ref, w_ref, bi_ref, xn_ref, q_ref, k_ref, v_ref, gb_ref, u_ref = refs
        x = _layer_norm(_stream_tile(n_prompt, xp_ref, xs_ref), g_ref[...], b_ref[...])
        xn_ref[...] = x
    else:
        x_ref, w_ref, bi_ref, q_ref, k_ref, v_ref, gb_ref, u_ref = refs
        x = x_ref[...]
    xb = x.astype(BF16)

    def proj(lo, hi):
        return jnp.dot(xb, w_ref[:, lo:hi], preferred_element_type=F32) + bi_ref[:, lo:hi]

    q_ref[...] = (proj(Q0, K0) * ATTN_SCALE).astype(BF16)
    k_ref[...] = proj(K0, V0)
    v_ref[...] = proj(V0, GB0)
    gb_ref[...] = proj(GB0, GC0)
    u_ref[...] = proj(GC0, H0) * proj(H0, IN_COLS)


def _inproj(x, ln, w_in_b, b_in):
    apply_ln = isinstance(x, tuple)
    row = lambda w: pl.BlockSpec((TM, w), lambda i: (i, 0))
    n_p = None
    if apply_ln:
        n_p, n_s = x[0].shape[0] // TM, x[1].shape[0] // TM
        t = x[0].shape[0] + x[1].shape[0]
        in_specs = _stream_specs(n_p, n_s, TM, D_MODEL) + [_full((1, D_MODEL)), _full((1, D_MODEL))]
        args = [x[0], x[1], ln[0], ln[1]]
    else:
        t = x.shape[0]
        in_specs = [row(D_MODEL)]
        args = [x]
    in_specs += [_full((D_MODEL, IN_COLS)), _full((1, IN_COLS))]
    args += [w_in_b, b_in]
    out_shape, out_specs = [], []
    if apply_ln:
        out_shape.append(jax.ShapeDtypeStruct((t, D_MODEL), F32))
        out_specs.append(row(D_MODEL))
    out_shape += [jax.ShapeDtypeStruct((t, ATTN_WIDTH), BF16),
                  jax.ShapeDtypeStruct((t, KV_WIDTH), F32),
                  jax.ShapeDtypeStruct((t, KV_WIDTH), F32),
                  jax.ShapeDtypeStruct((t, CONV_WIDTH), F32),
                  jax.ShapeDtypeStruct((t, CONV_WIDTH), F32)]
    out_specs += [row(ATTN_WIDTH), row(KV_WIDTH), row(KV_WIDTH), row(CONV_WIDTH), row(CONV_WIDTH)]
    outs = pl.pallas_call(
        functools.partial(_inproj_kernel, n_prompt=n_p),
        grid=(t // TM,), in_specs=in_specs, out_specs=out_specs, out_shape=out_shape,
        compiler_params=_cparams(("arbitrary",)), name="inproj")(*args)
    if apply_ln:
        return outs
    return [x] + list(outs)


def _attend_column(qcol, kexp_b, vexp_b, mask, sink_a, sink_b):
    lane = lax.broadcasted_iota(I32, qcol.shape, 1)
    outs = []
    for half, sink in ((0, sink_a), (1, sink_b)):
        keep = (lane < HEAD_DIM) if half == 0 else (lane >= HEAD_DIM)
        qm = jnp.where(keep, qcol, jnp.zeros_like(qcol))
        s = lax.dot_general(qm, kexp_b, (((1,), (1,)), ((), ())), preferred_element_type=F32)
        s = jnp.where(mask, s, -jnp.inf)
        m = jnp.maximum(jnp.max(s, -1, keepdims=True), sink)
        p = jnp.exp(s - m)
        denom = jnp.sum(p, -1, keepdims=True) + jnp.exp(sink - m)
        o = jnp.dot(p.astype(BF16), vexp_b, preferred_element_type=F32)
        outs.append(o * (1.0 / denom))
    return jnp.where(lane < HEAD_DIM, outs[0], outs[1])


def _dup_heads(x):
    lane = lax.broadcasted_iota(I32, x.shape, 1)
    xr = pltpu.roll(x, HEAD_DIM, 1)
    low = lane < HEAD_DIM
    return (jnp.where(low, x, xr).astype(BF16), jnp.where(low, xr, x).astype(BF16))


def _prompt_mixer_kernel(sinks_ref, q_ref, kc_ref, kp_ref, vc_ref, vp_ref, uc_ref, up_ref,
                         gb_ref, cw_ref, o_ref, nk_ref, nv_ref, nu_ref):
    first = pl.program_id(1) == 0

    @pl.when(pl.program_id(1) == pl.num_programs(1) - 1)
    def _():
        nk_ref[...] = kc_ref[TQ - WINDOW:, :]
        nv_ref[...] = vc_ref[TQ - WINDOW:, :]
        nu_ref[...] = uc_ref[TQ - SUBLANES:, :]

    r_i = lax.broadcasted_iota(I32, (WINDOW, 2 * WINDOW), 0)
    c_i = lax.broadcasted_iota(I32, (WINDOW, 2 * WINDOW), 1)
    band = (c_i <= WINDOW + r_i) & (c_i > r_i)
    band_first = band & (c_i >= WINDOW * first.astype(I32))
    for s in range(TQ // WINDOW):
        rows = slice(s * WINDOW, (s + 1) * WINDOW)
        if s == 0:
            kprev, vprev = kp_ref[...], vp_ref[...]
        else:
            prev = slice((s - 1) * WINDOW, s * WINDOW)
            kprev, vprev = kc_ref[prev, :], vc_ref[prev, :]
        kk = jnp.concatenate([kprev, kc_ref[rows, :]], 0)
        vv = jnp.concatenate([vprev, vc_ref[rows, :]], 0)
        kexp = _dup_heads(kk)
        vexp = _dup_heads(vv)
        mask = band_first if s == 0 else band
        for col in range(ATTN_WIDTH // LANES):
            h = col // 2
            cols = slice(col * LANES, (col + 1) * LANES)
            out = _attend_column(q_ref[rows, cols], kexp[h], vexp[h], mask,
                                 sinks_ref[2 * col], sinks_ref[2 * col + 1])
            o_ref[rows, cols] = out.astype(BF16)

    u = uc_ref[...]
    up = up_ref[...]
    zero = jnp.zeros((1, CONV_WIDTH), F32)
    p1 = jnp.where(first, zero, up[SUBLANES - 1:SUBLANES, :])
    p2 = jnp.where(first, zero, up[SUBLANES - 2:SUBLANES - 1, :])
    row = lax.broadcasted_iota(I32, u.shape, 0)
    u1 = jnp.where(row == 0, p1, pltpu.roll(u, 1, 0))
    u2 = jnp.where(row == 0, p2, jnp.where(row == 1, p1, pltpu.roll(u, 2, 0)))
    cw = cw_ref[...]
    y = u2 * cw[0:1, :] + u1 * cw[1:2, :] + u * cw[2:3, :]
    o_ref[:, ATTN_WIDTH:] = (gb_ref[...] * y).astype(BF16)


def _prompt_mixer(sinks, q, k, v, u, gb, conv_w, batch, seq):
    nj = seq // TQ
    tile = lambda b, j: b * nj + j
    cur = lambda w: pl.BlockSpec((TQ, w), lambda b, j: (tile(b, j), 0))
    prev_kv = pl.BlockSpec((WINDOW, KV_WIDTH),
                           lambda b, j: (jnp.maximum(tile(b, j) * (TQ // WINDOW) - 1, 0), 0))
    prev_u = pl.BlockSpec((SUBLANES, CONV_WIDTH),
                          lambda b, j: (jnp.maximum(tile(b, j) * (TQ // SUBLANES) - 1, 0), 0))
    return pl.pallas_call(
        _prompt_mixer_kernel,
        grid=(batch, nj),
        in_specs=[pl.BlockSpec(memory_space=pltpu.SMEM),
                  cur(ATTN_WIDTH), cur(KV_WIDTH), prev_kv, cur(KV_WIDTH), prev_kv,
                  cur(CONV_WIDTH), prev_u, cur(CONV_WIDTH), _full((CONV_K, CONV_WIDTH))],
        out_specs=[cur(D_MODEL),
                   pl.BlockSpec((None, WINDOW, KV_WIDTH), lambda b, j: (b, 0, 0)),
                   pl.BlockSpec((None, WINDOW, KV_WIDTH), lambda b, j: (b, 0, 0)),
                   pl.BlockSpec((None, SUBLANES, CONV_WIDTH), lambda b, j: (b, 0, 0))],
        out_shape=[jax.ShapeDtypeStruct((batch * seq, D_MODEL), BF16),
                   jax.ShapeDtypeStruct((batch, WINDOW, KV_WIDTH), F32),
                   jax.ShapeDtypeStruct((batch, WINDOW, KV_WIDTH), F32),
                   jax.ShapeDtypeStruct((batch, SUBLANES, CONV_WIDTH), F32)],
        compiler_params=_cparams(("arbitrary", "arbitrary")), name="prompt_mixer",
    )(sinks, q, k, k, v, v, u, u, gb, conv_w)


def _sample_mixer_kernel(sinks_ref, q_ref, kn_ref, vn_ref, kb_ref, vb_ref, u_ref, st_ref, gb_ref,
                         cw_ref, o_ref, nk_ref, nv_ref, *, dec_seq):
    g, w = kb_ref.shape[0], kb_ref.shape[1]
    rows = g * dec_seq
    n_cache = g * w
    n_keys = n_cache + 2 * rows
    kn, vn = kn_ref[...], vn_ref[...]
    pad = jnp.zeros((rows, KV_WIDTH), F32)
    kk = jnp.concatenate([kb_ref[...].reshape(n_cache, KV_WIDTH), kn, pad], 0)
    vv = jnp.concatenate([vb_ref[...].reshape(n_cache, KV_WIDTH), vn, pad], 0)
    kexp = _dup_heads(kk)
    vexp = _dup_heads(vv)

    r_i = lax.broadcasted_iota(I32, (rows, n_keys), 0)
    c_i = lax.broadcasted_iota(I32, (rows, n_keys), 1)
    r_seq, r_pos = _div_pow2(r_i, dec_seq), _mod_pow2(r_i, dec_seq)
    c_new = c_i - n_cache
    in_cache = ((c_i < n_cache) & (_div_pow2(c_i, w) == r_seq)
                & (_mod_pow2(c_i, w) > r_pos + (w - WINDOW)))
    in_new = ((c_new >= 0) & (c_new < rows) & (_div_pow2(c_new, dec_seq) == r_seq)
              & (_mod_pow2(c_new, dec_seq) <= r_pos))
    mask = in_cache | in_new
    for col in range(ATTN_WIDTH // LANES):
        h = col // 2
        cols = slice(col * LANES, (col + 1) * LANES)
        out = _attend_column(q_ref[:, cols], kexp[h], vexp[h], mask,
                             sinks_ref[2 * col], sinks_ref[2 * col + 1])
        o_ref[:, cols] = out.astype(BF16)

    nk_ref[:, 0:w - dec_seq, :] = kb_ref[:, dec_seq:w, :]
    nk_ref[:, w - dec_seq:w, :] = kn.reshape(g, dec_seq, KV_WIDTH)
    nv_ref[:, 0:w - dec_seq, :] = vb_ref[:, dec_seq:w, :]
    nv_ref[:, w - dec_seq:w, :] = vn.reshape(g, dec_seq, KV_WIDTH)

    u = u_ref[...]
    st = st_ref[...]
    pos = lax.broadcasted_iota(I32, u.shape, 0) % dec_seq
    u1 = jnp.where(pos == 0, pltpu.roll(st, rows - 1, 0), pltpu.roll(u, 1, 0))
    u2 = jnp.where(pos < 2, st, pltpu.roll(u, 2, 0))
    cw = cw_ref[...]
    y = u2 * cw[0:1, :] + u1 * cw[1:2, :] + u * cw[2:3, :]
    o_ref[:, ATTN_WIDTH:] = (gb_ref[...] * y).astype(BF16)


def _sample_mixer(sinks, q, k, v, u, gb, conv_w, cache_k, cache_v, state_rows, t_prompt):
    nseq, w = cache_k.shape[0], cache_k.shape[1]
    dec_seq = (q.shape[0] - t_prompt) // nseq
    rows = SEQ_GROUP * dec_seq
    off = t_prompt // rows
    tok = lambda wd: pl.BlockSpec((rows, wd), lambda i: (off + i, 0))
    local = lambda wd: pl.BlockSpec((rows, wd), lambda i: (i, 0))
    cache = pl.BlockSpec((SEQ_GROUP, w, KV_WIDTH), lambda i: (i, 0, 0))
    return pl.pallas_call(
        functools.partial(_sample_mixer_kernel, dec_seq=dec_seq),
        grid=(nseq // SEQ_GROUP,),
        in_specs=[pl.BlockSpec(memory_space=pltpu.SMEM),
                  tok(ATTN_WIDTH), tok(KV_WIDTH), tok(KV_WIDTH), cache, cache,
                  tok(CONV_WIDTH), local(CONV_WIDTH), tok(CONV_WIDTH),
                  _full((CONV_K, CONV_WIDTH))],
        out_specs=[local(D_MODEL), cache, cache],
        out_shape=[jax.ShapeDtypeStruct((nseq * dec_seq, D_MODEL), BF16),
                   jax.ShapeDtypeStruct(cache_k.shape, F32),
                   jax.ShapeDtypeStruct(cache_v.shape, F32)],
        compiler_params=_cparams(("parallel",)), name="sample_mixer",
    )(sinks, q, k, v, cache_k, cache_v, u, state_rows, gb, conv_w)


def _post_mixer_kernel(mixp_ref, mixs_ref, xn_ref, pp_ref, ps_ref, wo_ref, bo_ref, g1_ref, b1_ref,
                       wg_ref, bg_ref, wp_ref, wrh_ref, wrl_ref, br_ref,
                       x1_ref, c_ref, eid_ref, gate_ref, rank_ref, cnt_ref, carry_ref, *, n_prompt):
    @pl.when(pl.program_id(0) == 0)
    def _():
        carry_ref[...] = jnp.zeros_like(carry_ref)

    mixed = _stream_tile(n_prompt, mixp_ref, mixs_ref)
    mix = jnp.dot(mixed, wo_ref[...], preferred_element_type=F32) + bo_ref[...]
    x1 = _layer_norm(DN_ALPHA * xn_ref[...] + mix, g1_ref[...], b1_ref[...])
    x1_ref[...] = x1
    x1h = x1.astype(BF16)
    x1l = (x1 - x1h.astype(F32)).astype(BF16)
    gate = jax.nn.sigmoid(jnp.dot(x1h, wg_ref[...], preferred_element_type=F32) + bg_ref[...])
    p_tile = _stream_tile(n_prompt, pp_ref, ps_ref).astype(BF16)
    ple = jnp.dot(p_tile, wp_ref[...], preferred_element_type=F32)
    c_ref[...] = DN_ALPHA * x1 + gate * ple

    nt = (((1,), (1,)), ((), ()))
    logits = (lax.dot_general(wrh_ref[...], x1h, nt, preferred_element_type=F32)
              + lax.dot_general(wrh_ref[...], x1l, nt, preferred_element_type=F32)
              + lax.dot_general(wrl_ref[...], x1h, nt, preferred_element_type=F32)
              + br_ref[...])
    tm = logits.shape[1]
    e_i = lax.broadcasted_iota(I32, logits.shape, 0).astype(F32)
    work = logits
    vals, sels = [], []
    for k in range(TOP_K):
        m = jnp.max(work, 0, keepdims=True)
        idx = jnp.min(jnp.where(work == m, e_i, float(N_EXPERTS)), 0, keepdims=True)
        sel = e_i == idx
        vals.append(m)
        sels.append(sel)
        eid_ref[k:k + 1, :] = idx.astype(I32)
        work = jnp.where(sel, -jnp.inf, work)
    exps = [jnp.exp(v - vals[0]) for v in vals]
    denom = exps[0] + exps[1] + exps[2] + exps[3]
    for k in range(TOP_K):
        gate_ref[k:k + 1, :] = exps[k] / denom

    chosen = jnp.where(sels[0] | sels[1] | sels[2] | sels[3], 1.0, 0.0)
    s_i = lax.broadcasted_iota(I32, (tm, tm), 0)
    t_i = lax.broadcasted_iota(I32, (tm, tm), 1)
    before = jnp.where(s_i < t_i, 1.0, 0.0).astype(BF16)
    pos = carry_ref[:, 0:1] + jnp.dot(chosen.astype(BF16), before, preferred_element_type=F32)
    for k in range(TOP_K):
        rank_ref[k:k + 1, :] = jnp.sum(jnp.where(sels[k], pos, 0.0), 0, keepdims=True).astype(I32)
    carry = carry_ref[...] + jnp.sum(chosen, 1, keepdims=True)
    carry_ref[...] = carry
    cnt_ref[...] = carry


def _post_mixer(mix_p, mix_s, xn, layer, p_p, p_s, w_o, b_o, g1, b1, w_gate, b_gate, w_ple, wr_hi,
                wr_lo, b_r):
    t = xn.shape[0]
    n_p, n_s = mix_p.shape[0] // TM, mix_s.shape[0] // TM
    row = lambda w: pl.BlockSpec((TM, w), lambda i: (i, 0))
    meta = pl.BlockSpec((TOP_K, TM), lambda i: (0, i))
    vec = _full((1, D_MODEL))
    return pl.pallas_call(
        functools.partial(_post_mixer_kernel, n_prompt=n_p),
        grid=(t // TM,),
        in_specs=_stream_specs(n_p, n_s, TM, D_MODEL) + [row(D_MODEL)]
        + _stream_specs(n_p, n_s, TM, PLE_DIM, layer)
        + [_full((D_MODEL, D_MODEL)), vec, vec, vec,
           _full((D_MODEL, D_MODEL)), vec, _full((PLE_DIM, D_MODEL)),
           _full((N_EXPERTS, D_MODEL)), _full((N_EXPERTS, D_MODEL)), _full((N_EXPERTS, 1))],
        out_specs=[row(D_MODEL), row(D_MODEL), meta, meta, meta, _full((N_EXPERTS, LANES))],
        out_shape=[jax.ShapeDtypeStruct((t, D_MODEL), F32),
                   jax.ShapeDtypeStruct((t, D_MODEL), F32),
                   jax.ShapeDtypeStruct((TOP_K, t), I32),
                   jax.ShapeDtypeStruct((TOP_K, t), F32),
                   jax.ShapeDtypeStruct((TOP_K, t), I32),
                   jax.ShapeDtypeStruct((N_EXPERTS, LANES), F32)],
        scratch_shapes=[pltpu.VMEM((N_EXPERTS, LANES), F32)],
        compiler_params=_cparams(("arbitrary",)), name="post_mixer",
    )(mix_p, mix_s, xn, p_p, p_s, w_o, b_o, g1, b1, w_gate, b_gate, w_ple, wr_hi, wr_lo, b_r)


def _row_copy(src, dst, sem):
    return pltpu.make_async_copy(src, dst, sem)


def _for_each_row_slot(groups, fn):
    def group(j, carry):
        for s in range(SUBLANES):
            for k in range(TOP_K):
                fn(k, j, s, s * TOP_K + k)
        return carry

    lax.fori_loop(0, groups, group, 0)


def _hbm_row(ref, d):
    return ref.at[lax.shift_right_logical(d, SUBLANES.bit_length() - 1),
                  pl.ds(d & (SUBLANES - 1), 1)]


def _grouped_rows(x):
    return x.reshape(x.shape[0] // SUBLANES, SUBLANES, x.shape[1])


def _dispatch_kernel(dest_ref, x_ref, xs_hbm, sem):
    groups = x_ref.shape[0]
    tm = groups * SUBLANES

    def scatter_row(k, j, s, n):
        d = dest_ref[0, k * tm + j * SUBLANES + s]
        _row_copy(x_ref.at[j, pl.ds(s, 1)], _hbm_row(xs_hbm, d), sem).start(priority=n % 2)

    _for_each_row_slot(groups, scatter_row)
    for k in range(TOP_K):
        _row_copy(x_ref, xs_hbm.at[pl.ds(0, groups)], sem).wait()


def _dest_tiles(dest, tm):
    t = dest.shape[1]
    return dest.reshape(TOP_K, t // tm, tm).transpose(1, 0, 2).reshape(t // tm, 1, TOP_K * tm)


def _dispatch(dest, x1):
    t = x1.shape[0]
    tm = TM_DISPATCH
    xs = pl.pallas_call(
        _dispatch_kernel,
        grid=(t // tm,),
        in_specs=[pl.BlockSpec((None, 1, TOP_K * tm), lambda i: (i, 0, 0), memory_space=pltpu.SMEM),
                  pl.BlockSpec((tm // SUBLANES, SUBLANES, D_MODEL), lambda i: (i, 0, 0))],
        out_specs=pl.BlockSpec(memory_space=pl.ANY),
        out_shape=jax.ShapeDtypeStruct((t * TOP_K // SUBLANES, SUBLANES, D_MODEL), F32),
        scratch_shapes=[pltpu.SemaphoreType.DMA],
        compiler_params=_cparams(("arbitrary",)), name="dispatch",
    )(_dest_tiles(dest, tm), _grouped_rows(x1))
    return xs.reshape(t * TOP_K, D_MODEL)


def _moe_kernel(tile_ref, exp_ref, lo_ref, hi_ref, first_ref, newexp_ref,
                xs_ref, w1_ref, b1_ref, w2_ref, b2_ref, ys_ref, act_ref, w1b_ref, w2b_ref):
    del tile_ref, exp_ref
    w = pl.program_id(0)
    lo, hi = lo_ref[w], hi_ref[w]

    @pl.when(newexp_ref[w] == 1)
    def _():
        w1b_ref[...] = w1_ref[...].astype(BF16)
        r_i = lax.broadcasted_iota(I32, (LANES, LANES), 0)
        c_i = lax.broadcasted_iota(I32, (LANES, LANES), 1)
        src = (LANES // 2) * (r_i & 1) + lax.shift_right_logical(r_i, 1)
        perm = jnp.where(c_i == src, 1.0, 0.0).astype(BF16)
        for m in range(D_FF // LANES):
            blk = slice(m * LANES, (m + 1) * LANES)
            w2b_ref[blk, :] = jnp.dot(perm, w2_ref[blk, :].astype(BF16),
                                      preferred_element_type=F32).astype(BF16)

    @pl.when(hi > lo)
    def _():
        xb = xs_ref[...].astype(BF16)
        even = (lax.broadcasted_iota(I32, (xb.shape[0], LANES), 1) & 1) == 0
        for m in range(D_FF // LANES):
            cols = slice(2 * m * LANES, 2 * (m + 1) * LANES)
            h = jnp.dot(xb, w1b_ref[:, cols], preferred_element_type=F32) + b1_ref[:, cols]
            ha, hb = h[:, :LANES], h[:, LANES:]
            glu = jnp.where(even, ha, pltpu.roll(hb, 1, 1))
            lin = jnp.where(even, pltpu.roll(ha, LANES - 1, 1), hb)
            glu = jnp.minimum(glu, SWIGLU_LIMIT)
            lin = jnp.clip(lin, -SWIGLU_LIMIT, SWIGLU_LIMIT)
            act = glu * jax.nn.sigmoid(SWIGLU_ALPHA * glu) * (lin + 1.0)
            act_ref[:, m * LANES:(m + 1) * LANES] = act.astype(BF16)
        y = jnp.dot(act_ref[...], w2b_ref[...], preferred_element_type=F32) + b2_ref[...]
        r = lax.broadcasted_iota(I32, y.shape, 0)
        mine = (r >= lo) & (r < hi)

        @pl.when(first_ref[w] == 1)
        def _():
            ys_ref[...] = jnp.where(mine, y, 0.0)

        @pl.when(first_ref[w] == 0)
        def _():
            ys_ref[...] = jnp.where(mine, y, ys_ref[...])


def _moe(sched, xs, layer, w1, b1, w2, b2):
    a = xs.shape[0]
    n_items = sched[0].shape[0]
    by_expert = lambda r, c: pl.BlockSpec((None, None, r, c),
                                          lambda w, tile, ex, *_: (layer, ex[w], 0, 0))
    rows = pl.BlockSpec((TM_MOE, D_MODEL), lambda w, tile, *_: (tile[w], 0))
    return pl.pallas_call(
        _moe_kernel,
        grid_spec=pltpu.PrefetchScalarGridSpec(
            num_scalar_prefetch=6, grid=(n_items,),
            in_specs=[rows, by_expert(D_MODEL, 2 * D_FF), by_expert(1, 2 * D_FF),
                      by_expert(D_FF, D_MODEL), by_expert(1, D_MODEL)],
            out_specs=rows,
            scratch_shapes=[pltpu.VMEM((TM_MOE, D_FF), BF16),
                            pltpu.VMEM((D_MODEL, 2 * D_FF), BF16),
                            pltpu.VMEM((D_FF, D_MODEL), BF16)]),
        out_shape=jax.ShapeDtypeStruct((a, D_MODEL), F32),
        compiler_params=pltpu.CompilerParams(dimension_semantics=("arbitrary",),
                                             vmem_limit_bytes=VMEM_LIMIT_MOE),
        name="moe",
    )(*sched, xs, w1, b1, w2, b2)


def _moe_schedule(counts, n_rows):
    n_tiles = n_rows // TM_MOE
    n_items = n_tiles + N_EXPERTS - 1
    pend = jnp.cumsum(counts)
    pstart = pend - counts
    first_tile = pstart // TM_MOE
    last_tile = jnp.maximum(pend - 1, 0) // TM_MOE
    ntile = jnp.where(counts > 0, last_tile - first_tile + 1, 0)
    wend = jnp.cumsum(ntile)
    wstart = wend - ntile
    total = wend[-1]
    w = jnp.arange(n_items, dtype=I32)
    wv = jnp.minimum(w, total - 1)
    ex = jnp.minimum(jnp.sum(wend[None, :] <= wv[:, None], 1), N_EXPERTS - 1).astype(I32)
    tile = (first_tile[ex] + wv - wstart[ex]).astype(I32)
    valid = w < total
    lo = jnp.where(valid, jnp.clip(pstart[ex] - tile * TM_MOE, 0, TM_MOE), 0).astype(I32)
    hi = jnp.where(valid, jnp.clip(pend[ex] - tile * TM_MOE, 0, TM_MOE), 0).astype(I32)
    prev_tile = jnp.concatenate([jnp.full((1,), -1, I32), tile[:-1]])
    first = (valid & (tile != prev_tile)).astype(I32)
    prev_ex = jnp.concatenate([jnp.full((1,), -1, I32), ex[:-1]])
    newexp = (valid & (ex != prev_ex)).astype(I32)
    return (tile, ex, lo, hi, first, newexp), pstart


def _combine_kernel(dcur_ref, dnext_ref, c_ref, gate_ref, g2_ref, b2_ref, ys_hbm, *rest, n_prompt):
    *outs, buf, sem = rest
    tm = c_ref.shape[0]
    groups = tm // SUBLANES
    i = pl.program_id(0)

    def gather_tile(dest_ref, slot):
        def gather_row(k, j, s, n):
            d = dest_ref[0, k * tm + j * SUBLANES + s]
            _row_copy(_hbm_row(ys_hbm, d), buf.at[slot, k, j, pl.ds(s, 1)],
                      sem.at[slot]).start(priority=n % 2)

        _for_each_row_slot(groups, gather_row)

    @pl.when(i == 0)
    def _():
        gather_tile(dcur_ref, 0)

    @pl.when(i + 1 < pl.num_programs(0))
    def _():
        gather_tile(dnext_ref, (i + 1) % 2)

    slot = i % 2
    for k in range(TOP_K):
        _row_copy(ys_hbm.at[pl.ds(0, groups)], buf.at[slot, k], sem.at[slot]).wait()
    acc = c_ref[...]
    gates = gate_ref[...]
    for k in range(TOP_K):
        acc = acc + gates[:, k:k + 1] * buf[slot, k].reshape(tm, D_MODEL)
    y = _layer_norm(acc, g2_ref[...], b2_ref[...])
    if n_prompt is None:
        outs[0][...] = y
    else:
        @pl.when(i < n_prompt)
        def _():
            outs[0][...] = y

        @pl.when(i >= n_prompt)
        def _():
            outs[1][...] = y


def _combine(dest, c, gates_t, g2, b2, ys, t_prompt=None):
    t = c.shape[0]
    tm = TM_COMBINE
    n = t // tm
    vec = _full((1, D_MODEL))
    if t_prompt is None:
        n_p = None
        out_specs = pl.BlockSpec((tm, D_MODEL), lambda i: (i, 0))
        out_shape = jax.ShapeDtypeStruct((t, D_MODEL), F32)
    else:
        n_p = t_prompt // tm
        out_specs = _stream_specs(n_p, n - n_p, tm, D_MODEL)
        out_shape = [jax.ShapeDtypeStruct((t_prompt, D_MODEL), F32),
                     jax.ShapeDtypeStruct((t - t_prompt, D_MODEL), F32)]
    dest_spec = lambda f: pl.BlockSpec((None, 1, TOP_K * tm), lambda i: (f(i), 0, 0),
                                       memory_space=pltpu.SMEM)
    dest_tiles = _dest_tiles(dest, tm)
    return pl.pallas_call(
        functools.partial(_combine_kernel, n_prompt=n_p),
        grid=(n,),
        in_specs=[dest_spec(lambda i: i), dest_spec(lambda i: jnp.minimum(i + 1, n - 1)),
                  pl.BlockSpec((tm, D_MODEL), lambda i: (i, 0)),
                  pl.BlockSpec((tm, TOP_K), lambda i: (i, 0)),
                  vec, vec, pl.BlockSpec(memory_space=pl.ANY)],
        out_specs=out_specs,
        out_shape=out_shape,
        scratch_shapes=[pltpu.VMEM((2, TOP_K, tm // SUBLANES, SUBLANES, D_MODEL), F32),
                        pltpu.SemaphoreType.DMA((2,))],
        compiler_params=_cparams(("arbitrary",)), name="combine",
    )(dest_tiles, dest_tiles, c, gates_t, g2, b2, _grouped_rows(ys))


def kernel(x_prompt, x_sample, cache_k, cache_v, state_conv, p_prompt, p_sample, ln_emb_g, ln_emb_b,
           w_in, b_in, conv_w, sinks, w_o, b_o, ln1_g, ln1_b, w_gate, b_gate, w_ple, w_router,
           b_router, w1, b1, w2, b2, ln2_g, ln2_b):
    batch, seq, _ = x_prompt.shape
    nseq, dec_seq, _ = x_sample.shape
    t_p, t_s = batch * seq, nseq * dec_seq
    t = t_p + t_s
    w_cache = cache_k.shape[2]
    vec = lambda a: a.reshape(1, -1)

    x = (x_prompt.reshape(t_p, D_MODEL), x_sample.reshape(t_s, D_MODEL))
    state_rows = jnp.pad(state_conv, ((0, 0), (0, 0), (0, dec_seq - (CONV_K - 1)), (0, 0)))
    state_rows = state_rows.reshape(DEPTH, t_s, CONV_WIDTH)

    ks_p, vs_p, cs_p, ks_s, vs_s, cs_s = [], [], [], [], [], []
    for l in range(DEPTH):
        w_in_b = w_in[l].astype(BF16)
        xn, q, k, v, gb, u = _inproj(x, (vec(ln_emb_g), vec(ln_emb_b)), w_in_b, vec(b_in[l]))

        mix_p, nk_p, nv_p, nu_p = _prompt_mixer(sinks[l], q, k, v, u, gb, conv_w[l], batch, seq)
        ck = cache_k[l].reshape(nseq, w_cache, KV_WIDTH)
        cv = cache_v[l].reshape(nseq, w_cache, KV_WIDTH)
        mix_s, nk_s, nv_s = _sample_mixer(sinks[l], q, k, v, u, gb, conv_w[l], ck, cv,
                                          state_rows[l], t_p)

        wr_t = w_router[l].T
        wr_hi = wr_t.astype(BF16)
        wr_lo = (wr_t - wr_hi.astype(F32)).astype(BF16)
        x1, c, eid, gates, ranks, cnt = _post_mixer(
            mix_p, mix_s, xn, l, p_prompt.reshape(DEPTH, t_p, PLE_DIM),
            p_sample.reshape(DEPTH, t_s, PLE_DIM), w_o[l].astype(BF16), vec(b_o[l]), vec(ln1_g[l]), vec(ln1_b[l]),
            w_gate[l].astype(BF16), vec(b_gate[l]), w_ple[l].astype(BF16), wr_hi, wr_lo,
            b_router[l].reshape(N_EXPERTS, 1))

        counts = cnt[:, 0].astype(I32)
        sched, pstart = _moe_schedule(counts, t * TOP_K)
        onehot = eid[:, :, None] == jnp.arange(N_EXPERTS, dtype=I32)
        dest = ranks + jnp.sum(jnp.where(onehot, pstart, 0), -1)

        xs = _dispatch(dest, x1)
        ys = _moe(sched, xs, l, w1, b1[:, :, None, :], w2, b2[:, :, None, :])
        x = _combine(dest, c, gates.T, vec(ln2_g[l]), vec(ln2_b[l]), ys,
                     t_prompt=t_p if l == DEPTH - 1 else None)

        ks_p.append(nk_p.reshape(batch, WINDOW, N_KV_HEADS, HEAD_DIM))
        vs_p.append(nv_p.reshape(batch, WINDOW, N_KV_HEADS, HEAD_DIM))
        cs_p.append(nu_p[:, SUBLANES - (CONV_K - 1):])
        ks_s.append(nk_s.reshape(nseq, w_cache, N_KV_HEADS, HEAD_DIM))
        vs_s.append(nv_s.reshape(nseq, w_cache, N_KV_HEADS, HEAD_DIM))
        cs_s.append(u[t_p:].reshape(nseq, dec_seq, CONV_WIDTH)[:, dec_seq - (CONV_K - 1):])

    y_prompt = x[0].reshape(batch, seq, D_MODEL)
    y_sample = x[1].reshape(nseq, dec_seq, D_MODEL)
    return (y_prompt, y_sample, jnp.stack(ks_p), jnp.stack(vs_p), jnp.stack(cs_p),
            jnp.stack(ks_s), jnp.stack(vs_s), jnp.stack(cs_s))
```

```python
import functools

import jax
import jax.numpy as jnp
from jax import lax
from jax.experimental import pallas as pl
from jax.experimental.pallas import tpu as pltpu

F32 = jnp.float32
BF16 = jnp.bfloat16
I32 = jnp.int32

D_MODEL = 1024
DEPTH = 2
HEAD_DIM = 64
N_Q_HEADS = 8
N_KV_HEADS = 2
ATTN_WIDTH = N_Q_HEADS * HEAD_DIM
KV_WIDTH = N_KV_HEADS * HEAD_DIM
CONV_WIDTH = D_MODEL - ATTN_WIDTH
WINDOW = 128
ATTN_SCALE = HEAD_DIM ** -0.5
CONV_K = 3
N_EXPERTS = 32
TOP_K = 4
D_FF = D_MODEL
SWIGLU_LIMIT = 7.0
SWIGLU_ALPHA = 1.702
PLE_DIM = 256
LN_EPS = 1e-5
DN_ALPHA = (2.0 * DEPTH) ** 0.25
IN_COLS = ATTN_WIDTH + 2 * KV_WIDTH + 3 * CONV_WIDTH
Q0, K0, V0, GB0, GC0, H0 = 0, 512, 640, 768, 1280, 1792

LANES = 128
SUBLANES = 8
VMEM_LIMIT = 48 * 1024 * 1024
VMEM_LIMIT_MOE = 56 * 1024 * 1024

TM = 512
TQ = 512
SEQ_GROUP = 8
TM_DISPATCH = 512
TM_COMBINE = 256
TM_MOE = 512


def _layer_norm(x, g, b):
    mu = jnp.mean(x, -1, keepdims=True)
    xc = x - mu
    var = jnp.mean(xc * xc, -1, keepdims=True)
    return xc * lax.rsqrt(var + LN_EPS) * g + b


def _div_pow2(x, n):
    assert n & (n - 1) == 0
    return lax.shift_right_arithmetic(x, n.bit_length() - 1)


def _mod_pow2(x, n):
    assert n & (n - 1) == 0
    return x & (n - 1)


def _cparams(sem):
    return pltpu.CompilerParams(dimension_semantics=sem, vmem_limit_bytes=VMEM_LIMIT)


def _full(shape):
    return pl.BlockSpec(shape, lambda *_: (0,) * len(shape))


def _stream_specs(n_p, n_s, tm, width, layer=None):
    prompt_tile = lambda i: jnp.minimum(i, n_p - 1)
    sample_tile = lambda i: jnp.clip(i - n_p, 0, n_s - 1)
    if layer is None:
        return [pl.BlockSpec((tm, width), lambda i: (prompt_tile(i), 0)),
                pl.BlockSpec((tm, width), lambda i: (sample_tile(i), 0))]
    return [pl.BlockSpec((None, tm, width), lambda i: (layer, prompt_tile(i), 0)),
            pl.BlockSpec((None, tm, width), lambda i: (layer, sample_tile(i), 0))]


def _stream_tile(n_p, prompt_ref, sample_ref):
    return jnp.where(pl.program_id(0) < n_p, prompt_ref[...], sample_ref[...])


def _inproj_kernel(*refs, n_prompt):
    if n_prompt is not None:
        xp_ref, xs_ref, g_ref, b_ref, w_ref, bi_ref, xn_ref, q_ref, k_ref, v_ref, gb_ref, u_ref = refs
        x = _layer_norm(_stream_tile(n_prompt, xp_ref, xs_ref), g_ref[...], b_ref[...])
        xn_ref[...] = x
    else:
        x_ref, w_ref, bi_ref, q_ref, k_ref, v_ref, gb_ref, u_ref = refs
        x = x_ref[...]
    xb = x.astype(BF16)

    def proj(lo, hi):
        return jnp.dot(xb, w_ref[:, lo:hi], preferred_element_type=F32) + bi_ref[:, lo:hi]

    q_ref[...] = (proj(Q0, K0) * ATTN_SCALE).astype(BF16)
    k_ref[...] = proj(K0, V0)
    v_ref[...] = proj(V0, GB0)
    gb_ref[...] = proj(GB0, GC0)
    u_ref[...] = proj(GC0, H0) * proj(H0, IN_COLS)


def _inproj(x, ln, w_in_b, b_in):
    apply_ln = isinstance(x, tuple)
    row = lambda w: pl.BlockSpec((TM, w), lambda i: (i, 0))
    n_p = None
    if apply_ln:
        n_p, n_s = x[0].shape[0] // TM, x[1].shape[0] // TM
        t = x[0].shape[0] + x[1].shape[0]
        in_specs = _stream_specs(n_p, n_s, TM, D_MODEL) + [_full((1, D_MODEL)), _full((1, D_MODEL))]
        args = [x[0], x[1], ln[0], ln[1]]
    else:
        t = x.shape[0]
        in_specs = [row(D_MODEL)]
        args = [x]
    in_specs += [_full((D_MODEL, IN_COLS)), _full((1, IN_COLS))]
    args += [w_in_b, b_in]
    out_shape, out_specs = [], []
    if apply_ln:
        out_shape.append(jax.ShapeDtypeStruct((t, D_MODEL), F32))
        out_specs.append(row(D_MODEL))
    out_shape += [jax.ShapeDtypeStruct((t, ATTN_WIDTH), BF16),
                  jax.ShapeDtypeStruct((t, KV_WIDTH), F32),
                  jax.ShapeDtypeStruct((t, KV_WIDTH), F32),
                  jax.ShapeDtypeStruct((t, CONV_WIDTH), F32),
                  jax.ShapeDtypeStruct((t, CONV_WIDTH), F32)]
    out_specs += [row(ATTN_WIDTH), row(KV_WIDTH), row(KV_WIDTH), row(CONV_WIDTH), row(CONV_WIDTH)]
    outs = pl.pallas_call(
        functools.partial(_inproj_kernel, n_prompt=n_p),
        grid=(t // TM,), in_specs=in_specs, out_specs=out_specs, out_shape=out_shape,
        compiler_params=_cparams(("arbitrary",)), name="inproj")(*args)
    if apply_ln:
        return outs
    return [x] + list(outs)


def _attend_column(qcol, kexp_b, vexp_b, mask, sink_a, sink_b):
    lane = lax.broadcasted_iota(I32, qcol.shape, 1)
    outs = []
    for half, sink in ((0, sink_a), (1, sink_b)):
        keep = (lane < HEAD_DIM) if half == 0 else (lane >= HEAD_DIM)
        qm = jnp.where(keep, qcol, jnp.zeros_like(qcol))
        s = lax.dot_general(qm, kexp_b, (((1,), (1,)), ((), ())), preferred_element_type=F32)
        s = jnp.where(mask, s, -jnp.inf)
        m = jnp.maximum(jnp.max(s, -1, keepdims=True), sink)
        p = jnp.exp(s - m)
        denom = jnp.sum(p, -1, keepdims=True) + jnp.exp(sink - m)
        o = jnp.dot(p.astype(BF16), vexp_b, preferred_element_type=F32)
        outs.append(o * (1.0 / denom))
    return jnp.where(lane < HEAD_DIM, outs[0], outs[1])


def _dup_heads(x):
    lane = lax.broadcasted_iota(I32, x.shape, 1)
    xr = pltpu.roll(x, HEAD_DIM, 1)
    low = lane < HEAD_DIM
    return (jnp.where(low, x, xr).astype(BF16), jnp.where(low, xr, x).astype(BF16))


def _prompt_mixer_kernel(sinks_ref, q_ref, kc_ref, kp_ref, vc_ref, vp_ref, uc_ref, up_ref,
                         gb_ref, cw_ref, o_ref, nk_ref, nv_ref, nu_ref):
    first = pl.program_id(1) == 0

    @pl.when(pl.program_id(1) == pl.num_programs(1) - 1)
    def _():
        nk_ref[...] = kc_ref[TQ - WINDOW:, :]
        nv_ref[...] = vc_ref[TQ - WINDOW:, :]
        nu_ref[...] = uc_ref[TQ - SUBLANES:, :]

    r_i = lax.broadcasted_iota(I32, (WINDOW, 2 * WINDOW), 0)
    c_i = lax.broadcasted_iota(I32, (WINDOW, 2 * WINDOW), 1)
    band = (c_i <= WINDOW + r_i) & (c_i > r_i)
    band_first = band & (c_i >= WINDOW * first.astype(I32))
    for s in range(TQ // WINDOW):
        rows = slice(s * WINDOW, (s + 1) * WINDOW)
        if s == 0:
            kprev, vprev = kp_ref[...], vp_ref[...]
        else:
            prev = slice((s - 1) * WINDOW, s * WINDOW)
            kprev, vprev = kc_ref[prev, :], vc_ref[prev, :]
        kk = jnp.concatenate([kprev, kc_ref[rows, :]], 0)
        vv = jnp.concatenate([vprev, vc_ref[rows, :]], 0)
        kexp = _dup_heads(kk)
        vexp = _dup_heads(vv)
        mask = band_first if s == 0 else band
        for col in range(ATTN_WIDTH // LANES):
            h = col // 2
            cols = slice(col * LANES, (col + 1) * LANES)
            out = _attend_column(q_ref[rows, cols], kexp[h], vexp[h], mask,
                                 sinks_ref[2 * col], sinks_ref[2 * col + 1])
            o_ref[rows, cols] = out.astype(BF16)

    u = uc_ref[...]
    up = up_ref[...]
    zero = jnp.zeros((1, CONV_WIDTH), F32)
    p1 = jnp.where(first, zero, up[SUBLANES - 1:SUBLANES, :])
    p2 = jnp.where(first, zero, up[SUBLANES - 2:SUBLANES - 1, :])
    row = lax.broadcasted_iota(I32, u.shape, 0)
    u1 = jnp.where(row == 0, p1, pltpu.roll(u, 1, 0))
    u2 = jnp.where(row == 0, p2, jnp.where(row == 1, p1, pltpu.roll(u, 2, 0)))
    cw = cw_ref[...]
    y = u2 * cw[0:1, :] + u1 * cw[1:2, :] + u * cw[2:3, :]
    o_ref[:, ATTN_WIDTH:] = (gb_ref[...] * y).astype(BF16)


def _prompt_mixer(sinks, q, k, v, u, gb, conv_w, batch, seq):
    nj = seq // TQ
    tile = lambda b, j: b * nj + j
    cur = lambda w: pl.BlockSpec((TQ, w), lambda b, j: (tile(b, j), 0))
    prev_kv = pl.BlockSpec((WINDOW, KV_WIDTH),
                           lambda b, j: (jnp.maximum(tile(b, j) * (TQ // WINDOW) - 1, 0), 0))
    prev_u = pl.BlockSpec((SUBLANES, CONV_WIDTH),
                          lambda b, j: (jnp.maximum(tile(b, j) * (TQ // SUBLANES) - 1, 0), 0))
    return pl.pallas_call(
        _prompt_mixer_kernel,
        grid=(batch, nj),
        in_specs=[pl.BlockSpec(memory_space=pltpu.SMEM),
                  cur(ATTN_WIDTH), cur(KV_WIDTH), prev_kv, cur(KV_WIDTH), prev_kv,
                  cur(CONV_WIDTH), prev_u, cur(CONV_WIDTH), _full((CONV_K, CONV_WIDTH))],
        out_specs=[cur(D_MODEL),
                   pl.BlockSpec((None, WINDOW, KV_WIDTH), lambda b, j: (b, 0, 0)),
                   pl.BlockSpec((None, WINDOW, KV_WIDTH), lambda b, j: (b, 0, 0)),
                   pl.BlockSpec((None, SUBLANES, CONV_WIDTH), lambda b, j: (b, 0, 0))],
        out_shape=[jax.ShapeDtypeStruct((batch * seq, D_MODEL), BF16),
                   jax.ShapeDtypeStruct((batch, WINDOW, KV_WIDTH), F32),
                   jax.ShapeDtypeStruct((batch, WINDOW, KV_WIDTH), F32),
                   jax.ShapeDtypeStruct((batch, SUBLANES, CONV_WIDTH), F32)],
        compiler_params=_cparams(("arbitrary", "arbitrary")), name="prompt_mixer",
    )(sinks, q, k, k, v, v, u, u, gb, conv_w)


def _sample_mixer_kernel(sinks_ref, q_ref, kn_ref, vn_ref, kb_ref, vb_ref, u_ref, st_ref, gb_ref,
                         cw_ref, o_ref, nk_ref, nv_ref, *, dec_seq):
    g, w = kb_ref.shape[0], kb_ref.shape[1]
    rows = g * dec_seq
    n_cache = g * w
    n_keys = n_cache + 2 * rows
    kn, vn = kn_ref[...], vn_ref[...]
    pad = jnp.zeros((rows, KV_WIDTH), F32)
    kk = jnp.concatenate([kb_ref[...].reshape(n_cache, KV_WIDTH), kn, pad], 0)
    vv = jnp.concatenate([vb_ref[...].reshape(n_cache, KV_WIDTH), vn, pad], 0)
    kexp = _dup_heads(kk)
    vexp = _dup_heads(vv)

    r_i = lax.broadcasted_iota(I32, (rows, n_keys), 0)
    c_i = lax.broadcasted_iota(I32, (rows, n_keys), 1)
    r_seq, r_pos = _div_pow2(r_i, dec_seq), _mod_pow2(r_i, dec_seq)
    c_new = c_i - n_cache
    in_cache = ((c_i < n_cache) & (_div_pow2(c_i, w) == r_seq)
                & (_mod_pow2(c_i, w) > r_pos + (w - WINDOW)))
    in_new = ((c_new >= 0) & (c_new < rows) & (_div_pow2(c_new, dec_seq) == r_seq)
              & (_mod_pow2(c_new, dec_seq) <= r_pos))
    mask = in_cache | in_new
    for col in range(ATTN_WIDTH // LANES):
        h = col // 2
        cols = slice(col * LANES, (col + 1) * LANES)
        out = _attend_column(q_ref[:, cols], kexp[h], vexp[h], mask,
                             sinks_ref[2 * col], sinks_ref[2 * col + 1])
        o_ref[:, cols] = out.astype(BF16)

    nk_ref[:, 0:w - dec_seq, :] = kb_ref[:, dec_seq:w, :]
    nk_ref[:, w - dec_seq:w, :] = kn.reshape(g, dec_seq, KV_WIDTH)
    nv_ref[:, 0:w - dec_seq, :] = vb_ref[:, dec_seq:w, :]
    nv_ref[:, w - dec_seq:w, :] = vn.reshape(g, dec_seq, KV_WIDTH)

    u = u_ref[...]
    st = st_ref[...]
    pos = lax.broadcasted_iota(I32, u.shape, 0) % dec_seq
    u1 = jnp.where(pos == 0, pltpu.roll(st, rows - 1, 0), pltpu.roll(u, 1, 0))
    u2 = jnp.where(pos < 2, st, pltpu.roll(u, 2, 0))
    cw = cw_ref[...]
    y = u2 * cw[0:1, :] + u1 * cw[1:2, :] + u * cw[2:3, :]
    o_ref[:, ATTN_WIDTH:] = (gb_ref[...] * y).astype(BF16)


def _sample_mixer(sinks, q, k, v, u, gb, conv_w, cache_k, cache_v, state_rows, t_prompt):
    nseq, w = cache_k.shape[0], cache_k.shape[1]
    dec_seq = (q.shape[0] - t_prompt) // nseq
    rows = SEQ_GROUP * dec_seq
    off = t_prompt // rows
    tok = lambda wd: pl.BlockSpec((rows, wd), lambda i: (off + i, 0))
    local = lambda wd: pl.BlockSpec((rows, wd), lambda i: (i, 0))
    cache = pl.BlockSpec((SEQ_GROUP, w, KV_WIDTH), lambda i: (i, 0, 0))
    return pl.pallas_call(
        functools.partial(_sample_mixer_kernel, dec_seq=dec_seq),
        grid=(nseq // SEQ_GROUP,),
        in_specs=[pl.BlockSpec(memory_space=pltpu.SMEM),
                  tok(ATTN_WIDTH), tok(KV_WIDTH), tok(KV_WIDTH), cache, cache,
                  tok(CONV_WIDTH), local(CONV_WIDTH), tok(CONV_WIDTH),
                  _full((CONV_K, CONV_WIDTH))],
        out_specs=[local(D_MODEL), cache, cache],
        out_shape=[jax.ShapeDtypeStruct((nseq * dec_seq, D_MODEL), BF16),
                   jax.ShapeDtypeStruct(cache_k.shape, F32),
                   jax.ShapeDtypeStruct(cache_v.shape, F32)],
        compiler_params=_cparams(("parallel",)), name="sample_mixer",
    )(sinks, q, k, v, cache_k, cache_v, u, state_rows, gb, conv_w)


def _post_mixer_kernel(mixp_ref, mixs_ref, xn_ref, pp_ref, ps_ref, wo_ref, bo_ref, g1_ref, b1_ref,
                       wg_ref, bg_ref, wp_ref, wrh_ref, wrl_ref, br_ref,
                       x1_ref, c_ref, eid_ref, gate_ref, rank_ref, cnt_ref, carry_ref, *, n_prompt):
    @pl.when(pl.program_id(0) == 0)
    def _():
        carry_ref[...] = jnp.zeros_like(carry_ref)

    mixed = _stream_tile(n_prompt, mixp_ref, mixs_ref)
    mix = jnp.dot(mixed, wo_ref[...], preferred_element_type=F32) + bo_ref[...]
    x1 = _layer_norm(DN_ALPHA * xn_ref[...] + mix, g1_ref[...], b1_ref[...])
    x1_ref[...] = x1
    x1h = x1.astype(BF16)
    x1l = (x1 - x1h.astype(F32)).astype(BF16)
    gate = jax.nn.sigmoid(jnp.dot(x1h, wg_ref[...], preferred_element_type=F32) + bg_ref[...])
    p_tile = _stream_tile(n_prompt, pp_ref, ps_ref).astype(BF16)
    ple = jnp.dot(p_tile, wp_ref[...], preferred_element_type=F32)
    c_ref[...] = DN_ALPHA * x1 + gate * ple

    nt = (((1,), (1,)), ((), ()))
    logits = (lax.dot_general(wrh_ref[...], x1h, nt, preferred_element_type=F32)
              + lax.dot_general(wrh_ref[...], x1l, nt, preferred_element_type=F32)
              + lax.dot_general(wrl_ref[...], x1h, nt, preferred_element_type=F32)
              + br_ref[...])
    tm = logits.shape[1]
    e_i = lax.broadcasted_iota(I32, logits.shape, 0).astype(F32)
    work = logits
    vals, sels = [], []
    for k in range(TOP_K):
        m = jnp.max(work, 0, keepdims=True)
        idx = jnp.min(jnp.where(work == m, e_i, float(N_EXPERTS)), 0, keepdims=True)
        sel = e_i == idx
        vals.append(m)
        sels.append(sel)
        eid_ref[k:k + 1, :] = idx.astype(I32)
        work = jnp.where(sel, -jnp.inf, work)
    exps = [jnp.exp(v - vals[0]) for v in vals]
    denom = exps[0] + exps[1] + exps[2] + exps[3]
    for k in range(TOP_K):
        gate_ref[k:k + 1, :] = exps[k] / denom

    chosen = jnp.where(sels[0] | sels[1] | sels[2] | sels[3], 1.0, 0.0)
    s_i = lax.broadcasted_iota(I32, (tm, tm), 0)
    t_i = lax.broadcasted_iota(I32, (tm, tm), 1)
    before = jnp.where(s_i < t_i, 1.0, 0.0).astype(BF16)
    pos = carry_ref[:, 0:1] + jnp.dot(chosen.astype(BF16), before, preferred_element_type=F32)
    for k in range(TOP_K):
        rank_ref[k:k + 1, :] = jnp.sum(jnp.where(sels[k], pos, 0.0), 0, keepdims=True).astype(I32)
    carry = carry_ref[...] + jnp.sum(chosen, 1, keepdims=True)
    carry_ref[...] = carry
    cnt_ref[...] = carry


def _post_mixer(mix_p, mix_s, xn, layer, p_p, p_s, w_o, b_o, g1, b1, w_gate, b_gate, w_ple, wr_hi,
                wr_lo, b_r):
    t = xn.shape[0]
    n_p, n_s = mix_p.shape[0] // TM, mix_s.shape[0] // TM
    row = lambda w: pl.BlockSpec((TM, w), lambda i: (i, 0))
    meta = pl.BlockSpec((TOP_K, TM), lambda i: (0, i))
    vec = _full((1, D_MODEL))
    return pl.pallas_call(
        functools.partial(_post_mixer_kernel, n_prompt=n_p),
        grid=(t // TM,),
        in_specs=_stream_specs(n_p, n_s, TM, D_MODEL) + [row(D_MODEL)]
        + _stream_specs(n_p, n_s, TM, PLE_DIM, layer)
        + [_full((D_MODEL, D_MODEL)), vec, vec, vec,
           _full((D_MODEL, D_MODEL)), vec, _full((PLE_DIM, D_MODEL)),
           _full((N_EXPERTS, D_MODEL)), _full((N_EXPERTS, D_MODEL)), _full((N_EXPERTS, 1))],
        out_specs=[row(D_MODEL), row(D_MODEL), meta, meta, meta, _full((N_EXPERTS, LANES))],
        out_shape=[jax.ShapeDtypeStruct((t, D_MODEL), F32),
                   jax.ShapeDtypeStruct((t, D_MODEL), F32),
                   jax.ShapeDtypeStruct((TOP_K, t), I32),
                   jax.ShapeDtypeStruct((TOP_K, t), F32),
                   jax.ShapeDtypeStruct((TOP_K, t), I32),
                   jax.ShapeDtypeStruct((N_EXPERTS, LANES), F32)],
        scratch_shapes=[pltpu.VMEM((N_EXPERTS, LANES), F32)],
        compiler_params=_cparams(("arbitrary",)), name="post_mixer",
    )(mix_p, mix_s, xn, p_p, p_s, w_o, b_o, g1, b1, w_gate, b_gate, w_ple, wr_hi, wr_lo, b_r)


def _row_copy(src, dst, sem):
    return pltpu.make_async_copy(src, dst, sem)


def _for_each_row_slot(groups, fn):
    def group(j, carry):
        for s in range(SUBLANES):
            for k in range(TOP_K):
                fn(k, j, s, s * TOP_K + k)
        return carry

    lax.fori_loop(0, groups, group, 0)


def _hbm_row(ref, d):
    return ref.at[lax.shift_right_logical(d, SUBLANES.bit_length() - 1),
                  pl.ds(d & (SUBLANES - 1), 1)]


def _grouped_rows(x):
    return x.reshape(x.shape[0] // SUBLANES, SUBLANES, x.shape[1])


def _dispatch_kernel(dest_ref, x_ref, xs_hbm, sem):
    groups = x_ref.shape[0]
    tm = groups * SUBLANES

    def scatter_row(k, j, s, n):
        d = dest_ref[0, k * tm + j * SUBLANES + s]
        _row_copy(x_ref.at[j, pl.ds(s, 1)], _hbm_row(xs_hbm, d), sem).start(priority=n % 2)

    _for_each_row_slot(groups, scatter_row)
    for k in range(TOP_K):
        _row_copy(x_ref, xs_hbm.at[pl.ds(0, groups)], sem).wait()


def _dest_tiles(dest, tm):
    t = dest.shape[1]
    return dest.reshape(TOP_K, t // tm, tm).transpose(1, 0, 2).reshape(t // tm, 1, TOP_K * tm)


def _dispatch(dest, x1):
    t = x1.shape[0]
    tm = TM_DISPATCH
    xs = pl.pallas_call(
        _dispatch_kernel,
        grid=(t // tm,),
        in_specs=[pl.BlockSpec((None, 1, TOP_K * tm), lambda i: (i, 0, 0), memory_space=pltpu.SMEM),
                  pl.BlockSpec((tm // SUBLANES, SUBLANES, D_MODEL), lambda i: (i, 0, 0))],
        out_specs=pl.BlockSpec(memory_space=pl.ANY),
        out_shape=jax.ShapeDtypeStruct((t * TOP_K // SUBLANES, SUBLANES, D_MODEL), F32),
        scratch_shapes=[pltpu.SemaphoreType.DMA],
        compiler_params=_cparams(("arbitrary",)), name="dispatch",
    )(_dest_tiles(dest, tm), _grouped_rows(x1))
    return xs.reshape(t * TOP_K, D_MODEL)


def _moe_kernel(tile_ref, exp_ref, lo_ref, hi_ref, first_ref, newexp_ref,
                xs_ref, w1_ref, b1_ref, w2_ref, b2_ref, ys_ref, act_ref, w1b_ref, w2b_ref):
    del tile_ref, exp_ref
    w = pl.program_id(0)
    lo, hi = lo_ref[w], hi_ref[w]

    @pl.when(newexp_ref[w] == 1)
    def _():
        w1b_ref[...] = w1_ref[...].astype(BF16)
        r_i = lax.broadcasted_iota(I32, (LANES, LANES), 0)
        c_i = lax.broadcasted_iota(I32, (LANES, LANES), 1)
        src = (LANES // 2) * (r_i & 1) + lax.shift_right_logical(r_i, 1)
        perm = jnp.where(c_i == src, 1.0, 0.0).astype(BF16)
        for m in range(D_FF // LANES):
            blk = slice(m * LANES, (m + 1) * LANES)
            w2b_ref[blk, :] = jnp.dot(perm, w2_ref[blk, :].astype(BF16),
                                      preferred_element_type=F32).astype(BF16)

    def expert_mlp(r0, nrows):
        rows = slice(r0, r0 + nrows)
        xb = xs_ref[rows, :].astype(BF16)
        even = (lax.broadcasted_iota(I32, (nrows, LANES), 1) & 1) == 0
        for m in range(D_FF // LANES):
            cols = slice(2 * m * LANES, 2 * (m + 1) * LANES)
            h = jnp.dot(xb, w1b_ref[:, cols], preferred_element_type=F32) + b1_ref[:, cols]
            ha, hb = h[:, :LANES], h[:, LANES:]
            glu = jnp.where(even, ha, pltpu.roll(hb, 1, 1))
            lin = jnp.where(even, pltpu.roll(ha, LANES - 1, 1), hb)
            glu = jnp.minimum(glu, SWIGLU_LIMIT)
            lin = jnp.clip(lin, -SWIGLU_LIMIT, SWIGLU_LIMIT)
            act = glu * jax.nn.sigmoid(SWIGLU_ALPHA * glu) * (lin + 1.0)
            act_ref[rows, m * LANES:(m + 1) * LANES] = act.astype(BF16)
        y = jnp.dot(act_ref[rows, :], w2b_ref[...], preferred_element_type=F32) + b2_ref[...]
        r = r0 + lax.broadcasted_iota(I32, y.shape, 0)
        mine = (r >= lo) & (r < hi)

        @pl.when(first_ref[w] == 1)
        def _():
            if nrows < TM_MOE:
                other = slice(nrows - r0, TM_MOE - r0)
                ys_ref[other, :] = jnp.zeros((TM_MOE - nrows, D_MODEL), F32)
            ys_ref[rows, :] = jnp.where(mine, y, 0.0)

        @pl.when(first_ref[w] == 0)
        def _():
            ys_ref[rows, :] = jnp.where(mine, y, ys_ref[rows, :])

    half = TM_MOE // 2

    @pl.when((hi > lo) & (hi <= half))
    def _():
        expert_mlp(0, half)

    @pl.when((hi > lo) & (lo >= half))
    def _():
        expert_mlp(half, half)

    @pl.when((hi > lo) & (lo < half) & (hi > half))
    def _():
        expert_mlp(0, TM_MOE)


def _moe(sched, xs, layer, w1, b1, w2, b2):
    a = xs.shape[0]
    n_items = sched[0].shape[0]
    by_expert = lambda r, c: pl.BlockSpec((None, None, r, c),
                                          lambda w, tile, ex, *_: (layer, ex[w], 0, 0))
    rows = pl.BlockSpec((TM_MOE, D_MODEL), lambda w, tile, *_: (tile[w], 0))
    return pl.pallas_call(
        _moe_kernel,
        grid_spec=pltpu.PrefetchScalarGridSpec(
            num_scalar_prefetch=6, grid=(n_items,),
            in_specs=[rows, by_expert(D_MODEL, 2 * D_FF), by_expert(1, 2 * D_FF),
                      by_expert(D_FF, D_MODEL), by_expert(1, D_MODEL)],
            out_specs=rows,
            scratch_shapes=[pltpu.VMEM((TM_MOE, D_FF), BF16),
                            pltpu.VMEM((D_MODEL, 2 * D_FF), BF16),
                            pltpu.VMEM((D_FF, D_MODEL), BF16)]),
        out_shape=jax.ShapeDtypeStruct((a, D_MODEL), F32),
        compiler_params=pltpu.CompilerParams(dimension_semantics=("arbitrary",),
                                             vmem_limit_bytes=VMEM_LIMIT_MOE),
        name="moe",
    )(*sched, xs, w1, b1, w2, b2)


def _moe_schedule(counts, n_rows):
    n_tiles = n_rows // TM_MOE
    n_items = n_tiles + N_EXPERTS - 1
    pend = jnp.cumsum(counts)
    pstart = pend - counts
    first_tile = pstart // TM_MOE
    last_tile = jnp.maximum(pend - 1, 0) // TM_MOE
    ntile = jnp.where(counts > 0, last_tile - first_tile + 1, 0)
    wend = jnp.cumsum(ntile)
    wstart = wend - ntile
    total = wend[-1]
    w = jnp.arange(n_items, dtype=I32)
    wv = jnp.minimum(w, total - 1)
    ex = jnp.minimum(jnp.sum(wend[None, :] <= wv[:, None], 1), N_EXPERTS - 1).astype(I32)
    is_ex = ex[:, None] == jnp.arange(N_EXPERTS, dtype=I32)[None, :]
    of_ex = lambda per_expert: jnp.sum(jnp.where(is_ex, per_expert[None, :], 0), 1)
    tile = (of_ex(first_tile) + wv - of_ex(wstart)).astype(I32)
    valid = w < total
    lo = jnp.where(valid, jnp.clip(of_ex(pstart) - tile * TM_MOE, 0, TM_MOE), 0).astype(I32)
    hi = jnp.where(valid, jnp.clip(of_ex(pend) - tile * TM_MOE, 0, TM_MOE), 0).astype(I32)
    prev_tile = jnp.concatenate([jnp.full((1,), -1, I32), tile[:-1]])
    first = (valid & (tile != prev_tile)).astype(I32)
    prev_ex = jnp.concatenate([jnp.full((1,), -1, I32), ex[:-1]])
    newexp = (valid & (ex != prev_ex)).astype(I32)
    return (tile, ex, lo, hi, first, newexp), pstart


def _combine_kernel(dcur_ref, dnext_ref, c_ref, gate_ref, g2_ref, b2_ref, ys_hbm, *rest, n_prompt):
    *outs, buf, sem = rest
    tm = c_ref.shape[0]
    groups = tm // SUBLANES
    i = pl.program_id(0)

    def gather_tile(dest_ref, slot):
        def gather_row(k, j, s, n):
            d = dest_ref[0, k * tm + j * SUBLANES + s]
            _row_copy(_hbm_row(ys_hbm, d), buf.at[slot, k, j, pl.ds(s, 1)],
                      sem.at[slot]).start(priority=n % 2)

        _for_each_row_slot(groups, gather_row)

    @pl.when(i == 0)
    def _():
        gather_tile(dcur_ref, 0)

    @pl.when(i + 1 < pl.num_programs(0))
    def _():
        gather_tile(dnext_ref, (i + 1) % 2)

    slot = i % 2
    for k in range(TOP_K):
        _row_copy(ys_hbm.at[pl.ds(0, groups)], buf.at[slot, k], sem.at[slot]).wait()
    acc = c_ref[...]
    gates = gate_ref[...]
    for k in range(TOP_K):
        acc = acc + gates[:, k:k + 1] * buf[slot, k].reshape(tm, D_MODEL)
    y = _layer_norm(acc, g2_ref[...], b2_ref[...])
    if n_prompt is None:
        outs[0][...] = y
    else:
        @pl.when(i < n_prompt)
        def _():
            outs[0][...] = y

        @pl.when(i >= n_prompt)
        def _():
            outs[1][...] = y


def _combine(dest, c, gates_t, g2, b2, ys, t_prompt=None):
    t = c.shape[0]
    tm = TM_COMBINE
    n = t // tm
    vec = _full((1, D_MODEL))
    if t_prompt is None:
        n_p = None
        out_specs = pl.BlockSpec((tm, D_MODEL), lambda i: (i, 0))
        out_shape = jax.ShapeDtypeStruct((t, D_MODEL), F32)
    else:
        n_p = t_prompt // tm
        out_specs = _stream_specs(n_p, n - n_p, tm, D_MODEL)
        out_shape = [jax.ShapeDtypeStruct((t_prompt, D_MODEL), F32),
                     jax.ShapeDtypeStruct((t - t_prompt, D_MODEL), F32)]
    dest_spec = lambda f: pl.BlockSpec((None, 1, TOP_K * tm), lambda i: (f(i), 0, 0),
                                       memory_space=pltpu.SMEM)
    dest_tiles = _dest_tiles(dest, tm)
    return pl.pallas_call(
        functools.partial(_combine_kernel, n_prompt=n_p),
        grid=(n,),
        in_specs=[dest_spec(lambda i: i), dest_spec(lambda i: jnp.minimum(i + 1, n - 1)),
                  pl.BlockSpec((tm, D_MODEL), lambda i: (i, 0)),
                  pl.BlockSpec((tm, TOP_K), lambda i: (i, 0)),
                  vec, vec, pl.BlockSpec(memory_space=pl.ANY)],
        out_specs=out_specs,
        out_shape=out_shape,
        scratch_shapes=[pltpu.VMEM((2, TOP_K, tm // SUBLANES, SUBLANES, D_MODEL), F32),
                        pltpu.SemaphoreType.DMA((2,))],
        compiler_params=_cparams(("arbitrary",)), name="combine",
    )(dest_tiles, dest_tiles, c, gates_t, g2, b2, _grouped_rows(ys))


def kernel(x_prompt, x_sample, cache_k, cache_v, state_conv, p_prompt, p_sample, ln_emb_g, ln_emb_b,
           w_in, b_in, conv_w, sinks, w_o, b_o, ln1_g, ln1_b, w_gate, b_gate, w_ple, w_router,
           b_router, w1, b1, w2, b2, ln2_g, ln2_b):
    batch, seq, _ = x_prompt.shape
    nseq, dec_seq, _ = x_sample.shape
    t_p, t_s = batch * seq, nseq * dec_seq
    t = t_p + t_s
    w_cache = cache_k.shape[2]
    vec = lambda a: a.reshape(1, -1)

    x = (x_prompt.reshape(t_p, D_MODEL), x_sample.reshape(t_s, D_MODEL))
    state_rows = jnp.pad(state_conv, ((0, 0), (0, 0), (0, dec_seq - (CONV_K - 1)), (0, 0)))
    state_rows = state_rows.reshape(DEPTH, t_s, CONV_WIDTH)

    ks_p, vs_p, cs_p, ks_s, vs_s, cs_s = [], [], [], [], [], []
    for l in range(DEPTH):
        w_in_b = w_in[l].astype(BF16)
        xn, q, k, v, gb, u = _inproj(x, (vec(ln_emb_g), vec(ln_emb_b)), w_in_b, vec(b_in[l]))

        mix_p, nk_p, nv_p, nu_p = _prompt_mixer(sinks[l], q, k, v, u, gb, conv_w[l], batch, seq)
        ck = cache_k[l].reshape(nseq, w_cache, KV_WIDTH)
        cv = cache_v[l].reshape(nseq, w_cache, KV_WIDTH)
        mix_s, nk_s, nv_s = _sample_mixer(sinks[l], q, k, v, u, gb, conv_w[l], ck, cv,
                                          state_rows[l], t_p)

        wr_t = w_router[l].T
        wr_hi = wr_t.astype(BF16)
        wr_lo = (wr_t - wr_hi.astype(F32)).astype(BF16)
        x1, c, eid, gates, ranks, cnt = _post_mixer(
            mix_p, mix_s, xn, l, p_prompt.reshape(DEPTH, t_p, PLE_DIM),
            p_sample.reshape(DEPTH, t_s, PLE_DIM), w_o[l].astype(BF16), vec(b_o[l]), vec(ln1_g[l]), vec(ln1_b[l]),
            w_gate[l].astype(BF16), vec(b_gate[l]), w_ple[l].astype(BF16), wr_hi, wr_lo,
            b_router[l].reshape(N_EXPERTS, 1))

        counts = cnt[:, 0].astype(I32)
        sched, pstart = _moe_schedule(counts, t * TOP_K)
        onehot = eid[:, :, None] == jnp.arange(N_EXPERTS, dtype=I32)
        dest = ranks + jnp.sum(jnp.where(onehot, pstart, 0), -1)

        xs = _dispatch(dest, x1)
        ys = _moe(sched, xs, l, w1, b1[:, :, None, :], w2, b2[:, :, None, :])
        x = _combine(dest, c, gates.T, vec(ln2_g[l]), vec(ln2_b[l]), ys,
                     t_prompt=t_p if l == DEPTH - 1 else None)

        ks_p.append(nk_p.reshape(batch, WINDOW, N_KV_HEADS, HEAD_DIM))
        vs_p.append(nv_p.reshape(batch, WINDOW, N_KV_HEADS, HEAD_DIM))
        cs_p.append(nu_p[:, SUBLANES - (CONV_K - 1):])
        ks_s.append(nk_s.reshape(nseq, w_cache, N_KV_HEADS, HEAD_DIM))
        vs_s.append(nv_s.reshape(nseq, w_cache, N_KV_HEADS, HEAD_DIM))
        cs_s.append(u[t_p:].reshape(nseq, dec_seq, CONV_WIDTH)[:, dec_seq - (CONV_K - 1):])

    y_prompt = x[0].reshape(batch, seq, D_MODEL)
    y_sample = x[1].reshape(nseq, dec_seq, D_MODEL)
    return (y_prompt, y_sample, jnp.stack(ks_p), jnp.stack(vs_p), jnp.stack(cs_p),
            jnp.stack(ks_s), jnp.stack(vs_s), jnp.stack(cs_s))
```

```python
import functools

import jax
import jax.numpy as jnp
from jax import lax
from jax.experimental import pallas as pl
from jax.experimental.pallas import tpu as pltpu

F32 = jnp.float32
BF16 = jnp.bfloat16
I32 = jnp.int32

D_MODEL = 1024
DEPTH = 2
HEAD_DIM = 64
N_Q_HEADS = 8
N_KV_HEADS = 2
ATTN_WIDTH = N_Q_HEADS * HEAD_DIM
KV_WIDTH = N_KV_HEADS * HEAD_DIM
CONV_WIDTH = D_MODEL - ATTN_WIDTH
WINDOW = 128
ATTN_SCALE = HEAD_DIM ** -0.5
CONV_K = 3
N_EXPERTS = 32
TOP_K = 4
D_FF = D_MODEL
SWIGLU_LIMIT = 7.0
SWIGLU_ALPHA = 1.702
PLE_DIM = 256
LN_EPS = 1e-5
DN_ALPHA = (2.0 * DEPTH) ** 0.25
IN_COLS = ATTN_WIDTH + 2 * KV_WIDTH + 3 * CONV_WIDTH
Q0, K0, V0, GB0, GC0, H0 = 0, 512, 640, 768, 1280, 1792

LANES = 128
SUBLANES = 8
VMEM_LIMIT = 48 * 1024 * 1024
VMEM_LIMIT_MOE = 56 * 1024 * 1024

TM = 512
TQ = 512
SEQ_GROUP = 8
TM_DISPATCH = 512
TM_COMBINE = 256
TM_MOE = 512


def _layer_norm(x, g, b):
    mu = jnp.mean(x, -1, keepdims=True)
    xc = x - mu
    var = jnp.mean(xc * xc, -1, keepdims=True)
    return xc * lax.rsqrt(var + LN_EPS) * g + b


def _div_pow2(x, n):
    assert n & (n - 1) == 0
    return lax.shift_right_arithmetic(x, n.bit_length() - 1)


def _mod_pow2(x, n):
    assert n & (n - 1) == 0
    return x & (n - 1)


def _cparams(sem):
    return pltpu.CompilerParams(dimension_semantics=sem, vmem_limit_bytes=VMEM_LIMIT)


def _full(shape):
    return pl.BlockSpec(shape, lambda *_: (0,) * len(shape))


def _stream_specs(n_p, n_s, tm, width, layer=None):
    prompt_tile = lambda i: jnp.minimum(i, n_p - 1)
    sample_tile = lambda i: jnp.clip(i - n_p, 0, n_s - 1)
    if layer is None:
        return [pl.BlockSpec((tm, width), lambda i: (prompt_tile(i), 0)),
                pl.BlockSpec((tm, width), lambda i: (sample_tile(i), 0))]
    return [pl.BlockSpec((None, tm, width), lambda i: (layer, prompt_tile(i), 0)),
            pl.BlockSpec((None, tm, width), lambda i: (layer, sample_tile(i), 0))]


def _stream_tile(n_p, prompt_ref, sample_ref):
    return jnp.where(pl.program_id(0) < n_p, prompt_ref[...], sample_ref[...])


def _inproj_kernel(*refs, n_prompt):
    if n_prompt is not None:
        xp_ref, xs_ref, g_ref, b_ref, w_ref, bi_ref, xn_ref, q_ref, k_ref, v_ref, gb_ref, u_ref = refs
        x = _layer_norm(_stream_tile(n_prompt, xp_ref, xs_ref), g_ref[...], b_ref[...])
        xn_ref[...] = x
    else:
        x_ref, w_ref, bi_ref, q_ref, k_ref, v_ref, gb_ref, u_ref = refs
        x = x_ref[...]
    xb = x.astype(BF16)

    def proj(lo, hi):
        return jnp.dot(xb, w_ref[:, lo:hi], preferred_element_type=F32) + bi_ref[:, lo:hi]

    q_ref[...] = (proj(Q0, K0) * ATTN_SCALE).astype(BF16)
    k_ref[...] = proj(K0, V0)
    v_ref[...] = proj(V0, GB0)
    gb_ref[...] = proj(GB0, GC0)
    u_ref[...] = proj(GC0, H0) * proj(H0, IN_COLS)


def _inproj(x, ln, w_in_b, b_in):
    apply_ln = isinstance(x, tuple)
    row = lambda w: pl.BlockSpec((TM, w), lambda i: (i, 0))
    n_p = None
    if apply_ln:
        n_p, n_s = x[0].shape[0] // TM, x[1].shape[0] // TM
        t = x[0].shape[0] + x[1].shape[0]
        in_specs = _stream_specs(n_p, n_s, TM, D_MODEL) + [_full((1, D_MODEL)), _full((1, D_MODEL))]
        args = [x[0], x[1], ln[0], ln[1]]
    else:
        t = x.shape[0]
        in_specs = [row(D_MODEL)]
        args = [x]
    in_specs += [_full((D_MODEL, IN_COLS)), _full((1, IN_COLS))]
    args += [w_in_b, b_in]
    out_shape, out_specs = [], []
    if apply_ln:
        out_shape.append(jax.ShapeDtypeStruct((t, D_MODEL), F32))
        out_specs.append(row(D_MODEL))
    out_shape += [jax.ShapeDtypeStruct((t, ATTN_WIDTH), BF16),
                  jax.ShapeDtypeStruct((t, KV_WIDTH), F32),
                  jax.ShapeDtypeStruct((t, KV_WIDTH), F32),
                  jax.ShapeDtypeStruct((t, CONV_WIDTH), F32),
                  jax.ShapeDtypeStruct((t, CONV_WIDTH), F32)]
    out_specs += [row(ATTN_WIDTH), row(KV_WIDTH), row(KV_WIDTH), row(CONV_WIDTH), row(CONV_WIDTH)]
    outs = pl.pallas_call(
        functools.partial(_inproj_kernel, n_prompt=n_p),
        grid=(t // TM,), in_specs=in_specs, out_specs=out_specs, out_shape=out_shape,
        compiler_params=_cparams(("arbitrary",)), name="inproj")(*args)
    if apply_ln:
        return outs
    return [x] + list(outs)


def _attend_column(qcol, kexp_b, vexp_b, mask, sink_a, sink_b):
    lane = lax.broadcasted_iota(I32, qcol.shape, 1)
    outs = []
    for half, sink in ((0, sink_a), (1, sink_b)):
        keep = (lane < HEAD_DIM) if half == 0 else (lane >= HEAD_DIM)
        qm = jnp.where(keep, qcol, jnp.zeros_like(qcol))
        s = lax.dot_general(qm, kexp_b, (((1,), (1,)), ((), ())), preferred_element_type=F32)
        s = jnp.where(mask, s, -jnp.inf)
        m = jnp.maximum(jnp.max(s, -1, keepdims=True), sink)
        p = jnp.exp(s - m)
        denom = jnp.sum(p, -1, keepdims=True) + jnp.exp(sink - m)
        o = jnp.dot(p.astype(BF16), vexp_b, preferred_element_type=F32)
        outs.append(o * (1.0 / denom))
    return jnp.where(lane < HEAD_DIM, outs[0], outs[1])


def _dup_heads(x):
    lane = lax.broadcasted_iota(I32, x.shape, 1)
    xr = pltpu.roll(x, HEAD_DIM, 1)
    low = lane < HEAD_DIM
    return (jnp.where(low, x, xr).astype(BF16), jnp.where(low, xr, x).astype(BF16))


def _prompt_mixer_kernel(sinks_ref, q_ref, kc_ref, kp_ref, vc_ref, vp_ref, uc_ref, up_ref,
                         gb_ref, cw_ref, o_ref, nk_ref, nv_ref, nu_ref):
    first = pl.program_id(1) == 0

    @pl.when(pl.program_id(1) == pl.num_programs(1) - 1)
    def _():
        nk_ref[...] = kc_ref[TQ - WINDOW:, :]
        nv_ref[...] = vc_ref[TQ - WINDOW:, :]
        nu_ref[...] = uc_ref[TQ - SUBLANES:, :]

    r_i = lax.broadcasted_iota(I32, (WINDOW, 2 * WINDOW), 0)
    c_i = lax.broadcasted_iota(I32, (WINDOW, 2 * WINDOW), 1)
    band = (c_i <= WINDOW + r_i) & (c_i > r_i)
    band_first = band & (c_i >= WINDOW * first.astype(I32))
    for s in range(TQ // WINDOW):
        rows = slice(s * WINDOW, (s + 1) * WINDOW)
        if s == 0:
            kprev, vprev = kp_ref[...], vp_ref[...]
        else:
            prev = slice((s - 1) * WINDOW, s * WINDOW)
            kprev, vprev = kc_ref[prev, :], vc_ref[prev, :]
        kk = jnp.concatenate([kprev, kc_ref[rows, :]], 0)
        vv = jnp.concatenate([vprev, vc_ref[rows, :]], 0)
        kexp = _dup_heads(kk)
        vexp = _dup_heads(vv)
        mask = band_first if s == 0 else band
        for col in range(ATTN_WIDTH // LANES):
            h = col // 2
            cols = slice(col * LANES, (col + 1) * LANES)
            out = _attend_column(q_ref[rows, cols], kexp[h], vexp[h], mask,
                                 sinks_ref[2 * col], sinks_ref[2 * col + 1])
            o_ref[rows, cols] = out.astype(BF16)

    u = uc_ref[...]
    up = up_ref[...]
    zero = jnp.zeros((1, CONV_WIDTH), F32)
    p1 = jnp.where(first, zero, up[SUBLANES - 1:SUBLANES, :])
    p2 = jnp.where(first, zero, up[SUBLANES - 2:SUBLANES - 1, :])
    row = lax.broadcasted_iota(I32, u.shape, 0)
    u1 = jnp.where(row == 0, p1, pltpu.roll(u, 1, 0))
    u2 = jnp.where(row == 0, p2, jnp.where(row == 1, p1, pltpu.roll(u, 2, 0)))
    cw = cw_ref[...]
    y = u2 * cw[0:1, :] + u1 * cw[1:2, :] + u * cw[2:3, :]
    o_ref[:, ATTN_WIDTH:] = (gb_ref[...] * y).astype(BF16)


def _prompt_mixer(sinks, q, k, v, u, gb, conv_w, batch, seq):
    nj = seq // TQ
    tile = lambda b, j: b * nj + j
    cur = lambda w: pl.BlockSpec((TQ, w), lambda b, j: (tile(b, j), 0))
    prev_kv = pl.BlockSpec((WINDOW, KV_WIDTH),
                           lambda b, j: (jnp.maximum(tile(b, j) * (TQ // WINDOW) - 1, 0), 0))
    prev_u = pl.BlockSpec((SUBLANES, CONV_WIDTH),
                          lambda b, j: (jnp.maximum(tile(b, j) * (TQ // SUBLANES) - 1, 0), 0))
    return pl.pallas_call(
        _prompt_mixer_kernel,
        grid=(batch, nj),
        in_specs=[pl.BlockSpec(memory_space=pltpu.SMEM),
                  cur(ATTN_WIDTH), cur(KV_WIDTH), prev_kv, cur(KV_WIDTH), prev_kv,
                  cur(CONV_WIDTH), prev_u, cur(CONV_WIDTH), _full((CONV_K, CONV_WIDTH))],
        out_specs=[cur(D_MODEL),
                   pl.BlockSpec((None, WINDOW, KV_WIDTH), lambda b, j: (b, 0, 0)),
                   pl.BlockSpec((None, WINDOW, KV_WIDTH), lambda b, j: (b, 0, 0)),
                   pl.BlockSpec((None, SUBLANES, CONV_WIDTH), lambda b, j: (b, 0, 0))],
        out_shape=[jax.ShapeDtypeStruct((batch * seq, D_MODEL), BF16),
                   jax.ShapeDtypeStruct((batch, WINDOW, KV_WIDTH), F32),
                   jax.ShapeDtypeStruct((batch, WINDOW, KV_WIDTH), F32),
                   jax.ShapeDtypeStruct((batch, SUBLANES, CONV_WIDTH), F32)],
        compiler_params=_cparams(("arbitrary", "arbitrary")), name="prompt_mixer",
    )(sinks, q, k, k, v, v, u, u, gb, conv_w)


def _sample_mixer_kernel(sinks_ref, q_ref, kn_ref, vn_ref, kb_ref, vb_ref, u_ref, st_ref, gb_ref,
                         cw_ref, o_ref, nk_ref, nv_ref, *, dec_seq):
    g, w = kb_ref.shape[0], kb_ref.shape[1]
    rows = g * dec_seq
    n_cache = g * w
    n_keys = n_cache + 2 * rows
    kn, vn = kn_ref[...], vn_ref[...]
    pad = jnp.zeros((rows, KV_WIDTH), F32)
    kk = jnp.concatenate([kb_ref[...].reshape(n_cache, KV_WIDTH), kn, pad], 0)
    vv = jnp.concatenate([vb_ref[...].reshape(n_cache, KV_WIDTH), vn, pad], 0)
    kexp = _dup_heads(kk)
    vexp = _dup_heads(vv)

    r_i = lax.broadcasted_iota(I32, (rows, n_keys), 0)
    c_i = lax.broadcasted_iota(I32, (rows, n_keys), 1)
    r_seq, r_pos = _div_pow2(r_i, dec_seq), _mod_pow2(r_i, dec_seq)
    c_new = c_i - n_cache
    in_cache = ((c_i < n_cache) & (_div_pow2(c_i, w) == r_seq)
                & (_mod_pow2(c_i, w) > r_pos + (w - WINDOW)))
    in_new = ((c_new >= 0) & (c_new < rows) & (_div_pow2(c_new, dec_seq) == r_seq)
              & (_mod_pow2(c_new, dec_seq) <= r_pos))
    mask = in_cache | in_new
    for col in range(ATTN_WIDTH // LANES):
        h = col // 2
        cols = slice(col * LANES, (col + 1) * LANES)
        out = _attend_column(q_ref[:, cols], kexp[h], vexp[h], mask,
                             sinks_ref[2 * col], sinks_ref[2 * col + 1])
        o_ref[:, cols] = out.astype(BF16)

    nk_ref[:, 0:w - dec_seq, :] = kb_ref[:, dec_seq:w, :]
    nk_ref[:, w - dec_seq:w, :] = kn.reshape(g, dec_seq, KV_WIDTH)
    nv_ref[:, 0:w - dec_seq, :] = vb_ref[:, dec_seq:w, :]
    nv_ref[:, w - dec_seq:w, :] = vn.reshape(g, dec_seq, KV_WIDTH)

    u = u_ref[...]
    st = st_ref[...]
    pos = lax.broadcasted_iota(I32, u.shape, 0) % dec_seq
    u1 = jnp.where(pos == 0, pltpu.roll(st, rows - 1, 0), pltpu.roll(u, 1, 0))
    u2 = jnp.where(pos < 2, st, pltpu.roll(u, 2, 0))
    cw = cw_ref[...]
    y = u2 * cw[0:1, :] + u1 * cw[1:2, :] + u * cw[2:3, :]
    o_ref[:, ATTN_WIDTH:] = (gb_ref[...] * y).astype(BF16)


def _sample_mixer(sinks, q, k, v, u, gb, conv_w, cache_k, cache_v, state_rows, t_prompt):
    nseq, w = cache_k.shape[0], cache_k.shape[1]
    dec_seq = (q.shape[0] - t_prompt) // nseq
    rows = SEQ_GROUP * dec_seq
    off = t_prompt // rows
    tok = lambda wd: pl.BlockSpec((rows, wd), lambda i: (off + i, 0))
    local = lambda wd: pl.BlockSpec((rows, wd), lambda i: (i, 0))
    cache = pl.BlockSpec((SEQ_GROUP, w, KV_WIDTH), lambda i: (i, 0, 0))
    return pl.pallas_call(
        functools.partial(_sample_mixer_kernel, dec_seq=dec_seq),
        grid=(nseq // SEQ_GROUP,),
        in_specs=[pl.BlockSpec(memory_space=pltpu.SMEM),
                  tok(ATTN_WIDTH), tok(KV_WIDTH), tok(KV_WIDTH), cache, cache,
                  tok(CONV_WIDTH), local(CONV_WIDTH), tok(CONV_WIDTH),
                  _full((CONV_K, CONV_WIDTH))],
        out_specs=[local(D_MODEL), cache, cache],
        out_shape=[jax.ShapeDtypeStruct((nseq * dec_seq, D_MODEL), BF16),
                   jax.ShapeDtypeStruct(cache_k.shape, F32),
                   jax.ShapeDtypeStruct(cache_v.shape, F32)],
        compiler_params=_cparams(("parallel",)), name="sample_mixer",
    )(sinks, q, k, v, cache_k, cache_v, u, state_rows, gb, conv_w)


def _post_mixer_kernel(mixp_ref, mixs_ref, xn_ref, pp_ref, ps_ref, wo_ref, bo_ref, g1_ref, b1_ref,
                       wg_ref, bg_ref, wp_ref, wrh_ref, wrl_ref, br_ref,
                       x1_ref, c_ref, eid_ref, gate_ref, rank_ref, cnt_ref, carry_ref, *, n_prompt):
    @pl.when(pl.program_id(0) == 0)
    def _():
        carry_ref[...] = jnp.zeros_like(carry_ref)

    mixed = _stream_tile(n_prompt, mixp_ref, mixs_ref)
    mix = jnp.dot(mixed, wo_ref[...], preferred_element_type=F32) + bo_ref[...]
    x1 = _layer_norm(DN_ALPHA * xn_ref[...] + mix, g1_ref[...], b1_ref[...])
    for c in range(CHUNKS):
        x1_ref[pl.ds(c, x1.shape[0], stride=CHUNKS), :] = x1[:, c * LANES:(c + 1) * LANES]
    x1h = x1.astype(BF16)
    x1l = (x1 - x1h.astype(F32)).astype(BF16)
    gate = jax.nn.sigmoid(jnp.dot(x1h, wg_ref[...], preferred_element_type=F32) + bg_ref[...])
    p_tile = _stream_tile(n_prompt, pp_ref, ps_ref).astype(BF16)
    ple = jnp.dot(p_tile, wp_ref[...], preferred_element_type=F32)
    c_ref[...] = DN_ALPHA * x1 + gate * ple

    nt = (((1,), (1,)), ((), ()))
    logits = (lax.dot_general(wrh_ref[...], x1h, nt, preferred_element_type=F32)
              + lax.dot_general(wrh_ref[...], x1l, nt, preferred_element_type=F32)
              + lax.dot_general(wrl_ref[...], x1h, nt, preferred_element_type=F32)
              + br_ref[...])
    tm = logits.shape[1]
    e_i = lax.broadcasted_iota(I32, logits.shape, 0).astype(F32)
    work = logits
    vals, sels = [], []
    for k in range(TOP_K):
        m = jnp.max(work, 0, keepdims=True)
        idx = jnp.min(jnp.where(work == m, e_i, float(N_EXPERTS)), 0, keepdims=True)
        sel = e_i == idx
        vals.append(m)
        sels.append(sel)
        eid_ref[k:k + 1, :] = idx.astype(I32)
        work = jnp.where(sel, -jnp.inf, work)
    exps = [jnp.exp(v - vals[0]) for v in vals]
    denom = exps[0] + exps[1] + exps[2] + exps[3]
    for k in range(TOP_K):
        gate_ref[k:k + 1, :] = exps[k] / denom

    chosen = jnp.where(sels[0] | sels[1] | sels[2] | sels[3], 1.0, 0.0)
    s_i = lax.broadcasted_iota(I32, (tm, tm), 0)
    t_i = lax.broadcasted_iota(I32, (tm, tm), 1)
    before = jnp.where(s_i < t_i, 1.0, 0.0).astype(BF16)
    pos = carry_ref[:, 0:1] + jnp.dot(chosen.astype(BF16), before, preferred_element_type=F32)
    for k in range(TOP_K):
        rank_ref[k:k + 1, :] = jnp.sum(jnp.where(sels[k], pos, 0.0), 0, keepdims=True).astype(I32)
    carry = carry_ref[...] + jnp.sum(chosen, 1, keepdims=True)
    carry_ref[...] = carry
    cnt_ref[...] = carry


def _post_mixer(mix_p, mix_s, xn, layer, p_p, p_s, w_o, b_o, g1, b1, w_gate, b_gate, w_ple, wr_hi,
                wr_lo, b_r):
    t = xn.shape[0]
    n_p, n_s = mix_p.shape[0] // TM, mix_s.shape[0] // TM
    row = lambda w: pl.BlockSpec((TM, w), lambda i: (i, 0))
    meta = pl.BlockSpec((TOP_K, TM), lambda i: (0, i))
    vec = _full((1, D_MODEL))
    return pl.pallas_call(
        functools.partial(_post_mixer_kernel, n_prompt=n_p),
        grid=(t // TM,),
        in_specs=_stream_specs(n_p, n_s, TM, D_MODEL) + [row(D_MODEL)]
        + _stream_specs(n_p, n_s, TM, PLE_DIM, layer)
        + [_full((D_MODEL, D_MODEL)), vec, vec, vec,
           _full((D_MODEL, D_MODEL)), vec, _full((PLE_DIM, D_MODEL)),
           _full((N_EXPERTS, D_MODEL)), _full((N_EXPERTS, D_MODEL)), _full((N_EXPERTS, 1))],
        out_specs=[pl.BlockSpec((TM * CHUNKS, LANES), lambda i: (i, 0)), row(D_MODEL),
                   meta, meta, meta, _full((N_EXPERTS, LANES))],
        out_shape=[jax.ShapeDtypeStruct((t * CHUNKS, LANES), F32),
                   jax.ShapeDtypeStruct((t, D_MODEL), F32),
                   jax.ShapeDtypeStruct((TOP_K, t), I32),
                   jax.ShapeDtypeStruct((TOP_K, t), F32),
                   jax.ShapeDtypeStruct((TOP_K, t), I32),
                   jax.ShapeDtypeStruct((N_EXPERTS, LANES), F32)],
        scratch_shapes=[pltpu.VMEM((N_EXPERTS, LANES), F32)],
        compiler_params=_cparams(("arbitrary",)), name="post_mixer",
    )(mix_p, mix_s, xn, p_p, p_s, w_o, b_o, g1, b1, w_gate, b_gate, w_ple, wr_hi, wr_lo, b_r)


def _row_copy(src, dst, sem):
    return pltpu.make_async_copy(src, dst, sem)


CHUNKS = D_MODEL // LANES
assert CHUNKS == SUBLANES


def _load_token_tiles(ref, r0, n):
    return jnp.concatenate(
        [ref[pl.ds(r0 * CHUNKS + c, n, stride=CHUNKS), :] for c in range(CHUNKS)], axis=1)


def _token_tiles(x):
    if x.ndim == 2:
        return x.reshape(x.shape[0] // CHUNKS, CHUNKS, LANES)
    return x.reshape(x.shape[0] * CHUNKS, LANES)


def _for_each_row_slot(groups, fn):
    def group(j, carry):
        for s in range(SUBLANES):
            for k in range(TOP_K):
                fn(k, j, s, s * TOP_K + k)
        return carry

    lax.fori_loop(0, groups, group, 0)


def _dispatch_kernel(dest_ref, x_ref, xs_hbm, sem):
    tm = x_ref.shape[0]

    def scatter_row(k, j, s, n):
        r = j * SUBLANES + s
        _row_copy(x_ref.at[r], xs_hbm.at[dest_ref[0, k * tm + r]], sem).start(priority=n % 2)

    _for_each_row_slot(tm // SUBLANES, scatter_row)
    for k in range(TOP_K):
        _row_copy(x_ref, xs_hbm.at[pl.ds(0, tm)], sem).wait()


def _dest_tiles(dest, tm):
    t = dest.shape[1]
    return dest.reshape(TOP_K, t // tm, tm).transpose(1, 0, 2).reshape(t // tm, 1, TOP_K * tm)


def _dispatch(dest, x1t):
    t = x1t.shape[0] // CHUNKS
    tm = TM_DISPATCH
    xs = pl.pallas_call(
        _dispatch_kernel,
        grid=(t // tm,),
        in_specs=[pl.BlockSpec((None, 1, TOP_K * tm), lambda i: (i, 0, 0), memory_space=pltpu.SMEM),
                  pl.BlockSpec((tm, CHUNKS, LANES), lambda i: (i, 0, 0))],
        out_specs=pl.BlockSpec(memory_space=pl.ANY),
        out_shape=jax.ShapeDtypeStruct((t * TOP_K, CHUNKS, LANES), F32),
        scratch_shapes=[pltpu.SemaphoreType.DMA],
        compiler_params=_cparams(("arbitrary",)), name="dispatch",
    )(_dest_tiles(dest, tm), _token_tiles(x1t))
    return _token_tiles(xs)


def _moe_kernel(tile_ref, exp_ref, lo_ref, hi_ref, first_ref, newexp_ref,
                xs_ref, w1_ref, b1_ref, w2_ref, b2_ref, ys_ref, act_ref, w1b_ref, w2b_ref):
    del tile_ref, exp_ref
    w = pl.program_id(0)
    lo, hi = lo_ref[w], hi_ref[w]

    @pl.when(newexp_ref[w] == 1)
    def _():
        w1b_ref[...] = w1_ref[...].astype(BF16)
        r_i = lax.broadcasted_iota(I32, (LANES, LANES), 0)
        c_i = lax.broadcasted_iota(I32, (LANES, LANES), 1)
        src = (LANES // 2) * (r_i & 1) + lax.shift_right_logical(r_i, 1)
        perm = jnp.where(c_i == src, 1.0, 0.0).astype(BF16)
        for m in range(D_FF // LANES):
            blk = slice(m * LANES, (m + 1) * LANES)
            w2b_ref[blk, :] = jnp.dot(perm, w2_ref[blk, :].astype(BF16),
                                      preferred_element_type=F32).astype(BF16)

    def expert_mlp(r0, nrows):
        rows = slice(r0, r0 + nrows)
        xb = _load_token_tiles(xs_ref, r0, nrows).astype(BF16)
        even = (lax.broadcasted_iota(I32, (nrows, LANES), 1) & 1) == 0
        for m in range(D_FF // LANES):
            cols = slice(2 * m * LANES, 2 * (m + 1) * LANES)
            h = jnp.dot(xb, w1b_ref[:, cols], preferred_element_type=F32) + b1_ref[:, cols]
            ha, hb = h[:, :LANES], h[:, LANES:]
            glu = jnp.where(even, ha, pltpu.roll(hb, 1, 1))
            lin = jnp.where(even, pltpu.roll(ha, LANES - 1, 1), hb)
            glu = jnp.minimum(glu, SWIGLU_LIMIT)
            lin = jnp.clip(lin, -SWIGLU_LIMIT, SWIGLU_LIMIT)
            act = glu * jax.nn.sigmoid(SWIGLU_ALPHA * glu) * (lin + 1.0)
            act_ref[rows, m * LANES:(m + 1) * LANES] = act.astype(BF16)
        y = jnp.dot(act_ref[rows, :], w2b_ref[...], preferred_element_type=F32) + b2_ref[...]
        r = r0 + lax.broadcasted_iota(I32, (nrows, LANES), 0)
        mine = (r >= lo) & (r < hi)

        def put(first_visit):
            for c in range(CHUNKS):
                at = pl.ds(r0 * CHUNKS + c, nrows, stride=CHUNKS)
                old = 0.0 if first_visit else ys_ref[at, :]
                ys_ref[at, :] = jnp.where(mine, y[:, c * LANES:(c + 1) * LANES], old)

        @pl.when(first_ref[w] == 1)
        def _():
            if nrows < TM_MOE:
                ys_ref[pl.ds((nrows - r0) * CHUNKS, (TM_MOE - nrows) * CHUNKS), :] = jnp.zeros(
                    ((TM_MOE - nrows) * CHUNKS, LANES), F32)
            put(True)

        @pl.when(first_ref[w] == 0)
        def _():
            put(False)

    half = TM_MOE // 2

    @pl.when((hi > lo) & (hi <= half))
    def _():
        expert_mlp(0, half)

    @pl.when((hi > lo) & (lo >= half))
    def _():
        expert_mlp(half, half)

    @pl.when((hi > lo) & (lo < half) & (hi > half))
    def _():
        expert_mlp(0, TM_MOE)


def _moe(sched, xs, layer, w1, b1, w2, b2):
    a = xs.shape[0] // CHUNKS
    n_items = sched[0].shape[0]
    by_expert = lambda r, c: pl.BlockSpec((None, None, r, c),
                                          lambda w, tile, ex, *_: (layer, ex[w], 0, 0))
    rows = pl.BlockSpec((TM_MOE * CHUNKS, LANES), lambda w, tile, *_: (tile[w], 0))
    return pl.pallas_call(
        _moe_kernel,
        grid_spec=pltpu.PrefetchScalarGridSpec(
            num_scalar_prefetch=6, grid=(n_items,),
            in_specs=[rows, by_expert(D_MODEL, 2 * D_FF), by_expert(1, 2 * D_FF),
                      by_expert(D_FF, D_MODEL), by_expert(1, D_MODEL)],
            out_specs=rows,
            scratch_shapes=[pltpu.VMEM((TM_MOE, D_FF), BF16),
                            pltpu.VMEM((D_MODEL, 2 * D_FF), BF16),
                            pltpu.VMEM((D_FF, D_MODEL), BF16)]),
        out_shape=jax.ShapeDtypeStruct((a * CHUNKS, LANES), F32),
        compiler_params=pltpu.CompilerParams(dimension_semantics=("arbitrary",),
                                             vmem_limit_bytes=VMEM_LIMIT_MOE),
        name="moe",
    )(*sched, xs, w1, b1, w2, b2)


def _moe_schedule(counts, n_rows):
    n_tiles = n_rows // TM_MOE
    n_items = n_tiles + N_EXPERTS - 1
    pend = jnp.cumsum(counts)
    pstart = pend - counts
    first_tile = pstart // TM_MOE
    last_tile = jnp.maximum(pend - 1, 0) // TM_MOE
    ntile = jnp.where(counts > 0, last_tile - first_tile + 1, 0)
    wend = jnp.cumsum(ntile)
    wstart = wend - ntile
    total = wend[-1]
    w = jnp.arange(n_items, dtype=I32)
    wv = jnp.minimum(w, total - 1)
    ex = jnp.minimum(jnp.sum(wend[None, :] <= wv[:, None], 1), N_EXPERTS - 1).astype(I32)
    is_ex = ex[:, None] == jnp.arange(N_EXPERTS, dtype=I32)[None, :]
    of_ex = lambda per_expert: jnp.sum(jnp.where(is_ex, per_expert[None, :], 0), 1)
    tile = (of_ex(first_tile) + wv - of_ex(wstart)).astype(I32)
    valid = w < total
    lo = jnp.where(valid, jnp.clip(of_ex(pstart) - tile * TM_MOE, 0, TM_MOE), 0).astype(I32)
    hi = jnp.where(valid, jnp.clip(of_ex(pend) - tile * TM_MOE, 0, TM_MOE), 0).astype(I32)
    prev_tile = jnp.concatenate([jnp.full((1,), -1, I32), tile[:-1]])
    first = (valid & (tile != prev_tile)).astype(I32)
    prev_ex = jnp.concatenate([jnp.full((1,), -1, I32), ex[:-1]])
    newexp = (valid & (ex != prev_ex)).astype(I32)
    return (tile, ex, lo, hi, first, newexp), pstart


def _combine_kernel(dcur_ref, dnext_ref, c_ref, gate_ref, g2_ref, b2_ref, ys_hbm, ys_flat_hbm, *rest,
                    n_prompt):
    *outs, buf, sem = rest
    tm = c_ref.shape[0]
    i = pl.program_id(0)

    def gather_tile(dest_ref, slot):
        def gather_row(k, j, s, n):
            r = j * SUBLANES + s
            dst = buf.at[slot, k, pl.ds(pl.multiple_of(r * CHUNKS, CHUNKS), CHUNKS)]
            _row_copy(ys_hbm.at[dest_ref[0, k * tm + r]], dst, sem.at[slot]).start(priority=n % 2)

        _for_each_row_slot(tm // SUBLANES, gather_row)

    @pl.when(i == 0)
    def _():
        gather_tile(dcur_ref, 0)

    @pl.when(i + 1 < pl.num_programs(0))
    def _():
        gather_tile(dnext_ref, (i + 1) % 2)

    slot = i % 2
    for k in range(TOP_K):
        _row_copy(ys_flat_hbm.at[pl.ds(0, tm * CHUNKS)], buf.at[slot, k], sem.at[slot]).wait()
    gates = gate_ref[...]
    parts = []
    for c in range(CHUNKS):
        part = c_ref[:, c * LANES:(c + 1) * LANES]
        for k in range(TOP_K):
            part = part + gates[:, k:k + 1] * buf[slot, k, pl.ds(c, tm, stride=CHUNKS), :]
        parts.append(part)
    acc = jnp.concatenate(parts, axis=1)
    y = _layer_norm(acc, g2_ref[...], b2_ref[...])
    if n_prompt is None:
        outs[0][...] = y
    else:
        @pl.when(i < n_prompt)
        def _():
            outs[0][...] = y

        @pl.when(i >= n_prompt)
        def _():
            outs[1][...] = y


def _combine(dest, c, gates_t, g2, b2, ys, t_prompt=None):
    t = c.shape[0]
    tm = TM_COMBINE
    n = t // tm
    vec = _full((1, D_MODEL))
    if t_prompt is None:
        n_p = None
        out_specs = pl.BlockSpec((tm, D_MODEL), lambda i: (i, 0))
        out_shape = jax.ShapeDtypeStruct((t, D_MODEL), F32)
    else:
        n_p = t_prompt // tm
        out_specs = _stream_specs(n_p, n - n_p, tm, D_MODEL)
        out_shape = [jax.ShapeDtypeStruct((t_prompt, D_MODEL), F32),
                     jax.ShapeDtypeStruct((t - t_prompt, D_MODEL), F32)]
    dest_spec = lambda f: pl.BlockSpec((None, 1, TOP_K * tm), lambda i: (f(i), 0, 0),
                                       memory_space=pltpu.SMEM)
    dest_tiles = _dest_tiles(dest, tm)
    return pl.pallas_call(
        functools.partial(_combine_kernel, n_prompt=n_p),
        grid=(n,),
        in_specs=[dest_spec(lambda i: i), dest_spec(lambda i: jnp.minimum(i + 1, n - 1)),
                  pl.BlockSpec((tm, D_MODEL), lambda i: (i, 0)),
                  pl.BlockSpec((tm, TOP_K), lambda i: (i, 0)),
                  vec, vec, pl.BlockSpec(memory_space=pl.ANY), pl.BlockSpec(memory_space=pl.ANY)],
        out_specs=out_specs,
        out_shape=out_shape,
        scratch_shapes=[pltpu.VMEM((2, TOP_K, tm * CHUNKS, LANES), F32),
                        pltpu.SemaphoreType.DMA((2,))],
        compiler_params=_cparams(("arbitrary",)), name="combine",
    )(dest_tiles, dest_tiles, c, gates_t, g2, b2, _token_tiles(ys), ys)


def kernel(x_prompt, x_sample, cache_k, cache_v, state_conv, p_prompt, p_sample, ln_emb_g, ln_emb_b,
           w_in, b_in, conv_w, sinks, w_o, b_o, ln1_g, ln1_b, w_gate, b_gate, w_ple, w_router,
           b_router, w1, b1, w2, b2, ln2_g, ln2_b):
    batch, seq, _ = x_prompt.shape
    nseq, dec_seq, _ = x_sample.shape
    t_p, t_s = batch * seq, nseq * dec_seq
    t = t_p + t_s
    w_cache = cache_k.shape[2]
    vec = lambda a: a.reshape(1, -1)

    x = (x_prompt.reshape(t_p, D_MODEL), x_sample.reshape(t_s, D_MODEL))
    state_rows = jnp.pad(state_conv, ((0, 0), (0, 0), (0, dec_seq - (CONV_K - 1)), (0, 0)))
    state_rows = state_rows.reshape(DEPTH, t_s, CONV_WIDTH)

    ks_p, vs_p, cs_p, ks_s, vs_s, cs_s = [], [], [], [], [], []
    for l in range(DEPTH):
        w_in_b = w_in[l].astype(BF16)
        xn, q, k, v, gb, u = _inproj(x, (vec(ln_emb_g), vec(ln_emb_b)), w_in_b, vec(b_in[l]))

        mix_p, nk_p, nv_p, nu_p = _prompt_mixer(sinks[l], q, k, v, u, gb, conv_w[l], batch, seq)
        ck = cache_k[l].reshape(nseq, w_cache, KV_WIDTH)
        cv = cache_v[l].reshape(nseq, w_cache, KV_WIDTH)
        mix_s, nk_s, nv_s = _sample_mixer(sinks[l], q, k, v, u, gb, conv_w[l], ck, cv,
                                          state_rows[l], t_p)

        wr_t = w_router[l].T
        wr_hi = wr_t.astype(BF16)
        wr_lo = (wr_t - wr_hi.astype(F32)).astype(BF16)
        x1, c, eid, gates, ranks, cnt = _post_mixer(
            mix_p, mix_s, xn, l, p_prompt.reshape(DEPTH, t_p, PLE_DIM),
            p_sample.reshape(DEPTH, t_s, PLE_DIM), w_o[l].astype(BF16), vec(b_o[l]), vec(ln1_g[l]), vec(ln1_b[l]),
            w_gate[l].astype(BF16), vec(b_gate[l]), w_ple[l].astype(BF16), wr_hi, wr_lo,
            b_router[l].reshape(N_EXPERTS, 1))

        counts = cnt[:, 0].astype(I32)
        sched, pstart = _moe_schedule(counts, t * TOP_K)
        onehot = eid[:, :, None] == jnp.arange(N_EXPERTS, dtype=I32)
        dest = ranks + jnp.sum(jnp.where(onehot, pstart, 0), -1)

        xs = _dispatch(dest, x1)
        ys = _moe(sched, xs, l, w1, b1[:, :, None, :], w2, b2[:, :, None, :])
        x = _combine(dest, c, gates.T, vec(ln2_g[l]), vec(ln2_b[l]), ys,
                     t_prompt=t_p if l == DEPTH - 1 else None)

        ks_p.append(nk_p.reshape(batch, WINDOW, N_KV_HEADS, HEAD_DIM))
        vs_p.append(nv_p.reshape(batch, WINDOW, N_KV_HEADS, HEAD_DIM))
        cs_p.append(nu_p[:, SUBLANES - (CONV_K - 1):])
        ks_s.append(nk_s.reshape(nseq, w_cache, N_KV_HEADS, HEAD_DIM))
        vs_s.append(nv_s.reshape(nseq, w_cache, N_KV_HEADS, HEAD_DIM))
        cs_s.append(u[t_p:].reshape(nseq, dec_seq, CONV_WIDTH)[:, dec_seq - (CONV_K - 1):])

    y_prompt = x[0].reshape(batch, seq, D_MODEL)
    y_sample = x[1].reshape(nseq, dec_seq, D_MODEL)
    return (y_prompt, y_sample, jnp.stack(ks_p), jnp.stack(vs_p), jnp.stack(cs_p),
            jnp.stack(ks_s), jnp.stack(vs_s), jnp.stack(cs_s))
```

```python
import functools

import jax
import jax.numpy as jnp
from jax import lax
from jax.experimental import pallas as pl
from jax.experimental.pallas import tpu as pltpu

F32 = jnp.float32
BF16 = jnp.bfloat16
I32 = jnp.int32

D_MODEL = 1024
DEPTH = 2
HEAD_DIM = 64
N_Q_HEADS = 8
N_KV_HEADS = 2
ATTN_WIDTH = N_Q_HEADS * HEAD_DIM
KV_WIDTH = N_KV_HEADS * HEAD_DIM
CONV_WIDTH = D_MODEL - ATTN_WIDTH
WINDOW = 128
ATTN_SCALE = HEAD_DIM ** -0.5
CONV_K = 3
N_EXPERTS = 32
TOP_K = 4
D_FF = D_MODEL
SWIGLU_LIMIT = 7.0
SWIGLU_ALPHA = 1.702
PLE_DIM = 256
LN_EPS = 1e-5
DN_ALPHA = (2.0 * DEPTH) ** 0.25
IN_COLS = ATTN_WIDTH + 2 * KV_WIDTH + 3 * CONV_WIDTH
Q0, K0, V0, GB0, GC0, H0 = 0, 512, 640, 768, 1280, 1792

LANES = 128
SUBLANES = 8
VMEM_LIMIT = 48 * 1024 * 1024
VMEM_LIMIT_MOE = 56 * 1024 * 1024

TM = 512
TQ = 512
SEQ_GROUP = 8
TM_DISPATCH = 512
TM_COMBINE = 256
TM_MOE = 512


def _layer_norm(x, g, b):
    mu = jnp.mean(x, -1, keepdims=True)
    xc = x - mu
    var = jnp.mean(xc * xc, -1, keepdims=True)
    return xc * lax.rsqrt(var + LN_EPS) * g + b


def _div_pow2(x, n):
    assert n & (n - 1) == 0
    return lax.shift_right_arithmetic(x, n.bit_length() - 1)


def _mod_pow2(x, n):
    assert n & (n - 1) == 0
    return x & (n - 1)


def _cparams(sem):
    return pltpu.CompilerParams(dimension_semantics=sem, vmem_limit_bytes=VMEM_LIMIT)


def _full(shape):
    return pl.BlockSpec(shape, lambda *_: (0,) * len(shape))


def _stream_specs(n_p, n_s, tm, width, layer=None):
    prompt_tile = lambda i: jnp.minimum(i, n_p - 1)
    sample_tile = lambda i: jnp.clip(i - n_p, 0, n_s - 1)
    if layer is None:
        return [pl.BlockSpec((tm, width), lambda i: (prompt_tile(i), 0)),
                pl.BlockSpec((tm, width), lambda i: (sample_tile(i), 0))]
    return [pl.BlockSpec((None, tm, width), lambda i: (layer, prompt_tile(i), 0)),
            pl.BlockSpec((None, tm, width), lambda i: (layer, sample_tile(i), 0))]


def _stream_tile(n_p, prompt_ref, sample_ref):
    return jnp.where(pl.program_id(0) < n_p, prompt_ref[...], sample_ref[...])


def _inproj_kernel(*refs, n_prompt):
    if n_prompt is not None:
        xp_ref, xs_ref, g_ref, b_ref, w_ref, bi_ref, xn_ref, q_ref, k_ref, v_ref, gb_ref, u_ref = refs
        x = _layer_norm(_stream_tile(n_prompt, xp_ref, xs_ref), g_ref[...], b_ref[...])
        xn_ref[...] = x
    else:
        x_ref, w_ref, bi_ref, q_ref, k_ref, v_ref, gb_ref, u_ref = refs
        x = x_ref[...]
    xb = x.astype(BF16)

    def proj(lo, hi):
        return jnp.dot(xb, w_ref[:, lo:hi], preferred_element_type=F32) + bi_ref[:, lo:hi]

    q_ref[...] = (proj(Q0, K0) * ATTN_SCALE).astype(BF16)
    k_ref[...] = proj(K0, V0)
    v_ref[...] = proj(V0, GB0)
    gb_ref[...] = proj(GB0, GC0)
    u_ref[...] = proj(GC0, H0) * proj(H0, IN_COLS)


def _inproj(x, ln, w_in_b, b_in):
    apply_ln = isinstance(x, tuple)
    row = lambda w: pl.BlockSpec((TM, w), lambda i: (i, 0))
    n_p = None
    if apply_ln:
        n_p, n_s = x[0].shape[0] // TM, x[1].shape[0] // TM
        t = x[0].shape[0] + x[1].shape[0]
        in_specs = _stream_specs(n_p, n_s, TM, D_MODEL) + [_full((1, D_MODEL)), _full((1, D_MODEL))]
        args = [x[0], x[1], ln[0], ln[1]]
    else:
        t = x.shape[0]
        in_specs = [row(D_MODEL)]
        args = [x]
    in_specs += [_full((D_MODEL, IN_COLS)), _full((1, IN_COLS))]
    args += [w_in_b, b_in]
    out_shape, out_specs = [], []
    if apply_ln:
        out_shape.append(jax.ShapeDtypeStruct((t, D_MODEL), F32))
        out_specs.append(row(D_MODEL))
    out_shape += [jax.ShapeDtypeStruct((t, ATTN_WIDTH), BF16),
                  jax.ShapeDtypeStruct((t, KV_WIDTH), F32),
                  jax.ShapeDtypeStruct((t, KV_WIDTH), F32),
                  jax.ShapeDtypeStruct((t, CONV_WIDTH), F32),
                  jax.ShapeDtypeStruct((t, CONV_WIDTH), F32)]
    out_specs += [row(ATTN_WIDTH), row(KV_WIDTH), row(KV_WIDTH), row(CONV_WIDTH), row(CONV_WIDTH)]
    outs = pl.pallas_call(
        functools.partial(_inproj_kernel, n_prompt=n_p),
        grid=(t // TM,), in_specs=in_specs, out_specs=out_specs, out_shape=out_shape,
        compiler_params=_cparams(("arbitrary",)), name="inproj")(*args)
    if apply_ln:
        return outs
    return [x] + list(outs)


def _attend_column(qcol, kexp_b, vexp_b, mask, sink_a, sink_b):
    lane = lax.broadcasted_iota(I32, qcol.shape, 1)
    outs = []
    for half, sink in ((0, sink_a), (1, sink_b)):
        keep = (lane < HEAD_DIM) if half == 0 else (lane >= HEAD_DIM)
        qm = jnp.where(keep, qcol, jnp.zeros_like(qcol))
        s = lax.dot_general(qm, kexp_b, (((1,), (1,)), ((), ())), preferred_element_type=F32)
        s = jnp.where(mask, s, -jnp.inf)
        m = jnp.maximum(jnp.max(s, -1, keepdims=True), sink)
        p = jnp.exp(s - m)
        denom = jnp.sum(p, -1, keepdims=True) + jnp.exp(sink - m)
        o = jnp.dot(p.astype(BF16), vexp_b, preferred_element_type=F32)
        outs.append(o * (1.0 / denom))
    return jnp.where(lane < HEAD_DIM, outs[0], outs[1])


def _dup_heads(x):
    lane = lax.broadcasted_iota(I32, x.shape, 1)
    xr = pltpu.roll(x, HEAD_DIM, 1)
    low = lane < HEAD_DIM
    return (jnp.where(low, x, xr).astype(BF16), jnp.where(low, xr, x).astype(BF16))


def _prompt_mixer_kernel(sinks_ref, q_ref, kc_ref, kp_ref, vc_ref, vp_ref, uc_ref, up_ref,
                         gb_ref, cw_ref, o_ref, nk_ref, nv_ref, nu_ref):
    first = pl.program_id(1) == 0

    @pl.when(pl.program_id(1) == pl.num_programs(1) - 1)
    def _():
        nk_ref[...] = kc_ref[TQ - WINDOW:, :]
        nv_ref[...] = vc_ref[TQ - WINDOW:, :]
        nu_ref[...] = uc_ref[TQ - SUBLANES:, :]

    r_i = lax.broadcasted_iota(I32, (WINDOW, 2 * WINDOW), 0)
    c_i = lax.broadcasted_iota(I32, (WINDOW, 2 * WINDOW), 1)
    band = (c_i <= WINDOW + r_i) & (c_i > r_i)
    band_first = band & (c_i >= WINDOW * first.astype(I32))
    for s in range(TQ // WINDOW):
        rows = slice(s * WINDOW, (s + 1) * WINDOW)
        if s == 0:
            kprev, vprev = kp_ref[...], vp_ref[...]
        else:
            prev = slice((s - 1) * WINDOW, s * WINDOW)
            kprev, vprev = kc_ref[prev, :], vc_ref[prev, :]
        kk = jnp.concatenate([kprev, kc_ref[rows, :]], 0)
        vv = jnp.concatenate([vprev, vc_ref[rows, :]], 0)
        kexp = _dup_heads(kk)
        vexp = _dup_heads(vv)
        mask = band_first if s == 0 else band
        for col in range(ATTN_WIDTH // LANES):
            h = col // 2
            cols = slice(col * LANES, (col + 1) * LANES)
            out = _attend_column(q_ref[rows, cols], kexp[h], vexp[h], mask,
                                 sinks_ref[2 * col], sinks_ref[2 * col + 1])
            o_ref[rows, cols] = out.astype(BF16)

    u = uc_ref[...]
    up = up_ref[...]
    zero = jnp.zeros((1, CONV_WIDTH), F32)
    p1 = jnp.where(first, zero, up[SUBLANES - 1:SUBLANES, :])
    p2 = jnp.where(first, zero, up[SUBLANES - 2:SUBLANES - 1, :])
    row = lax.broadcasted_iota(I32, u.shape, 0)
    u1 = jnp.where(row == 0, p1, pltpu.roll(u, 1, 0))
    u2 = jnp.where(row == 0, p2, jnp.where(row == 1, p1, pltpu.roll(u, 2, 0)))
    cw = cw_ref[...]
    y = u2 * cw[0:1, :] + u1 * cw[1:2, :] + u * cw[2:3, :]
    o_ref[:, ATTN_WIDTH:] = (gb_ref[...] * y).astype(BF16)


def _prompt_mixer(sinks, q, k, v, u, gb, conv_w, batch, seq):
    nj = seq // TQ
    tile = lambda b, j: b * nj + j
    cur = lambda w: pl.BlockSpec((TQ, w), lambda b, j: (tile(b, j), 0))
    prev_kv = pl.BlockSpec((WINDOW, KV_WIDTH),
                           lambda b, j: (jnp.maximum(tile(b, j) * (TQ // WINDOW) - 1, 0), 0))
    prev_u = pl.BlockSpec((SUBLANES, CONV_WIDTH),
                          lambda b, j: (jnp.maximum(tile(b, j) * (TQ // SUBLANES) - 1, 0), 0))
    return pl.pallas_call(
        _prompt_mixer_kernel,
        grid=(batch, nj),
        in_specs=[pl.BlockSpec(memory_space=pltpu.SMEM),
                  cur(ATTN_WIDTH), cur(KV_WIDTH), prev_kv, cur(KV_WIDTH), prev_kv,
                  cur(CONV_WIDTH), prev_u, cur(CONV_WIDTH), _full((CONV_K, CONV_WIDTH))],
        out_specs=[cur(D_MODEL),
                   pl.BlockSpec((None, WINDOW, KV_WIDTH), lambda b, j: (b, 0, 0)),
                   pl.BlockSpec((None, WINDOW, KV_WIDTH), lambda b, j: (b, 0, 0)),
                   pl.BlockSpec((None, SUBLANES, CONV_WIDTH), lambda b, j: (b, 0, 0))],
        out_shape=[jax.ShapeDtypeStruct((batch * seq, D_MODEL), BF16),
                   jax.ShapeDtypeStruct((batch, WINDOW, KV_WIDTH), F32),
                   jax.ShapeDtypeStruct((batch, WINDOW, KV_WIDTH), F32),
                   jax.ShapeDtypeStruct((batch, SUBLANES, CONV_WIDTH), F32)],
        compiler_params=_cparams(("arbitrary", "arbitrary")), name="prompt_mixer",
    )(sinks, q, k, k, v, v, u, u, gb, conv_w)


def _sample_mixer_kernel(sinks_ref, q_ref, kn_ref, vn_ref, kb_ref, vb_ref, u_ref, st_ref, gb_ref,
                         cw_ref, o_ref, nk_ref, nv_ref, *, dec_seq):
    g, w = kb_ref.shape[0], kb_ref.shape[1]
    rows = g * dec_seq
    n_cache = g * w
    n_keys = n_cache + 2 * rows
    kn, vn = kn_ref[...], vn_ref[...]
    pad = jnp.zeros((rows, KV_WIDTH), F32)
    kk = jnp.concatenate([kb_ref[...].reshape(n_cache, KV_WIDTH), kn, pad], 0)
    vv = jnp.concatenate([vb_ref[...].reshape(n_cache, KV_WIDTH), vn, pad], 0)
    kexp = _dup_heads(kk)
    vexp = _dup_heads(vv)

    r_i = lax.broadcasted_iota(I32, (rows, n_keys), 0)
    c_i = lax.broadcasted_iota(I32, (rows, n_keys), 1)
    r_seq, r_pos = _div_pow2(r_i, dec_seq), _mod_pow2(r_i, dec_seq)
    c_new = c_i - n_cache
    in_cache = ((c_i < n_cache) & (_div_pow2(c_i, w) == r_seq)
                & (_mod_pow2(c_i, w) > r_pos + (w - WINDOW)))
    in_new = ((c_new >= 0) & (c_new < rows) & (_div_pow2(c_new, dec_seq) == r_seq)
              & (_mod_pow2(c_new, dec_seq) <= r_pos))
    mask = in_cache | in_new
    for col in range(ATTN_WIDTH // LANES):
        h = col // 2
        cols = slice(col * LANES, (col + 1) * LANES)
        out = _attend_column(q_ref[:, cols], kexp[h], vexp[h], mask,
                             sinks_ref[2 * col], sinks_ref[2 * col + 1])
        o_ref[:, cols] = out.astype(BF16)

    nk_ref[:, 0:w - dec_seq, :] = kb_ref[:, dec_seq:w, :]
    nk_ref[:, w - dec_seq:w, :] = kn.reshape(g, dec_seq, KV_WIDTH)
    nv_ref[:, 0:w - dec_seq, :] = vb_ref[:, dec_seq:w, :]
    nv_ref[:, w - dec_seq:w, :] = vn.reshape(g, dec_seq, KV_WIDTH)

    u = u_ref[...]
    st = st_ref[...]
    pos = lax.broadcasted_iota(I32, u.shape, 0) % dec_seq
    u1 = jnp.where(pos == 0, pltpu.roll(st, rows - 1, 0), pltpu.roll(u, 1, 0))
    u2 = jnp.where(pos < 2, st, pltpu.roll(u, 2, 0))
    cw = cw_ref[...]
    y = u2 * cw[0:1, :] + u1 * cw[1:2, :] + u * cw[2:3, :]
    o_ref[:, ATTN_WIDTH:] = (gb_ref[...] * y).astype(BF16)


def _sample_mixer(sinks, q, k, v, u, gb, conv_w, cache_k, cache_v, state_rows, t_prompt):
    nseq, w = cache_k.shape[0], cache_k.shape[1]
    dec_seq = (q.shape[0] - t_prompt) // nseq
    rows = SEQ_GROUP * dec_seq
    off = t_prompt // rows
    tok = lambda wd: pl.BlockSpec((rows, wd), lambda i: (off + i, 0))
    local = lambda wd: pl.BlockSpec((rows, wd), lambda i: (i, 0))
    cache = pl.BlockSpec((SEQ_GROUP, w, KV_WIDTH), lambda i: (i, 0, 0))
    return pl.pallas_call(
        functools.partial(_sample_mixer_kernel, dec_seq=dec_seq),
        grid=(nseq // SEQ_GROUP,),
        in_specs=[pl.BlockSpec(memory_space=pltpu.SMEM),
                  tok(ATTN_WIDTH), tok(KV_WIDTH), tok(KV_WIDTH), cache, cache,
                  tok(CONV_WIDTH), local(CONV_WIDTH), tok(CONV_WIDTH),
                  _full((CONV_K, CONV_WIDTH))],
        out_specs=[local(D_MODEL), cache, cache],
        out_shape=[jax.ShapeDtypeStruct((nseq * dec_seq, D_MODEL), BF16),
                   jax.ShapeDtypeStruct(cache_k.shape, F32),
                   jax.ShapeDtypeStruct(cache_v.shape, F32)],
        compiler_params=_cparams(("parallel",)), name="sample_mixer",
    )(sinks, q, k, v, cache_k, cache_v, u, state_rows, gb, conv_w)


def _post_mixer_kernel(mixp_ref, mixs_ref, xn_ref, pp_ref, ps_ref, wo_ref, bo_ref, g1_ref, b1_ref,
                       wg_ref, bg_ref, wp_ref, wrh_ref, wrl_ref, br_ref,
                       x1_ref, c_ref, eid_ref, gate_ref, rank_ref, cnt_ref, carry_ref, *, n_prompt):
    @pl.when(pl.program_id(0) == 0)
    def _():
        carry_ref[...] = jnp.zeros_like(carry_ref)

    mixed = _stream_tile(n_prompt, mixp_ref, mixs_ref)
    mix = jnp.dot(mixed, wo_ref[...], preferred_element_type=F32) + bo_ref[...]
    x1 = _layer_norm(DN_ALPHA * xn_ref[...] + mix, g1_ref[...], b1_ref[...])
    for c in range(CHUNKS):
        x1_ref[pl.ds(c, x1.shape[0], stride=CHUNKS), :] = x1[:, c * LANES:(c + 1) * LANES]
    x1h = x1.astype(BF16)
    x1l = (x1 - x1h.astype(F32)).astype(BF16)
    gate = jax.nn.sigmoid(jnp.dot(x1h, wg_ref[...], preferred_element_type=F32) + bg_ref[...])
    p_tile = _stream_tile(n_prompt, pp_ref, ps_ref).astype(BF16)
    ple = jnp.dot(p_tile, wp_ref[...], preferred_element_type=F32)
    c_ref[...] = DN_ALPHA * x1 + gate * ple

    nt = (((1,), (1,)), ((), ()))
    logits = (lax.dot_general(wrh_ref[...], x1h, nt, preferred_element_type=F32)
              + lax.dot_general(wrh_ref[...], x1l, nt, preferred_element_type=F32)
              + lax.dot_general(wrl_ref[...], x1h, nt, preferred_element_type=F32)
              + br_ref[...])
    tm = logits.shape[1]
    e_i = lax.broadcasted_iota(I32, logits.shape, 0).astype(F32)
    work = logits
    vals, sels = [], []
    for k in range(TOP_K):
        m = jnp.max(work, 0, keepdims=True)
        idx = jnp.min(jnp.where(work == m, e_i, float(N_EXPERTS)), 0, keepdims=True)
        sel = e_i == idx
        vals.append(m)
        sels.append(sel)
        eid_ref[k:k + 1, :] = idx.astype(I32)
        work = jnp.where(sel, -jnp.inf, work)
    exps = [jnp.exp(v - vals[0]) for v in vals]
    denom = exps[0] + exps[1] + exps[2] + exps[3]
    for k in range(TOP_K):
        gate_ref[k:k + 1, :] = exps[k] / denom

    chosen = jnp.where(sels[0] | sels[1] | sels[2] | sels[3], 1.0, 0.0)
    s_i = lax.broadcasted_iota(I32, (tm, tm), 0)
    t_i = lax.broadcasted_iota(I32, (tm, tm), 1)
    before = jnp.where(s_i < t_i, 1.0, 0.0).astype(BF16)
    pos = carry_ref[:, 0:1] + jnp.dot(chosen.astype(BF16), before, preferred_element_type=F32)
    for k in range(TOP_K):
        rank_ref[k:k + 1, :] = jnp.sum(jnp.where(sels[k], pos, 0.0), 0, keepdims=True).astype(I32)
    carry = carry_ref[...] + jnp.sum(chosen, 1, keepdims=True)
    carry_ref[...] = carry
    cnt_ref[...] = carry


def _post_mixer(mix_p, mix_s, xn, layer, p_p, p_s, w_o, b_o, g1, b1, w_gate, b_gate, w_ple, wr_hi,
                wr_lo, b_r):
    t = xn.shape[0]
    n_p, n_s = mix_p.shape[0] // TM, mix_s.shape[0] // TM
    row = lambda w: pl.BlockSpec((TM, w), lambda i: (i, 0))
    meta = pl.BlockSpec((TOP_K, TM), lambda i: (0, i))
    vec = _full((1, D_MODEL))
    return pl.pallas_call(
        functools.partial(_post_mixer_kernel, n_prompt=n_p),
        grid=(t // TM,),
        in_specs=_stream_specs(n_p, n_s, TM, D_MODEL) + [row(D_MODEL)]
        + _stream_specs(n_p, n_s, TM, PLE_DIM, layer)
        + [_full((D_MODEL, D_MODEL)), vec, vec, vec,
           _full((D_MODEL, D_MODEL)), vec, _full((PLE_DIM, D_MODEL)),
           _full((N_EXPERTS, D_MODEL)), _full((N_EXPERTS, D_MODEL)), _full((N_EXPERTS, 1))],
        out_specs=[pl.BlockSpec((TM * CHUNKS, LANES), lambda i: (i, 0)), row(D_MODEL),
                   meta, meta, meta, _full((N_EXPERTS, LANES))],
        out_shape=[jax.ShapeDtypeStruct((t * CHUNKS, LANES), F32),
                   jax.ShapeDtypeStruct((t, D_MODEL), F32),
                   jax.ShapeDtypeStruct((TOP_K, t), I32),
                   jax.ShapeDtypeStruct((TOP_K, t), F32),
                   jax.ShapeDtypeStruct((TOP_K, t), I32),
                   jax.ShapeDtypeStruct((N_EXPERTS, LANES), F32)],
        scratch_shapes=[pltpu.VMEM((N_EXPERTS, LANES), F32)],
        compiler_params=_cparams(("arbitrary",)), name="post_mixer",
    )(mix_p, mix_s, xn, p_p, p_s, w_o, b_o, g1, b1, w_gate, b_gate, w_ple, wr_hi, wr_lo, b_r)


def _row_copy(src, dst, sem):
    return pltpu.make_async_copy(src, dst, sem)


CHUNKS = D_MODEL // LANES
assert CHUNKS == SUBLANES


def _load_token_tiles(ref, r0, n):
    return jnp.concatenate(
        [ref[pl.ds(r0 * CHUNKS + c, n, stride=CHUNKS), :] for c in range(CHUNKS)], axis=1)


def _token_tiles(x):
    if x.ndim == 2:
        return x.reshape(x.shape[0] // CHUNKS, CHUNKS, LANES)
    return x.reshape(x.shape[0] * CHUNKS, LANES)


def _for_each_row_slot(groups, fn):
    def group(j, carry):
        for s in range(SUBLANES):
            for k in range(TOP_K):
                fn(k, j, s, s * TOP_K + k)
        return carry

    lax.fori_loop(0, groups, group, 0)


def _dispatch_kernel(dest_ref, x_ref, xs_hbm, sem):
    tm = x_ref.shape[0]

    def scatter_row(k, j, s, n):
        r = j * SUBLANES + s
        _row_copy(x_ref.at[r], xs_hbm.at[dest_ref[0, k * tm + r]], sem).start(priority=n % 2)

    _for_each_row_slot(tm // SUBLANES, scatter_row)
    for k in range(TOP_K):
        _row_copy(x_ref, xs_hbm.at[pl.ds(0, tm)], sem).wait()


def _dest_tiles(dest, tm):
    t = dest.shape[1]
    return dest.reshape(TOP_K, t // tm, tm).transpose(1, 0, 2).reshape(t // tm, 1, TOP_K * tm)


def _dispatch(dest, x1t):
    t = x1t.shape[0] // CHUNKS
    tm = TM_DISPATCH
    xs = pl.pallas_call(
        _dispatch_kernel,
        grid=(t // tm,),
        in_specs=[pl.BlockSpec((None, 1, TOP_K * tm), lambda i: (i, 0, 0), memory_space=pltpu.SMEM),
                  pl.BlockSpec((tm, CHUNKS, LANES), lambda i: (i, 0, 0))],
        out_specs=pl.BlockSpec(memory_space=pl.ANY),
        out_shape=jax.ShapeDtypeStruct((t * TOP_K, CHUNKS, LANES), F32),
        scratch_shapes=[pltpu.SemaphoreType.DMA],
        compiler_params=_cparams(("arbitrary",)), name="dispatch",
    )(_dest_tiles(dest, tm), _token_tiles(x1t))
    return _token_tiles(xs)


def _moe_kernel(tile_ref, exp_ref, lo_ref, hi_ref, first_ref, newexp_ref,
                xs_ref, w1_ref, b1_ref, w2_ref, b2_ref, ys_ref, act_ref, w1b_ref, w2b_ref):
    del tile_ref, exp_ref
    w = pl.program_id(0)
    lo, hi = lo_ref[w], hi_ref[w]

    @pl.when(newexp_ref[w] == 1)
    def _():
        w1b_ref[...] = w1_ref[...].astype(BF16)
        r_i = lax.broadcasted_iota(I32, (LANES, LANES), 0)
        c_i = lax.broadcasted_iota(I32, (LANES, LANES), 1)
        src = (LANES // 2) * (r_i & 1) + lax.shift_right_logical(r_i, 1)
        perm = jnp.where(c_i == src, 1.0, 0.0).astype(BF16)
        for m in range(D_FF // LANES):
            blk = slice(m * LANES, (m + 1) * LANES)
            w2b_ref[blk, :] = jnp.dot(perm, w2_ref[blk, :].astype(BF16),
                                      preferred_element_type=F32).astype(BF16)

    def expert_mlp(r0, nrows, first_visit):
        rows = slice(r0, r0 + nrows)
        xb = _load_token_tiles(xs_ref, r0, nrows).astype(BF16)
        even = (lax.broadcasted_iota(I32, (nrows, LANES), 1) & 1) == 0
        for m in range(D_FF // LANES):
            cols = slice(2 * m * LANES, 2 * (m + 1) * LANES)
            h = jnp.dot(xb, w1b_ref[:, cols], preferred_element_type=F32) + b1_ref[:, cols]
            ha, hb = h[:, :LANES], h[:, LANES:]
            glu = jnp.where(even, ha, pltpu.roll(hb, 1, 1))
            lin = jnp.where(even, pltpu.roll(ha, LANES - 1, 1), hb)
            glu = jnp.minimum(glu, SWIGLU_LIMIT)
            lin = jnp.clip(lin, -SWIGLU_LIMIT, SWIGLU_LIMIT)
            act = glu * jax.nn.sigmoid(SWIGLU_ALPHA * glu) * (lin + 1.0)
            act_ref[rows, m * LANES:(m + 1) * LANES] = act.astype(BF16)
        r = r0 + lax.broadcasted_iota(I32, (nrows, LANES), 0)
        mine = (r >= lo) & (r < hi)
        if first_visit and nrows < TM_MOE:
            ys_ref[pl.ds((nrows - r0) * CHUNKS, (TM_MOE - nrows) * CHUNKS), :] = jnp.zeros(
                ((TM_MOE - nrows) * CHUNKS, LANES), F32)
        width = 2 * LANES
        for jb in range(D_MODEL // width):
            cols = slice(jb * width, (jb + 1) * width)
            y = jnp.dot(act_ref[rows, :], w2b_ref[:, cols], preferred_element_type=F32) + b2_ref[:, cols]
            for cc in range(width // LANES):
                at = pl.ds(r0 * CHUNKS + jb * (width // LANES) + cc, nrows, stride=CHUNKS)
                old = 0.0 if first_visit else ys_ref[at, :]
                ys_ref[at, :] = jnp.where(mine, y[:, cc * LANES:(cc + 1) * LANES], old)

    half = TM_MOE // 2
    some = hi > lo
    for first_visit in (True, False):
        visit = some & (first_ref[w] == (1 if first_visit else 0))

        @pl.when(visit & (hi <= half))
        def _():
            expert_mlp(0, half, first_visit)

        @pl.when(visit & (lo >= half))
        def _():
            expert_mlp(half, half, first_visit)

        @pl.when(visit & (lo < half) & (hi > half))
        def _():
            expert_mlp(0, TM_MOE, first_visit)


def _moe(sched, xs, layer, w1, b1, w2, b2):
    a = xs.shape[0] // CHUNKS
    n_items = sched[0].shape[0]
    by_expert = lambda r, c: pl.BlockSpec((None, None, r, c),
                                          lambda w, tile, ex, *_: (layer, ex[w], 0, 0))
    rows = pl.BlockSpec((TM_MOE * CHUNKS, LANES), lambda w, tile, *_: (tile[w], 0))
    return pl.pallas_call(
        _moe_kernel,
        grid_spec=pltpu.PrefetchScalarGridSpec(
            num_scalar_prefetch=6, grid=(n_items,),
            in_specs=[rows, by_expert(D_MODEL, 2 * D_FF), by_expert(1, 2 * D_FF),
                      by_expert(D_FF, D_MODEL), by_expert(1, D_MODEL)],
            out_specs=rows,
            scratch_shapes=[pltpu.VMEM((TM_MOE, D_FF), BF16),
                            pltpu.VMEM((D_MODEL, 2 * D_FF), BF16),
                            pltpu.VMEM((D_FF, D_MODEL), BF16)]),
        out_shape=jax.ShapeDtypeStruct((a * CHUNKS, LANES), F32),
        compiler_params=pltpu.CompilerParams(dimension_semantics=("arbitrary",),
                                             vmem_limit_bytes=VMEM_LIMIT_MOE),
        name="moe",
    )(*sched, xs, w1, b1, w2, b2)


def _moe_schedule(counts, n_rows):
    n_tiles = n_rows // TM_MOE
    n_items = n_tiles + N_EXPERTS - 1
    pend = jnp.cumsum(counts)
    pstart = pend - counts
    first_tile = pstart // TM_MOE
    last_tile = jnp.maximum(pend - 1, 0) // TM_MOE
    ntile = jnp.where(counts > 0, last_tile - first_tile + 1, 0)
    wend = jnp.cumsum(ntile)
    wstart = wend - ntile
    total = wend[-1]
    w = jnp.arange(n_items, dtype=I32)
    wv = jnp.minimum(w, total - 1)
    ex = jnp.minimum(jnp.sum(wend[None, :] <= wv[:, None], 1), N_EXPERTS - 1).astype(I32)
    is_ex = ex[:, None] == jnp.arange(N_EXPERTS, dtype=I32)[None, :]
    of_ex = lambda per_expert: jnp.sum(jnp.where(is_ex, per_expert[None, :], 0), 1)
    tile = (of_ex(first_tile) + wv - of_ex(wstart)).astype(I32)
    valid = w < total
    lo = jnp.where(valid, jnp.clip(of_ex(pstart) - tile * TM_MOE, 0, TM_MOE), 0).astype(I32)
    hi = jnp.where(valid, jnp.clip(of_ex(pend) - tile * TM_MOE, 0, TM_MOE), 0).astype(I32)
    prev_tile = jnp.concatenate([jnp.full((1,), -1, I32), tile[:-1]])
    first = (valid & (tile != prev_tile)).astype(I32)
    prev_ex = jnp.concatenate([jnp.full((1,), -1, I32), ex[:-1]])
    newexp = (valid & (ex != prev_ex)).astype(I32)
    return (tile, ex, lo, hi, first, newexp), pstart


def _combine_kernel(dcur_ref, dnext_ref, c_ref, gate_ref, g2_ref, b2_ref, ys_hbm, ys_flat_hbm, *rest,
                    n_prompt):
    *outs, buf, sem = rest
    tm = c_ref.shape[0]
    i = pl.program_id(0)

    def gather_tile(dest_ref, slot):
        def gather_row(k, j, s, n):
            r = j * SUBLANES + s
            dst = buf.at[slot, k, pl.ds(pl.multiple_of(r * CHUNKS, CHUNKS), CHUNKS)]
            _row_copy(ys_hbm.at[dest_ref[0, k * tm + r]], dst, sem.at[slot]).start(priority=n % 2)

        _for_each_row_slot(tm // SUBLANES, gather_row)

    @pl.when(i == 0)
    def _():
        gather_tile(dcur_ref, 0)

    @pl.when(i + 1 < pl.num_programs(0))
    def _():
        gather_tile(dnext_ref, (i + 1) % 2)

    slot = i % 2
    for k in range(TOP_K):
        _row_copy(ys_flat_hbm.at[pl.ds(0, tm * CHUNKS)], buf.at[slot, k], sem.at[slot]).wait()
    gates = gate_ref[...]
    parts = []
    for c in range(CHUNKS):
        part = c_ref[:, c * LANES:(c + 1) * LANES]
        for k in range(TOP_K):
            part = part + gates[:, k:k + 1] * buf[slot, k, pl.ds(c, tm, stride=CHUNKS), :]
        parts.append(part)
    acc = jnp.concatenate(parts, axis=1)
    y = _layer_norm(acc, g2_ref[...], b2_ref[...])
    if n_prompt is None:
        outs[0][...] = y
    else:
        @pl.when(i < n_prompt)
        def _():
            outs[0][...] = y

        @pl.when(i >= n_prompt)
        def _():
            outs[1][...] = y


def _combine(dest, c, gates_t, g2, b2, ys, t_prompt=None):
    t = c.shape[0]
    tm = TM_COMBINE
    n = t // tm
    vec = _full((1, D_MODEL))
    if t_prompt is None:
        n_p = None
        out_specs = pl.BlockSpec((tm, D_MODEL), lambda i: (i, 0))
        out_shape = jax.ShapeDtypeStruct((t, D_MODEL), F32)
    else:
        n_p = t_prompt // tm
        out_specs = _stream_specs(n_p, n - n_p, tm, D_MODEL)
        out_shape = [jax.ShapeDtypeStruct((t_prompt, D_MODEL), F32),
                     jax.ShapeDtypeStruct((t - t_prompt, D_MODEL), F32)]
    dest_spec = lambda f: pl.BlockSpec((None, 1, TOP_K * tm), lambda i: (f(i), 0, 0),
                                       memory_space=pltpu.SMEM)
    dest_tiles = _dest_tiles(dest, tm)
    return pl.pallas_call(
        functools.partial(_combine_kernel, n_prompt=n_p),
        grid=(n,),
        in_specs=[dest_spec(lambda i: i), dest_spec(lambda i: jnp.minimum(i + 1, n - 1)),
                  pl.BlockSpec((tm, D_MODEL), lambda i: (i, 0)),
                  pl.BlockSpec((tm, TOP_K), lambda i: (i, 0)),
                  vec, vec, pl.BlockSpec(memory_space=pl.ANY), pl.BlockSpec(memory_space=pl.ANY)],
        out_specs=out_specs,
        out_shape=out_shape,
        scratch_shapes=[pltpu.VMEM((2, TOP_K, tm * CHUNKS, LANES), F32),
                        pltpu.SemaphoreType.DMA((2,))],
        compiler_params=_cparams(("arbitrary",)), name="combine",
    )(dest_tiles, dest_tiles, c, gates_t, g2, b2, _token_tiles(ys), ys)


def kernel(x_prompt, x_sample, cache_k, cache_v, state_conv, p_prompt, p_sample, ln_emb_g, ln_emb_b,
           w_in, b_in, conv_w, sinks, w_o, b_o, ln1_g, ln1_b, w_gate, b_gate, w_ple, w_router,
           b_router, w1, b1, w2, b2, ln2_g, ln2_b):
    batch, seq, _ = x_prompt.shape
    nseq, dec_seq, _ = x_sample.shape
    t_p, t_s = batch * seq, nseq * dec_seq
    t = t_p + t_s
    w_cache = cache_k.shape[2]
    vec = lambda a: a.reshape(1, -1)

    x = (x_prompt.reshape(t_p, D_MODEL), x_sample.reshape(t_s, D_MODEL))
    state_rows = jnp.pad(state_conv, ((0, 0), (0, 0), (0, dec_seq - (CONV_K - 1)), (0, 0)))
    state_rows = state_rows.reshape(DEPTH, t_s, CONV_WIDTH)

    ks_p, vs_p, cs_p, ks_s, vs_s, cs_s = [], [], [], [], [], []
    for l in range(DEPTH):
        w_in_b = w_in[l].astype(BF16)
        xn, q, k, v, gb, u = _inproj(x, (vec(ln_emb_g), vec(ln_emb_b)), w_in_b, vec(b_in[l]))

        mix_p, nk_p, nv_p, nu_p = _prompt_mixer(sinks[l], q, k, v, u, gb, conv_w[l], batch, seq)
        ck = cache_k[l].reshape(nseq, w_cache, KV_WIDTH)
        cv = cache_v[l].reshape(nseq, w_cache, KV_WIDTH)
        mix_s, nk_s, nv_s = _sample_mixer(sinks[l], q, k, v, u, gb, conv_w[l], ck, cv,
                                          state_rows[l], t_p)

        wr_t = w_router[l].T
        wr_hi = wr_t.astype(BF16)
        wr_lo = (wr_t - wr_hi.astype(F32)).astype(BF16)
        x1, c, eid, gates, ranks, cnt = _post_mixer(
            mix_p, mix_s, xn, l, p_prompt.reshape(DEPTH, t_p, PLE_DIM),
            p_sample.reshape(DEPTH, t_s, PLE_DIM), w_o[l].astype(BF16), vec(b_o[l]), vec(ln1_g[l]), vec(ln1_b[l]),
            w_gate[l].astype(BF16), vec(b_gate[l]), w_ple[l].astype(BF16), wr_hi, wr_lo,
            b_router[l].reshape(N_EXPERTS, 1))

        counts = cnt[:, 0].astype(I32)
        sched, pstart = _moe_schedule(counts, t * TOP_K)
        onehot = eid[:, :, None] == jnp.arange(N_EXPERTS, dtype=I32)
        dest = ranks + jnp.sum(jnp.where(onehot, pstart, 0), -1)

        xs = _dispatch(dest, x1)
        ys = _moe(sched, xs, l, w1, b1[:, :, None, :], w2, b2[:, :, None, :])
        x = _combine(dest, c, gates.T, vec(ln2_g[l]), vec(ln2_b[l]), ys,
                     t_prompt=t_p if l == DEPTH - 1 else None)

        ks_p.append(nk_p.reshape(batch, WINDOW, N_KV_HEADS, HEAD_DIM))
        vs_p.append(nv_p.reshape(batch, WINDOW, N_KV_HEADS, HEAD_DIM))
        cs_p.append(nu_p[:, SUBLANES - (CONV_K - 1):])
        ks_s.append(nk_s.reshape(nseq, w_cache, N_KV_HEADS, HEAD_DIM))
        vs_s.append(nv_s.reshape(nseq, w_cache, N_KV_HEADS, HEAD_DIM))
        cs_s.append(u[t_p:].reshape(nseq, dec_seq, CONV_WIDTH)[:, dec_seq - (CONV_K - 1):])

    y_prompt = x[0].reshape(batch, seq, D_MODEL)
    y_sample = x[1].reshape(nseq, dec_seq, D_MODEL)
    return (y_prompt, y_sample, jnp.stack(ks_p), jnp.stack(vs_p), jnp.stack(cs_p),
            jnp.stack(ks_s), jnp.stack(vs_s), jnp.stack(cs_s))
```

```python
import functools

import jax
import jax.numpy as jnp
from jax import lax
from jax.experimental import pallas as pl
from jax.experimental.pallas import tpu as pltpu

F32 = jnp.float32
BF16 = jnp.bfloat16
I32 = jnp.int32

D_MODEL = 1024
DEPTH = 2
HEAD_DIM = 64
N_Q_HEADS = 8
N_KV_HEADS = 2
ATTN_WIDTH = N_Q_HEADS * HEAD_DIM
KV_WIDTH = N_KV_HEADS * HEAD_DIM
CONV_WIDTH = D_MODEL - ATTN_WIDTH
WINDOW = 128
ATTN_SCALE = HEAD_DIM ** -0.5
CONV_K = 3
N_EXPERTS = 32
TOP_K = 4
D_FF = D_MODEL
SWIGLU_LIMIT = 7.0
SWIGLU_ALPHA = 1.702
PLE_DIM = 256
LN_EPS = 1e-5
DN_ALPHA = (2.0 * DEPTH) ** 0.25
IN_COLS = ATTN_WIDTH + 2 * KV_WIDTH + 3 * CONV_WIDTH
Q0, K0, V0, GB0, GC0, H0 = 0, 512, 640, 768, 1280, 1792

LANES = 128
SUBLANES = 8
VMEM_LIMIT = 48 * 1024 * 1024
VMEM_LIMIT_MOE = 56 * 1024 * 1024

TM = 512
TQ = 512
SEQ_GROUP = 8
TM_DISPATCH = 512
TM_COMBINE = 256
TM_MOE = 512


def _layer_norm(x, g, b):
    mu = jnp.mean(x, -1, keepdims=True)
    xc = x - mu
    var = jnp.mean(xc * xc, -1, keepdims=True)
    return xc * lax.rsqrt(var + LN_EPS) * g + b


def _div_pow2(x, n):
    assert n & (n - 1) == 0
    return lax.shift_right_arithmetic(x, n.bit_length() - 1)


def _mod_pow2(x, n):
    assert n & (n - 1) == 0
    return x & (n - 1)


def _cparams(sem):
    return pltpu.CompilerParams(dimension_semantics=sem, vmem_limit_bytes=VMEM_LIMIT)


def _full(shape):
    return pl.BlockSpec(shape, lambda *_: (0,) * len(shape))


def _stream_specs(n_p, n_s, tm, width, layer=None):
    prompt_tile = lambda i: jnp.minimum(i, n_p - 1)
    sample_tile = lambda i: jnp.clip(i - n_p, 0, n_s - 1)
    if layer is None:
        return [pl.BlockSpec((tm, width), lambda i: (prompt_tile(i), 0)),
                pl.BlockSpec((tm, width), lambda i: (sample_tile(i), 0))]
    return [pl.BlockSpec((None, tm, width), lambda i: (layer, prompt_tile(i), 0)),
            pl.BlockSpec((None, tm, width), lambda i: (layer, sample_tile(i), 0))]


def _stream_tile(n_p, prompt_ref, sample_ref):
    return jnp.where(pl.program_id(0) < n_p, prompt_ref[...], sample_ref[...])


def _inproj_kernel(*refs, n_prompt):
    if n_prompt is not None:
        xp_ref, xs_ref, g_ref, b_ref, w_ref, bi_ref, xn_ref, q_ref, k_ref, v_ref, gb_ref, u_ref = refs
        x = _layer_norm(_stream_tile(n_prompt, xp_ref, xs_ref), g_ref[...], b_ref[...])
        xn_ref[...] = x
    else:
        x_ref, w_ref, bi_ref, q_ref, k_ref, v_ref, gb_ref, u_ref = refs
        x = x_ref[...]
    xb = x.astype(BF16)

    def proj(lo, hi):
        return jnp.dot(xb, w_ref[:, lo:hi], preferred_element_type=F32) + bi_ref[:, lo:hi]

    q_ref[...] = (proj(Q0, K0) * ATTN_SCALE).astype(BF16)
    k_ref[...] = proj(K0, V0)
    v_ref[...] = proj(V0, GB0)
    gb_ref[...] = proj(GB0, GC0)
    u_ref[...] = proj(GC0, H0) * proj(H0, IN_COLS)


def _inproj(x, ln, w_in_b, b_in):
    apply_ln = isinstance(x, tuple)
    row = lambda w: pl.BlockSpec((TM, w), lambda i: (i, 0))
    n_p = None
    if apply_ln:
        n_p, n_s = x[0].shape[0] // TM, x[1].shape[0] // TM
        t = x[0].shape[0] + x[1].shape[0]
        in_specs = _stream_specs(n_p, n_s, TM, D_MODEL) + [_full((1, D_MODEL)), _full((1, D_MODEL))]
        args = [x[0], x[1], ln[0], ln[1]]
    else:
        t = x.shape[0]
        in_specs = [row(D_MODEL)]
        args = [x]
    in_specs += [_full((D_MODEL, IN_COLS)), _full((1, IN_COLS))]
    args += [w_in_b, b_in]
    out_shape, out_specs = [], []
    if apply_ln:
        out_shape.append(jax.ShapeDtypeStruct((t, D_MODEL), F32))
        out_specs.append(row(D_MODEL))
    out_shape += [jax.ShapeDtypeStruct((t, ATTN_WIDTH), BF16),
                  jax.ShapeDtypeStruct((t, KV_WIDTH), F32),
                  jax.ShapeDtypeStruct((t, KV_WIDTH), F32),
                  jax.ShapeDtypeStruct((t, CONV_WIDTH), F32),
                  jax.ShapeDtypeStruct((t, CONV_WIDTH), F32)]
    out_specs += [row(ATTN_WIDTH), row(KV_WIDTH), row(KV_WIDTH), row(CONV_WIDTH), row(CONV_WIDTH)]
    outs = pl.pallas_call(
        functools.partial(_inproj_kernel, n_prompt=n_p),
        grid=(t // TM,), in_specs=in_specs, out_specs=out_specs, out_shape=out_shape,
        compiler_params=_cparams(("arbitrary",)), name="inproj")(*args)
    if apply_ln:
        return outs
    return [x] + list(outs)


def _attend_column(qcol, kexp_b, vexp_b, mask, sink_a, sink_b, fold=None):
    lane = lax.broadcasted_iota(I32, qcol.shape, 1)
    outs = []
    for half, sink in ((0, sink_a), (1, sink_b)):
        keep = (lane < HEAD_DIM) if half == 0 else (lane >= HEAD_DIM)
        qm = jnp.where(keep, qcol, jnp.zeros_like(qcol))
        s = lax.dot_general(qm, kexp_b, (((1,), (1,)), ((), ())), preferred_element_type=F32)
        if fold is None:
            s = jnp.where(mask, s, -jnp.inf)
        else:
            upper, prev_ok = fold
            s_prev, s_cur = s[:, :WINDOW], s[:, WINDOW:]
            if prev_ok is not None:
                s_prev = jnp.where(prev_ok, s_prev, -jnp.inf)
            s = jnp.where(upper, s_prev, s_cur)
        m = jnp.maximum(jnp.max(s, -1, keepdims=True), sink)
        p = jnp.exp(s - m)
        denom = jnp.sum(p, -1, keepdims=True) + jnp.exp(sink - m)
        if fold is not None:
            zero = jnp.zeros_like(p)
            p = jnp.concatenate([jnp.where(upper, p, zero), jnp.where(upper, zero, p)], axis=1)
        o = jnp.dot(p.astype(BF16), vexp_b, preferred_element_type=F32)
        outs.append(o * (1.0 / denom))
    return jnp.where(lane < HEAD_DIM, outs[0], outs[1])


def _dup_heads(x):
    lane = lax.broadcasted_iota(I32, x.shape, 1)
    xr = pltpu.roll(x, HEAD_DIM, 1)
    low = lane < HEAD_DIM
    return (jnp.where(low, x, xr).astype(BF16), jnp.where(low, xr, x).astype(BF16))


def _prompt_mixer_kernel(sinks_ref, q_ref, kc_ref, kp_ref, vc_ref, vp_ref, uc_ref, up_ref,
                         gb_ref, cw_ref, o_ref, nk_ref, nv_ref, nu_ref):
    first = pl.program_id(1) == 0

    @pl.when(pl.program_id(1) == pl.num_programs(1) - 1)
    def _():
        nk_ref[...] = kc_ref[TQ - WINDOW:, :]
        nv_ref[...] = vc_ref[TQ - WINDOW:, :]
        nu_ref[...] = uc_ref[TQ - SUBLANES:, :]

    r_i = lax.broadcasted_iota(I32, (WINDOW, WINDOW), 0)
    c_i = lax.broadcasted_iota(I32, (WINDOW, WINDOW), 1)
    upper = c_i > r_i
    has_prev = jnp.logical_not(first)
    for s in range(TQ // WINDOW):
        rows = slice(s * WINDOW, (s + 1) * WINDOW)
        if s == 0:
            kprev, vprev = kp_ref[...], vp_ref[...]
        else:
            prev = slice((s - 1) * WINDOW, s * WINDOW)
            kprev, vprev = kc_ref[prev, :], vc_ref[prev, :]
        kk = jnp.concatenate([kprev, kc_ref[rows, :]], 0)
        vv = jnp.concatenate([vprev, vc_ref[rows, :]], 0)
        kexp = _dup_heads(kk)
        vexp = _dup_heads(vv)
        fold = (upper, has_prev if s == 0 else None)
        for col in range(ATTN_WIDTH // LANES):
            h = col // 2
            cols = slice(col * LANES, (col + 1) * LANES)
            out = _attend_column(q_ref[rows, cols], kexp[h], vexp[h], None,
                                 sinks_ref[2 * col], sinks_ref[2 * col + 1], fold=fold)
            o_ref[rows, cols] = out.astype(BF16)

    u = uc_ref[...]
    up = up_ref[...]
    zero = jnp.zeros((1, CONV_WIDTH), F32)
    p1 = jnp.where(first, zero, up[SUBLANES - 1:SUBLANES, :])
    p2 = jnp.where(first, zero, up[SUBLANES - 2:SUBLANES - 1, :])
    row = lax.broadcasted_iota(I32, u.shape, 0)
    u1 = jnp.where(row == 0, p1, pltpu.roll(u, 1, 0))
    u2 = jnp.where(row == 0, p2, jnp.where(row == 1, p1, pltpu.roll(u, 2, 0)))
    cw = cw_ref[...]
    y = u2 * cw[0:1, :] + u1 * cw[1:2, :] + u * cw[2:3, :]
    o_ref[:, ATTN_WIDTH:] = (gb_ref[...] * y).astype(BF16)


def _prompt_mixer(sinks, q, k, v, u, gb, conv_w, batch, seq):
    nj = seq // TQ
    tile = lambda b, j: b * nj + j
    cur = lambda w: pl.BlockSpec((TQ, w), lambda b, j: (tile(b, j), 0))
    prev_kv = pl.BlockSpec((WINDOW, KV_WIDTH),
                           lambda b, j: (jnp.maximum(tile(b, j) * (TQ // WINDOW) - 1, 0), 0))
    prev_u = pl.BlockSpec((SUBLANES, CONV_WIDTH),
                          lambda b, j: (jnp.maximum(tile(b, j) * (TQ // SUBLANES) - 1, 0), 0))
    return pl.pallas_call(
        _prompt_mixer_kernel,
        grid=(batch, nj),
        in_specs=[pl.BlockSpec(memory_space=pltpu.SMEM),
                  cur(ATTN_WIDTH), cur(KV_WIDTH), prev_kv, cur(KV_WIDTH), prev_kv,
                  cur(CONV_WIDTH), prev_u, cur(CONV_WIDTH), _full((CONV_K, CONV_WIDTH))],
        out_specs=[cur(D_MODEL),
                   pl.BlockSpec((None, WINDOW, KV_WIDTH), lambda b, j: (b, 0, 0)),
                   pl.BlockSpec((None, WINDOW, KV_WIDTH), lambda b, j: (b, 0, 0)),
                   pl.BlockSpec((None, SUBLANES, CONV_WIDTH), lambda b, j: (b, 0, 0))],
        out_shape=[jax.ShapeDtypeStruct((batch * seq, D_MODEL), BF16),
                   jax.ShapeDtypeStruct((batch, WINDOW, KV_WIDTH), F32),
                   jax.ShapeDtypeStruct((batch, WINDOW, KV_WIDTH), F32),
                   jax.ShapeDtypeStruct((batch, SUBLANES, CONV_WIDTH), F32)],
        compiler_params=_cparams(("arbitrary", "arbitrary")), name="prompt_mixer",
    )(sinks, q, k, k, v, v, u, u, gb, conv_w)


def _sample_mixer_kernel(sinks_ref, q_ref, kn_ref, vn_ref, kb_ref, vb_ref, u_ref, st_ref, gb_ref,
                         cw_ref, o_ref, nk_ref, nv_ref, *, dec_seq):
    g, w = kb_ref.shape[0], kb_ref.shape[1]
    rows = g * dec_seq
    n_cache = g * w
    n_keys = n_cache + 2 * rows
    kn, vn = kn_ref[...], vn_ref[...]
    pad = jnp.zeros((rows, KV_WIDTH), F32)
    kk = jnp.concatenate([kb_ref[...].reshape(n_cache, KV_WIDTH), kn, pad], 0)
    vv = jnp.concatenate([vb_ref[...].reshape(n_cache, KV_WIDTH), vn, pad], 0)
    kexp = _dup_heads(kk)
    vexp = _dup_heads(vv)

    r_i = lax.broadcasted_iota(I32, (rows, n_keys), 0)
    c_i = lax.broadcasted_iota(I32, (rows, n_keys), 1)
    r_seq, r_pos = _div_pow2(r_i, dec_seq), _mod_pow2(r_i, dec_seq)
    c_new = c_i - n_cache
    in_cache = ((c_i < n_cache) & (_div_pow2(c_i, w) == r_seq)
                & (_mod_pow2(c_i, w) > r_pos + (w - WINDOW)))
    in_new = ((c_new >= 0) & (c_new < rows) & (_div_pow2(c_new, dec_seq) == r_seq)
              & (_mod_pow2(c_new, dec_seq) <= r_pos))
    mask = in_cache | in_new
    for col in range(ATTN_WIDTH // LANES):
        h = col // 2
        cols = slice(col * LANES, (col + 1) * LANES)
        out = _attend_column(q_ref[:, cols], kexp[h], vexp[h], mask,
                             sinks_ref[2 * col], sinks_ref[2 * col + 1])
        o_ref[:, cols] = out.astype(BF16)

    nk_ref[:, 0:w - dec_seq, :] = kb_ref[:, dec_seq:w, :]
    nk_ref[:, w - dec_seq:w, :] = kn.reshape(g, dec_seq, KV_WIDTH)
    nv_ref[:, 0:w - dec_seq, :] = vb_ref[:, dec_seq:w, :]
    nv_ref[:, w - dec_seq:w, :] = vn.reshape(g, dec_seq, KV_WIDTH)

    u = u_ref[...]
    st = st_ref[...]
    pos = lax.broadcasted_iota(I32, u.shape, 0) % dec_seq
    u1 = jnp.where(pos == 0, pltpu.roll(st, rows - 1, 0), pltpu.roll(u, 1, 0))
    u2 = jnp.where(pos < 2, st, pltpu.roll(u, 2, 0))
    cw = cw_ref[...]
    y = u2 * cw[0:1, :] + u1 * cw[1:2, :] + u * cw[2:3, :]
    o_ref[:, ATTN_WIDTH:] = (gb_ref[...] * y).astype(BF16)


def _sample_mixer(sinks, q, k, v, u, gb, conv_w, layer, cache_k, cache_v, state_rows, t_prompt):
    nseq, w = cache_k.shape[1], cache_k.shape[2]
    dec_seq = (q.shape[0] - t_prompt) // nseq
    rows = SEQ_GROUP * dec_seq
    off = t_prompt // rows
    tok = lambda wd: pl.BlockSpec((rows, wd), lambda i: (off + i, 0))
    local = lambda wd: pl.BlockSpec((rows, wd), lambda i: (i, 0))
    cache_in = pl.BlockSpec((None, SEQ_GROUP, w, KV_WIDTH), lambda i: (layer, i, 0, 0))
    cache = pl.BlockSpec((SEQ_GROUP, w, KV_WIDTH), lambda i: (i, 0, 0))
    return pl.pallas_call(
        functools.partial(_sample_mixer_kernel, dec_seq=dec_seq),
        grid=(nseq // SEQ_GROUP,),
        in_specs=[pl.BlockSpec(memory_space=pltpu.SMEM),
                  tok(ATTN_WIDTH), tok(KV_WIDTH), tok(KV_WIDTH), cache_in, cache_in,
                  tok(CONV_WIDTH), local(CONV_WIDTH), tok(CONV_WIDTH),
                  _full((CONV_K, CONV_WIDTH))],
        out_specs=[local(D_MODEL), cache, cache],
        out_shape=[jax.ShapeDtypeStruct((nseq * dec_seq, D_MODEL), BF16),
                   jax.ShapeDtypeStruct((nseq, w, KV_WIDTH), F32),
                   jax.ShapeDtypeStruct((nseq, w, KV_WIDTH), F32)],
        compiler_params=_cparams(("parallel",)), name="sample_mixer",
    )(sinks, q, k, v, cache_k, cache_v, u, state_rows, gb, conv_w)


def _post_mixer_kernel(mixp_ref, mixs_ref, xn_ref, pp_ref, ps_ref, wo_ref, bo_ref, g1_ref, b1_ref,
                       wg_ref, bg_ref, wp_ref, wrh_ref, wrl_ref, br_ref,
                       x1_ref, c_ref, eid_ref, gate_ref, rank_ref, cnt_ref, carry_ref, *, n_prompt):
    @pl.when(pl.program_id(0) == 0)
    def _():
        carry_ref[...] = jnp.zeros_like(carry_ref)

    mixed = _stream_tile(n_prompt, mixp_ref, mixs_ref)
    mix = jnp.dot(mixed, wo_ref[...], preferred_element_type=F32) + bo_ref[...]
    x1 = _layer_norm(DN_ALPHA * xn_ref[...] + mix, g1_ref[...], b1_ref[...])
    for c in range(CHUNKS):
        x1_ref[pl.ds(c, x1.shape[0], stride=CHUNKS), :] = x1[:, c * LANES:(c + 1) * LANES]
    x1h = x1.astype(BF16)
    x1l = (x1 - x1h.astype(F32)).astype(BF16)
    gate = jax.nn.sigmoid(jnp.dot(x1h, wg_ref[...], preferred_element_type=F32) + bg_ref[...])
    p_tile = _stream_tile(n_prompt, pp_ref, ps_ref).astype(BF16)
    ple = jnp.dot(p_tile, wp_ref[...], preferred_element_type=F32)
    c_ref[...] = DN_ALPHA * x1 + gate * ple

    nt = (((1,), (1,)), ((), ()))
    logits = (lax.dot_general(wrh_ref[...], x1h, nt, preferred_element_type=F32)
              + lax.dot_general(wrh_ref[...], x1l, nt, preferred_element_type=F32)
              + lax.dot_general(wrl_ref[...], x1h, nt, preferred_element_type=F32)
              + br_ref[...])
    tm = logits.shape[1]
    e_i = lax.broadcasted_iota(I32, logits.shape, 0).astype(F32)
    work = logits
    vals, sels = [], []
    for k in range(TOP_K):
        m = jnp.max(work, 0, keepdims=True)
        idx = jnp.min(jnp.where(work == m, e_i, float(N_EXPERTS)), 0, keepdims=True)
        sel = e_i == idx
        vals.append(m)
        sels.append(sel)
        eid_ref[k:k + 1, :] = idx.astype(I32)
        work = jnp.where(sel, -jnp.inf, work)
    exps = [jnp.exp(v - vals[0]) for v in vals]
    denom = exps[0] + exps[1] + exps[2] + exps[3]
    for k in range(TOP_K):
        gate_ref[k:k + 1, :] = exps[k] / denom

    chosen = jnp.where(sels[0] | sels[1] | sels[2] | sels[3], 1.0, 0.0)
    s_i = lax.broadcasted_iota(I32, (tm, tm), 0)
    t_i = lax.broadcasted_iota(I32, (tm, tm), 1)
    before = jnp.where(s_i < t_i, 1.0, 0.0).astype(BF16)
    pos = carry_ref[:, 0:1] + jnp.dot(chosen.astype(BF16), before, preferred_element_type=F32)
    for k in range(TOP_K):
        rank_ref[k:k + 1, :] = jnp.sum(jnp.where(sels[k], pos, 0.0), 0, keepdims=True).astype(I32)
    carry = carry_ref[...] + jnp.sum(chosen, 1, keepdims=True)
    carry_ref[...] = carry
    cnt_ref[...] = carry


def _post_mixer(mix_p, mix_s, xn, layer, p_p, p_s, w_o, b_o, g1, b1, w_gate, b_gate, w_ple, wr_hi,
                wr_lo, b_r):
    t = xn.shape[0]
    n_p, n_s = mix_p.shape[0] // TM, mix_s.shape[0] // TM
    row = lambda w: pl.BlockSpec((TM, w), lambda i: (i, 0))
    meta = pl.BlockSpec((TOP_K, TM), lambda i: (0, i))
    vec = _full((1, D_MODEL))
    return pl.pallas_call(
        functools.partial(_post_mixer_kernel, n_prompt=n_p),
        grid=(t // TM,),
        in_specs=_stream_specs(n_p, n_s, TM, D_MODEL) + [row(D_MODEL)]
        + _stream_specs(n_p, n_s, TM, PLE_DIM, layer)
        + [_full((D_MODEL, D_MODEL)), vec, vec, vec,
           _full((D_MODEL, D_MODEL)), vec, _full((PLE_DIM, D_MODEL)),
           _full((N_EXPERTS, D_MODEL)), _full((N_EXPERTS, D_MODEL)), _full((N_EXPERTS, 1))],
        out_specs=[pl.BlockSpec((TM * CHUNKS, LANES), lambda i: (i, 0)), row(D_MODEL),
                   meta, meta, meta, _full((N_EXPERTS, LANES))],
        out_shape=[jax.ShapeDtypeStruct((t * CHUNKS, LANES), F32),
                   jax.ShapeDtypeStruct((t, D_MODEL), F32),
                   jax.ShapeDtypeStruct((TOP_K, t), I32),
                   jax.ShapeDtypeStruct((TOP_K, t), F32),
                   jax.ShapeDtypeStruct((TOP_K, t), I32),
                   jax.ShapeDtypeStruct((N_EXPERTS, LANES), F32)],
        scratch_shapes=[pltpu.VMEM((N_EXPERTS, LANES), F32)],
        compiler_params=_cparams(("arbitrary",)), name="post_mixer",
    )(mix_p, mix_s, xn, p_p, p_s, w_o, b_o, g1, b1, w_gate, b_gate, w_ple, wr_hi, wr_lo, b_r)


def _row_copy(src, dst, sem):
    return pltpu.make_async_copy(src, dst, sem)


CHUNKS = D_MODEL // LANES
assert CHUNKS == SUBLANES


def _load_token_tiles(ref, r0, n):
    return jnp.concatenate(
        [ref[pl.ds(r0 * CHUNKS + c, n, stride=CHUNKS), :] for c in range(CHUNKS)], axis=1)


def _token_tiles(x):
    if x.ndim == 2:
        return x.reshape(x.shape[0] // CHUNKS, CHUNKS, LANES)
    return x.reshape(x.shape[0] * CHUNKS, LANES)


def _for_each_row_slot(groups, fn):
    def group(j, carry):
        for s in range(SUBLANES):
            for k in range(TOP_K):
                fn(k, j, s, s * TOP_K + k)
        return carry

    lax.fori_loop(0, groups, group, 0)


def _dispatch_kernel(dest_ref, x_ref, xs_hbm, sem):
    tm = x_ref.shape[0]

    def scatter_row(k, j, s, n):
        r = j * SUBLANES + s
        _row_copy(x_ref.at[r], xs_hbm.at[dest_ref[0, k * tm + r]], sem).start(priority=n % 2)

    _for_each_row_slot(tm // SUBLANES, scatter_row)
    for k in range(TOP_K):
        _row_copy(x_ref, xs_hbm.at[pl.ds(0, tm)], sem).wait()


def _dest_tiles(dest, tm):
    t = dest.shape[1]
    return dest.reshape(TOP_K, t // tm, tm).transpose(1, 0, 2).reshape(t // tm, 1, TOP_K * tm)


def _dispatch(dest, x1t):
    t = x1t.shape[0] // CHUNKS
    tm = TM_DISPATCH
    xs = pl.pallas_call(
        _dispatch_kernel,
        grid=(t // tm,),
        in_specs=[pl.BlockSpec((None, 1, TOP_K * tm), lambda i: (i, 0, 0), memory_space=pltpu.SMEM),
                  pl.BlockSpec((tm, CHUNKS, LANES), lambda i: (i, 0, 0))],
        out_specs=pl.BlockSpec(memory_space=pl.ANY),
        out_shape=jax.ShapeDtypeStruct((t * TOP_K, CHUNKS, LANES), F32),
        scratch_shapes=[pltpu.SemaphoreType.DMA],
        compiler_params=_cparams(("arbitrary",)), name="dispatch",
    )(_dest_tiles(dest, tm), _token_tiles(x1t))
    return _token_tiles(xs)


def _moe_kernel(tile_ref, exp_ref, lo_ref, hi_ref, first_ref, newexp_ref,
                xs_ref, w1_ref, b1_ref, w2_ref, b2_ref, ys_ref, act_ref, w1b_ref, w2b_ref):
    del tile_ref, exp_ref
    w = pl.program_id(0)
    lo, hi = lo_ref[w], hi_ref[w]

    @pl.when(newexp_ref[w] == 1)
    def _():
        w1b_ref[...] = w1_ref[...].astype(BF16)
        r_i = lax.broadcasted_iota(I32, (LANES, LANES), 0)
        c_i = lax.broadcasted_iota(I32, (LANES, LANES), 1)
        src = (LANES // 2) * (r_i & 1) + lax.shift_right_logical(r_i, 1)
        perm = jnp.where(c_i == src, 1.0, 0.0).astype(BF16)
        for m in range(D_FF // LANES):
            blk = slice(m * LANES, (m + 1) * LANES)
            w2b_ref[blk, :] = jnp.dot(perm, w2_ref[blk, :].astype(BF16),
                                      preferred_element_type=F32).astype(BF16)

    def expert_mlp(r0, nrows, first_visit):
        rows = slice(r0, r0 + nrows)
        xb = _load_token_tiles(xs_ref, r0, nrows).astype(BF16)
        even = (lax.broadcasted_iota(I32, (nrows, LANES), 1) & 1) == 0
        for m in range(D_FF // LANES):
            cols = slice(2 * m * LANES, 2 * (m + 1) * LANES)
            h = jnp.dot(xb, w1b_ref[:, cols], preferred_element_type=F32) + b1_ref[:, cols]
            ha, hb = h[:, :LANES], h[:, LANES:]
            glu = jnp.where(even, ha, pltpu.roll(hb, 1, 1))
            lin = jnp.where(even, pltpu.roll(ha, LANES - 1, 1), hb)
            glu = jnp.minimum(glu, SWIGLU_LIMIT)
            lin = jnp.clip(lin, -SWIGLU_LIMIT, SWIGLU_LIMIT)
            act = glu * jax.nn.sigmoid(SWIGLU_ALPHA * glu) * (lin + 1.0)
            act_ref[rows, m * LANES:(m + 1) * LANES] = act.astype(BF16)
        r = r0 + lax.broadcasted_iota(I32, (nrows, LANES), 0)
        mine = (r >= lo) & (r < hi)
        if first_visit and nrows < TM_MOE:
            ys_ref[pl.ds((nrows - r0) * CHUNKS, (TM_MOE - nrows) * CHUNKS), :] = jnp.zeros(
                ((TM_MOE - nrows) * CHUNKS, LANES), F32)
        width = 2 * LANES
        for jb in range(D_MODEL // width):
            cols = slice(jb * width, (jb + 1) * width)
            y = jnp.dot(act_ref[rows, :], w2b_ref[:, cols], preferred_element_type=F32) + b2_ref[:, cols]
            for cc in range(width // LANES):
                at = pl.ds(r0 * CHUNKS + jb * (width // LANES) + cc, nrows, stride=CHUNKS)
                old = 0.0 if first_visit else ys_ref[at, :]
                ys_ref[at, :] = jnp.where(mine, y[:, cc * LANES:(cc + 1) * LANES], old)

    half = TM_MOE // 2
    some = hi > lo
    for first_visit in (True, False):
        visit = some & (first_ref[w] == (1 if first_visit else 0))

        @pl.when(visit & (hi <= half))
        def _():
            expert_mlp(0, half, first_visit)

        @pl.when(visit & (lo >= half))
        def _():
            expert_mlp(half, half, first_visit)

        @pl.when(visit & (lo < half) & (hi > half))
        def _():
            expert_mlp(0, TM_MOE, first_visit)


def _moe(sched, xs, layer, w1, b1, w2, b2):
    a = xs.shape[0] // CHUNKS
    n_items = sched[0].shape[0]
    by_expert = lambda r, c: pl.BlockSpec((None, None, r, c),
                                          lambda w, tile, ex, *_: (layer, ex[w], 0, 0))
    rows = pl.BlockSpec((TM_MOE * CHUNKS, LANES), lambda w, tile, *_: (tile[w], 0))
    return pl.pallas_call(
        _moe_kernel,
        grid_spec=pltpu.PrefetchScalarGridSpec(
            num_scalar_prefetch=6, grid=(n_items,),
            in_specs=[rows, by_expert(D_MODEL, 2 * D_FF), by_expert(1, 2 * D_FF),
                      by_expert(D_FF, D_MODEL), by_expert(1, D_MODEL)],
            out_specs=rows,
            scratch_shapes=[pltpu.VMEM((TM_MOE, D_FF), BF16),
                            pltpu.VMEM((D_MODEL, 2 * D_FF), BF16),
                            pltpu.VMEM((D_FF, D_MODEL), BF16)]),
        out_shape=jax.ShapeDtypeStruct((a * CHUNKS, LANES), F32),
        compiler_params=pltpu.CompilerParams(dimension_semantics=("arbitrary",),
                                             vmem_limit_bytes=VMEM_LIMIT_MOE),
        name="moe",
    )(*sched, xs, w1, b1, w2, b2)


def _moe_schedule(counts, n_rows):
    n_tiles = n_rows // TM_MOE
    n_items = n_tiles + N_EXPERTS - 1
    pend = jnp.cumsum(counts)
    pstart = pend - counts
    first_tile = pstart // TM_MOE
    last_tile = jnp.maximum(pend - 1, 0) // TM_MOE
    ntile = jnp.where(counts > 0, last_tile - first_tile + 1, 0)
    wend = jnp.cumsum(ntile)
    wstart = wend - ntile
    total = wend[-1]
    w = jnp.arange(n_items, dtype=I32)
    wv = jnp.minimum(w, total - 1)
    ex = jnp.minimum(jnp.sum(wend[None, :] <= wv[:, None], 1), N_EXPERTS - 1).astype(I32)
    is_ex = ex[:, None] == jnp.arange(N_EXPERTS, dtype=I32)[None, :]
    of_ex = lambda per_expert: jnp.sum(jnp.where(is_ex, per_expert[None, :], 0), 1)
    tile = (of_ex(first_tile) + wv - of_ex(wstart)).astype(I32)
    valid = w < total
    lo = jnp.where(valid, jnp.clip(of_ex(pstart) - tile * TM_MOE, 0, TM_MOE), 0).astype(I32)
    hi = jnp.where(valid, jnp.clip(of_ex(pend) - tile * TM_MOE, 0, TM_MOE), 0).astype(I32)
    prev_tile = jnp.concatenate([jnp.full((1,), -1, I32), tile[:-1]])
    first = (valid & (tile != prev_tile)).astype(I32)
    prev_ex = jnp.concatenate([jnp.full((1,), -1, I32), ex[:-1]])
    newexp = (valid & (ex != prev_ex)).astype(I32)
    return (tile, ex, lo, hi, first, newexp), pstart


def _combine_kernel(dcur_ref, dnext_ref, c_ref, gate_ref, g2_ref, b2_ref, ys_hbm, ys_flat_hbm, *rest,
                    n_prompt):
    *outs, buf, sem = rest
    tm = c_ref.shape[0]
    i = pl.program_id(0)

    def gather_tile(dest_ref, slot):
        def gather_row(k, j, s, n):
            r = j * SUBLANES + s
            dst = buf.at[slot, k, pl.ds(pl.multiple_of(r * CHUNKS, CHUNKS), CHUNKS)]
            _row_copy(ys_hbm.at[dest_ref[0, k * tm + r]], dst, sem.at[slot]).start(priority=n % 2)

        _for_each_row_slot(tm // SUBLANES, gather_row)

    @pl.when(i == 0)
    def _():
        gather_tile(dcur_ref, 0)

    @pl.when(i + 1 < pl.num_programs(0))
    def _():
        gather_tile(dnext_ref, (i + 1) % 2)

    slot = i % 2
    for k in range(TOP_K):
        _row_copy(ys_flat_hbm.at[pl.ds(0, tm * CHUNKS)], buf.at[slot, k], sem.at[slot]).wait()
    gates = gate_ref[...]
    gate_cols = [jnp.broadcast_to(gates[:, k:k + 1], (tm, LANES)) for k in range(TOP_K)]
    parts = []
    for c in range(CHUNKS):
        part = c_ref[:, c * LANES:(c + 1) * LANES]
        for k in range(TOP_K):
            part = part + gate_cols[k] * buf[slot, k, pl.ds(c, tm, stride=CHUNKS), :]
        parts.append(part)
    acc = jnp.concatenate(parts, axis=1)
    y = _layer_norm(acc, g2_ref[...], b2_ref[...])
    if n_prompt is None:
        outs[0][...] = y
    else:
        @pl.when(i < n_prompt)
        def _():
            outs[0][...] = y

        @pl.when(i >= n_prompt)
        def _():
            outs[1][...] = y


def _combine(dest, c, gates_t, g2, b2, ys, t_prompt=None):
    t = c.shape[0]
    tm = TM_COMBINE
    n = t // tm
    vec = _full((1, D_MODEL))
    if t_prompt is None:
        n_p = None
        out_specs = pl.BlockSpec((tm, D_MODEL), lambda i: (i, 0))
        out_shape = jax.ShapeDtypeStruct((t, D_MODEL), F32)
    else:
        n_p = t_prompt // tm
        out_specs = _stream_specs(n_p, n - n_p, tm, D_MODEL)
        out_shape = [jax.ShapeDtypeStruct((t_prompt, D_MODEL), F32),
                     jax.ShapeDtypeStruct((t - t_prompt, D_MODEL), F32)]
    dest_spec = lambda f: pl.BlockSpec((None, 1, TOP_K * tm), lambda i: (f(i), 0, 0),
                                       memory_space=pltpu.SMEM)
    dest_tiles = _dest_tiles(dest, tm)
    return pl.pallas_call(
        functools.partial(_combine_kernel, n_prompt=n_p),
        grid=(n,),
        in_specs=[dest_spec(lambda i: i), dest_spec(lambda i: jnp.minimum(i + 1, n - 1)),
                  pl.BlockSpec((tm, D_MODEL), lambda i: (i, 0)),
                  pl.BlockSpec((tm, TOP_K), lambda i: (i, 0)),
                  vec, vec, pl.BlockSpec(memory_space=pl.ANY), pl.BlockSpec(memory_space=pl.ANY)],
        out_specs=out_specs,
        out_shape=out_shape,
        scratch_shapes=[pltpu.VMEM((2, TOP_K, tm * CHUNKS, LANES), F32),
                        pltpu.SemaphoreType.DMA((2,))],
        compiler_params=_cparams(("arbitrary",)), name="combine",
    )(dest_tiles, dest_tiles, c, gates_t, g2, b2, _token_tiles(ys), ys)


def kernel(x_prompt, x_sample, cache_k, cache_v, state_conv, p_prompt, p_sample, ln_emb_g, ln_emb_b,
           w_in, b_in, conv_w, sinks, w_o, b_o, ln1_g, ln1_b, w_gate, b_gate, w_ple, w_router,
           b_router, w1, b1, w2, b2, ln2_g, ln2_b):
    batch, seq, _ = x_prompt.shape
    nseq, dec_seq, _ = x_sample.shape
    t_p, t_s = batch * seq, nseq * dec_seq
    t = t_p + t_s
    w_cache = cache_k.shape[2]
    vec = lambda a: a.reshape(1, -1)

    x = (x_prompt.reshape(t_p, D_MODEL), x_sample.reshape(t_s, D_MODEL))
    state_rows = jnp.pad(state_conv, ((0, 0), (0, 0), (0, dec_seq - (CONV_K - 1)), (0, 0)))
    state_rows = state_rows.reshape(DEPTH, t_s, CONV_WIDTH)

    ks_p, vs_p, cs_p, ks_s, vs_s, cs_s = [], [], [], [], [], []
    for l in range(DEPTH):
        w_in_b = w_in[l].astype(BF16)
        xn, q, k, v, gb, u = _inproj(x, (vec(ln_emb_g), vec(ln_emb_b)), w_in_b, vec(b_in[l]))

        mix_p, nk_p, nv_p, nu_p = _prompt_mixer(sinks[l], q, k, v, u, gb, conv_w[l], batch, seq)
        mix_s, nk_s, nv_s = _sample_mixer(
            sinks[l], q, k, v, u, gb, conv_w[l], l,
            cache_k.reshape(DEPTH, nseq, w_cache, KV_WIDTH),
            cache_v.reshape(DEPTH, nseq, w_cache, KV_WIDTH), state_rows[l], t_p)

        wr_t = w_router[l].T
        wr_hi = wr_t.astype(BF16)
        wr_lo = (wr_t - wr_hi.astype(F32)).astype(BF16)
        x1, c, eid, gates, ranks, cnt = _post_mixer(
            mix_p, mix_s, xn, l, p_prompt.reshape(DEPTH, t_p, PLE_DIM),
            p_sample.reshape(DEPTH, t_s, PLE_DIM), w_o[l].astype(BF16), vec(b_o[l]), vec(ln1_g[l]), vec(ln1_b[l]),
            w_gate[l].astype(BF16), vec(b_gate[l]), w_ple[l].astype(BF16), wr_hi, wr_lo,
            b_router[l].reshape(N_EXPERTS, 1))

        counts = cnt[:, 0].astype(I32)
        sched, pstart = _moe_schedule(counts, t * TOP_K)
        onehot = eid[:, :, None] == jnp.arange(N_EXPERTS, dtype=I32)
        dest = ranks + jnp.sum(jnp.where(onehot, pstart, 0), -1)

        xs = _dispatch(dest, x1)
        ys = _moe(sched, xs, l, w1, b1[:, :, None, :], w2, b2[:, :, None, :])
        x = _combine(dest, c, gates.T, vec(ln2_g[l]), vec(ln2_b[l]), ys,
                     t_prompt=t_p if l == DEPTH - 1 else None)

        ks_p.append(nk_p.reshape(batch, WINDOW, N_KV_HEADS, HEAD_DIM))
        vs_p.append(nv_p.reshape(batch, WINDOW, N_KV_HEADS, HEAD_DIM))
        cs_p.append(nu_p[:, SUBLANES - (CONV_K - 1):])
        ks_s.append(nk_s.reshape(nseq, w_cache, N_KV_HEADS, HEAD_DIM))
        vs_s.append(nv_s.reshape(nseq, w_cache, N_KV_HEADS, HEAD_DIM))
        cs_s.append(u[t_p:].reshape(nseq, dec_seq, CONV_WIDTH)[:, dec_seq - (CONV_K - 1):])

    y_prompt = x[0].reshape(batch, seq, D_MODEL)
    y_sample = x[1].reshape(nseq, dec_seq, D_MODEL)
    return (y_prompt, y_sample, jnp.stack(ks_p), jnp.stack(vs_p), jnp.stack(cs_p),
            jnp.stack(ks_s), jnp.stack(vs_s), jnp.stack(cs_s))
```

```python
import functools

import jax
import jax.numpy as jnp
from jax import lax
from jax.experimental import pallas as pl
from jax.experimental.pallas import tpu as pltpu

F32 = jnp.float32
BF16 = jnp.bfloat16
I32 = jnp.int32

D_MODEL = 1024
DEPTH = 2
HEAD_DIM = 64
N_Q_HEADS = 8
N_KV_HEADS = 2
ATTN_WIDTH = N_Q_HEADS * HEAD_DIM
KV_WIDTH = N_KV_HEADS * HEAD_DIM
CONV_WIDTH = D_MODEL - ATTN_WIDTH
WINDOW = 128
ATTN_SCALE = HEAD_DIM ** -0.5
CONV_K = 3
N_EXPERTS = 32
TOP_K = 4
D_FF = D_MODEL
SWIGLU_LIMIT = 7.0
SWIGLU_ALPHA = 1.702
PLE_DIM = 256
LN_EPS = 1e-5
DN_ALPHA = (2.0 * DEPTH) ** 0.25
IN_COLS = ATTN_WIDTH + 2 * KV_WIDTH + 3 * CONV_WIDTH
Q0, K0, V0, GB0, GC0, H0 = 0, 512, 640, 768, 1280, 1792

LANES = 128
SUBLANES = 8
VMEM_LIMIT = 48 * 1024 * 1024
VMEM_LIMIT_MOE = 56 * 1024 * 1024

TM = 512
TQ = 512
SEQ_GROUP = 8
TM_DISPATCH = 1024
TM_COMBINE = 512
TM_MOE = 512


def _layer_norm(x, g, b):
    mu = jnp.mean(x, -1, keepdims=True)
    xc = x - mu
    var = jnp.mean(xc * xc, -1, keepdims=True)
    return xc * lax.rsqrt(var + LN_EPS) * g + b


def _div_pow2(x, n):
    assert n & (n - 1) == 0
    return lax.shift_right_arithmetic(x, n.bit_length() - 1)


def _mod_pow2(x, n):
    assert n & (n - 1) == 0
    return x & (n - 1)


def _cparams(sem):
    return pltpu.CompilerParams(dimension_semantics=sem, vmem_limit_bytes=VMEM_LIMIT)


def _full(shape):
    return pl.BlockSpec(shape, lambda *_: (0,) * len(shape))


def _stream_specs(n_p, n_s, tm, width, layer=None):
    prompt_tile = lambda i: jnp.minimum(i, n_p - 1)
    sample_tile = lambda i: jnp.clip(i - n_p, 0, n_s - 1)
    if layer is None:
        return [pl.BlockSpec((tm, width), lambda i: (prompt_tile(i), 0)),
                pl.BlockSpec((tm, width), lambda i: (sample_tile(i), 0))]
    return [pl.BlockSpec((None, tm, width), lambda i: (layer, prompt_tile(i), 0)),
            pl.BlockSpec((None, tm, width), lambda i: (layer, sample_tile(i), 0))]


def _stream_tile(n_p, prompt_ref, sample_ref):
    return jnp.where(pl.program_id(0) < n_p, prompt_ref[...], sample_ref[...])


def _inproj_kernel(*refs, n_prompt):
    if n_prompt is not None:
        xp_ref, xs_ref, g_ref, b_ref, w_ref, bi_ref, xn_ref, q_ref, k_ref, v_ref, gb_ref, u_ref = refs
        x = _layer_norm(_stream_tile(n_prompt, xp_ref, xs_ref), g_ref[...], b_ref[...])
        xn_ref[...] = x
    else:
        x_ref, w_ref, bi_ref, q_ref, k_ref, v_ref, gb_ref, u_ref = refs
        x = x_ref[...]
    xb = x.astype(BF16)

    def proj(lo, hi):
        return jnp.dot(xb, w_ref[:, lo:hi], preferred_element_type=F32) + bi_ref[:, lo:hi]

    q_ref[...] = (proj(Q0, K0) * ATTN_SCALE).astype(BF16)
    k_ref[...] = proj(K0, V0)
    v_ref[...] = proj(V0, GB0)
    gb_ref[...] = proj(GB0, GC0)
    u_ref[...] = proj(GC0, H0) * proj(H0, IN_COLS)


def _inproj(x, ln, w_in_b, b_in):
    apply_ln = isinstance(x, tuple)
    row = lambda w: pl.BlockSpec((TM, w), lambda i: (i, 0))
    n_p = None
    if apply_ln:
        n_p, n_s = x[0].shape[0] // TM, x[1].shape[0] // TM
        t = x[0].shape[0] + x[1].shape[0]
        in_specs = _stream_specs(n_p, n_s, TM, D_MODEL) + [_full((1, D_MODEL)), _full((1, D_MODEL))]
        args = [x[0], x[1], ln[0], ln[1]]
    else:
        t = x.shape[0]
        in_specs = [row(D_MODEL)]
        args = [x]
    in_specs += [_full((D_MODEL, IN_COLS)), _full((1, IN_COLS))]
    args += [w_in_b, b_in]
    out_shape, out_specs = [], []
    if apply_ln:
        out_shape.append(jax.ShapeDtypeStruct((t, D_MODEL), F32))
        out_specs.append(row(D_MODEL))
    out_shape += [jax.ShapeDtypeStruct((t, ATTN_WIDTH), BF16),
                  jax.ShapeDtypeStruct((t, KV_WIDTH), F32),
                  jax.ShapeDtypeStruct((t, KV_WIDTH), F32),
                  jax.ShapeDtypeStruct((t, CONV_WIDTH), F32),
                  jax.ShapeDtypeStruct((t, CONV_WIDTH), F32)]
    out_specs += [row(ATTN_WIDTH), row(KV_WIDTH), row(KV_WIDTH), row(CONV_WIDTH), row(CONV_WIDTH)]
    outs = pl.pallas_call(
        functools.partial(_inproj_kernel, n_prompt=n_p),
        grid=(t // TM,), in_specs=in_specs, out_specs=out_specs, out_shape=out_shape,
        compiler_params=_cparams(("arbitrary",)), name="inproj")(*args)
    if apply_ln:
        return outs
    return [x] + list(outs)


def _attend_column(qcol, kexp_b, vexp_b, mask, sink_a, sink_b, fold=None):
    lane = lax.broadcasted_iota(I32, qcol.shape, 1)
    outs = []
    for half, sink in ((0, sink_a), (1, sink_b)):
        keep = (lane < HEAD_DIM) if half == 0 else (lane >= HEAD_DIM)
        qm = jnp.where(keep, qcol, jnp.zeros_like(qcol))
        s = lax.dot_general(qm, kexp_b, (((1,), (1,)), ((), ())), preferred_element_type=F32)
        if fold is None:
            s = jnp.where(mask, s, -jnp.inf)
        else:
            upper, prev_ok = fold
            s_prev, s_cur = s[:, :WINDOW], s[:, WINDOW:]
            if prev_ok is not None:
                s_prev = jnp.where(prev_ok, s_prev, -jnp.inf)
            s = jnp.where(upper, s_prev, s_cur)
        m = jnp.maximum(jnp.max(s, -1, keepdims=True), sink)
        p = jnp.exp(s - m)
        denom = jnp.sum(p, -1, keepdims=True) + jnp.exp(sink - m)
        if fold is not None:
            zero = jnp.zeros_like(p)
            p = jnp.concatenate([jnp.where(upper, p, zero), jnp.where(upper, zero, p)], axis=1)
        o = jnp.dot(p.astype(BF16), vexp_b, preferred_element_type=F32)
        outs.append(o * (1.0 / denom))
    return jnp.where(lane < HEAD_DIM, outs[0], outs[1])


def _dup_heads(x):
    lane = lax.broadcasted_iota(I32, x.shape, 1)
    xr = pltpu.roll(x, HEAD_DIM, 1)
    low = lane < HEAD_DIM
    return (jnp.where(low, x, xr).astype(BF16), jnp.where(low, xr, x).astype(BF16))


def _prompt_mixer_kernel(sinks_ref, q_ref, kc_ref, kp_ref, vc_ref, vp_ref, uc_ref, up_ref,
                         gb_ref, cw_ref, o_ref, nk_ref, nv_ref, nu_ref):
    first = pl.program_id(1) == 0

    @pl.when(pl.program_id(1) == pl.num_programs(1) - 1)
    def _():
        nk_ref[...] = kc_ref[TQ - WINDOW:, :]
        nv_ref[...] = vc_ref[TQ - WINDOW:, :]
        nu_ref[...] = uc_ref[TQ - SUBLANES:, :]

    r_i = lax.broadcasted_iota(I32, (WINDOW, WINDOW), 0)
    c_i = lax.broadcasted_iota(I32, (WINDOW, WINDOW), 1)
    upper = c_i > r_i
    has_prev = jnp.logical_not(first)
    for s in range(TQ // WINDOW):
        rows = slice(s * WINDOW, (s + 1) * WINDOW)
        if s == 0:
            kprev, vprev = kp_ref[...], vp_ref[...]
        else:
            prev = slice((s - 1) * WINDOW, s * WINDOW)
            kprev, vprev = kc_ref[prev, :], vc_ref[prev, :]
        kk = jnp.concatenate([kprev, kc_ref[rows, :]], 0)
        vv = jnp.concatenate([vprev, vc_ref[rows, :]], 0)
        kexp = _dup_heads(kk)
        vexp = _dup_heads(vv)
        fold = (upper, has_prev if s == 0 else None)
        for col in range(ATTN_WIDTH // LANES):
            h = col // 2
            cols = slice(col * LANES, (col + 1) * LANES)
            out = _attend_column(q_ref[rows, cols], kexp[h], vexp[h], None,
                                 sinks_ref[2 * col], sinks_ref[2 * col + 1], fold=fold)
            o_ref[rows, cols] = out.astype(BF16)

    u = uc_ref[...]
    up = up_ref[...]
    zero = jnp.zeros((1, CONV_WIDTH), F32)
    p1 = jnp.where(first, zero, up[SUBLANES - 1:SUBLANES, :])
    p2 = jnp.where(first, zero, up[SUBLANES - 2:SUBLANES - 1, :])
    row = lax.broadcasted_iota(I32, u.shape, 0)
    u1 = jnp.where(row == 0, p1, pltpu.roll(u, 1, 0))
    u2 = jnp.where(row == 0, p2, jnp.where(row == 1, p1, pltpu.roll(u, 2, 0)))
    cw = cw_ref[...]
    y = u2 * cw[0:1, :] + u1 * cw[1:2, :] + u * cw[2:3, :]
    o_ref[:, ATTN_WIDTH:] = (gb_ref[...] * y).astype(BF16)


def _prompt_mixer(sinks, q, k, v, u, gb, conv_w, batch, seq):
    nj = seq // TQ
    tile = lambda b, j: b * nj + j
    cur = lambda w: pl.BlockSpec((TQ, w), lambda b, j: (tile(b, j), 0))
    prev_kv = pl.BlockSpec((WINDOW, KV_WIDTH),
                           lambda b, j: (jnp.maximum(tile(b, j) * (TQ // WINDOW) - 1, 0), 0))
    prev_u = pl.BlockSpec((SUBLANES, CONV_WIDTH),
                          lambda b, j: (jnp.maximum(tile(b, j) * (TQ // SUBLANES) - 1, 0), 0))
    return pl.pallas_call(
        _prompt_mixer_kernel,
        grid=(batch, nj),
        in_specs=[pl.BlockSpec(memory_space=pltpu.SMEM),
                  cur(ATTN_WIDTH), cur(KV_WIDTH), prev_kv, cur(KV_WIDTH), prev_kv,
                  cur(CONV_WIDTH), prev_u, cur(CONV_WIDTH), _full((CONV_K, CONV_WIDTH))],
        out_specs=[cur(D_MODEL),
                   pl.BlockSpec((None, WINDOW, KV_WIDTH), lambda b, j: (b, 0, 0)),
                   pl.BlockSpec((None, WINDOW, KV_WIDTH), lambda b, j: (b, 0, 0)),
                   pl.BlockSpec((None, SUBLANES, CONV_WIDTH), lambda b, j: (b, 0, 0))],
        out_shape=[jax.ShapeDtypeStruct((batch * seq, D_MODEL), BF16),
                   jax.ShapeDtypeStruct((batch, WINDOW, KV_WIDTH), F32),
                   jax.ShapeDtypeStruct((batch, WINDOW, KV_WIDTH), F32),
                   jax.ShapeDtypeStruct((batch, SUBLANES, CONV_WIDTH), F32)],
        compiler_params=_cparams(("arbitrary", "arbitrary")), name="prompt_mixer",
    )(sinks, q, k, k, v, v, u, u, gb, conv_w)


def _sample_mixer_kernel(sinks_ref, q_ref, kn_ref, vn_ref, kb_ref, vb_ref, u_ref, st_ref, gb_ref,
                         cw_ref, o_ref, nk_ref, nv_ref, *, dec_seq):
    g, w = kb_ref.shape[0], kb_ref.shape[1]
    rows = g * dec_seq
    n_cache = g * w
    n_keys = n_cache + 2 * rows
    kn, vn = kn_ref[...], vn_ref[...]
    pad = jnp.zeros((rows, KV_WIDTH), F32)
    kk = jnp.concatenate([kb_ref[...].reshape(n_cache, KV_WIDTH), kn, pad], 0)
    vv = jnp.concatenate([vb_ref[...].reshape(n_cache, KV_WIDTH), vn, pad], 0)
    kexp = _dup_heads(kk)
    vexp = _dup_heads(vv)

    r_i = lax.broadcasted_iota(I32, (rows, n_keys), 0)
    c_i = lax.broadcasted_iota(I32, (rows, n_keys), 1)
    r_seq, r_pos = _div_pow2(r_i, dec_seq), _mod_pow2(r_i, dec_seq)
    c_new = c_i - n_cache
    in_cache = ((c_i < n_cache) & (_div_pow2(c_i, w) == r_seq)
                & (_mod_pow2(c_i, w) > r_pos + (w - WINDOW)))
    in_new = ((c_new >= 0) & (c_new < rows) & (_div_pow2(c_new, dec_seq) == r_seq)
              & (_mod_pow2(c_new, dec_seq) <= r_pos))
    mask = in_cache | in_new
    for col in range(ATTN_WIDTH // LANES):
        h = col // 2
        cols = slice(col * LANES, (col + 1) * LANES)
        out = _attend_column(q_ref[:, cols], kexp[h], vexp[h], mask,
                             sinks_ref[2 * col], sinks_ref[2 * col + 1])
        o_ref[:, cols] = out.astype(BF16)

    nk_ref[:, 0:w - dec_seq, :] = kb_ref[:, dec_seq:w, :]
    nk_ref[:, w - dec_seq:w, :] = kn.reshape(g, dec_seq, KV_WIDTH)
    nv_ref[:, 0:w - dec_seq, :] = vb_ref[:, dec_seq:w, :]
    nv_ref[:, w - dec_seq:w, :] = vn.reshape(g, dec_seq, KV_WIDTH)

    u = u_ref[...]
    st = st_ref[...]
    pos = lax.broadcasted_iota(I32, u.shape, 0) % dec_seq
    u1 = jnp.where(pos == 0, pltpu.roll(st, rows - 1, 0), pltpu.roll(u, 1, 0))
    u2 = jnp.where(pos < 2, st, pltpu.roll(u, 2, 0))
    cw = cw_ref[...]
    y = u2 * cw[0:1, :] + u1 * cw[1:2, :] + u * cw[2:3, :]
    o_ref[:, ATTN_WIDTH:] = (gb_ref[...] * y).astype(BF16)


def _sample_mixer(sinks, q, k, v, u, gb, conv_w, layer, cache_k, cache_v, state_rows, t_prompt):
    nseq, w = cache_k.shape[1], cache_k.shape[2]
    dec_seq = (q.shape[0] - t_prompt) // nseq
    rows = SEQ_GROUP * dec_seq
    off = t_prompt // rows
    tok = lambda wd: pl.BlockSpec((rows, wd), lambda i: (off + i, 0))
    local = lambda wd: pl.BlockSpec((rows, wd), lambda i: (i, 0))
    cache_in = pl.BlockSpec((None, SEQ_GROUP, w, KV_WIDTH), lambda i: (layer, i, 0, 0))
    cache = pl.BlockSpec((SEQ_GROUP, w, KV_WIDTH), lambda i: (i, 0, 0))
    return pl.pallas_call(
        functools.partial(_sample_mixer_kernel, dec_seq=dec_seq),
        grid=(nseq // SEQ_GROUP,),
        in_specs=[pl.BlockSpec(memory_space=pltpu.SMEM),
                  tok(ATTN_WIDTH), tok(KV_WIDTH), tok(KV_WIDTH), cache_in, cache_in,
                  tok(CONV_WIDTH), local(CONV_WIDTH), tok(CONV_WIDTH),
                  _full((CONV_K, CONV_WIDTH))],
        out_specs=[local(D_MODEL), cache, cache],
        out_shape=[jax.ShapeDtypeStruct((nseq * dec_seq, D_MODEL), BF16),
                   jax.ShapeDtypeStruct((nseq, w, KV_WIDTH), F32),
                   jax.ShapeDtypeStruct((nseq, w, KV_WIDTH), F32)],
        compiler_params=_cparams(("parallel",)), name="sample_mixer",
    )(sinks, q, k, v, cache_k, cache_v, u, state_rows, gb, conv_w)


def _post_mixer_kernel(mixp_ref, mixs_ref, xn_ref, pp_ref, ps_ref, wo_ref, bo_ref, g1_ref, b1_ref,
                       wg_ref, bg_ref, wp_ref, wrh_ref, wrl_ref, br_ref,
                       x1_ref, c_ref, eid_ref, gate_ref, rank_ref, cnt_ref, carry_ref, *, n_prompt):
    @pl.when(pl.program_id(0) == 0)
    def _():
        carry_ref[...] = jnp.zeros_like(carry_ref)

    mixed = _stream_tile(n_prompt, mixp_ref, mixs_ref)
    mix = jnp.dot(mixed, wo_ref[...], preferred_element_type=F32) + bo_ref[...]
    x1 = _layer_norm(DN_ALPHA * xn_ref[...] + mix, g1_ref[...], b1_ref[...])
    for c in range(CHUNKS):
        x1_ref[pl.ds(c, x1.shape[0], stride=CHUNKS), :] = x1[:, c * LANES:(c + 1) * LANES]
    x1h = x1.astype(BF16)
    x1l = (x1 - x1h.astype(F32)).astype(BF16)
    gate = jax.nn.sigmoid(jnp.dot(x1h, wg_ref[...], preferred_element_type=F32) + bg_ref[...])
    p_tile = _stream_tile(n_prompt, pp_ref, ps_ref).astype(BF16)
    ple = jnp.dot(p_tile, wp_ref[...], preferred_element_type=F32)
    c_ref[...] = DN_ALPHA * x1 + gate * ple

    nt = (((1,), (1,)), ((), ()))
    logits = (lax.dot_general(wrh_ref[...], x1h, nt, preferred_element_type=F32)
              + lax.dot_general(wrh_ref[...], x1l, nt, preferred_element_type=F32)
              + lax.dot_general(wrl_ref[...], x1h, nt, preferred_element_type=F32)
              + br_ref[...])
    tm = logits.shape[1]
    e_i = lax.broadcasted_iota(I32, logits.shape, 0).astype(F32)
    work = logits
    vals, sels = [], []
    for k in range(TOP_K):
        m = jnp.max(work, 0, keepdims=True)
        idx = jnp.min(jnp.where(work == m, e_i, float(N_EXPERTS)), 0, keepdims=True)
        sel = e_i == idx
        vals.append(m)
        sels.append(sel)
        eid_ref[k:k + 1, :] = idx.astype(I32)
        work = jnp.where(sel, -jnp.inf, work)
    exps = [jnp.exp(v - vals[0]) for v in vals]
    denom = exps[0] + exps[1] + exps[2] + exps[3]
    for k in range(TOP_K):
        gate_ref[k:k + 1, :] = exps[k] / denom

    chosen = jnp.where(sels[0] | sels[1] | sels[2] | sels[3], 1.0, 0.0)
    s_i = lax.broadcasted_iota(I32, (tm, tm), 0)
    t_i = lax.broadcasted_iota(I32, (tm, tm), 1)
    before = jnp.where(s_i < t_i, 1.0, 0.0).astype(BF16)
    pos = carry_ref[:, 0:1] + jnp.dot(chosen.astype(BF16), before, preferred_element_type=F32)
    for k in range(TOP_K):
        rank_ref[k:k + 1, :] = jnp.sum(jnp.where(sels[k], pos, 0.0), 0, keepdims=True).astype(I32)
    carry = carry_ref[...] + jnp.sum(chosen, 1, keepdims=True)
    carry_ref[...] = carry
    cnt_ref[...] = carry


def _post_mixer(mix_p, mix_s, xn, layer, p_p, p_s, w_o, b_o, g1, b1, w_gate, b_gate, w_ple, wr_hi,
                wr_lo, b_r):
    t = xn.shape[0]
    n_p, n_s = mix_p.shape[0] // TM, mix_s.shape[0] // TM
    row = lambda w: pl.BlockSpec((TM, w), lambda i: (i, 0))
    meta = pl.BlockSpec((TOP_K, TM), lambda i: (0, i))
    vec = _full((1, D_MODEL))
    return pl.pallas_call(
        functools.partial(_post_mixer_kernel, n_prompt=n_p),
        grid=(t // TM,),
        in_specs=_stream_specs(n_p, n_s, TM, D_MODEL) + [row(D_MODEL)]
        + _stream_specs(n_p, n_s, TM, PLE_DIM, layer)
        + [_full((D_MODEL, D_MODEL)), vec, vec, vec,
           _full((D_MODEL, D_MODEL)), vec, _full((PLE_DIM, D_MODEL)),
           _full((N_EXPERTS, D_MODEL)), _full((N_EXPERTS, D_MODEL)), _full((N_EXPERTS, 1))],
        out_specs=[pl.BlockSpec((TM * CHUNKS, LANES), lambda i: (i, 0)), row(D_MODEL),
                   meta, meta, meta, _full((N_EXPERTS, LANES))],
        out_shape=[jax.ShapeDtypeStruct((t * CHUNKS, LANES), F32),
                   jax.ShapeDtypeStruct((t, D_MODEL), F32),
                   jax.ShapeDtypeStruct((TOP_K, t), I32),
                   jax.ShapeDtypeStruct((TOP_K, t), F32),
                   jax.ShapeDtypeStruct((TOP_K, t), I32),
                   jax.ShapeDtypeStruct((N_EXPERTS, LANES), F32)],
        scratch_shapes=[pltpu.VMEM((N_EXPERTS, LANES), F32)],
        compiler_params=_cparams(("arbitrary",)), name="post_mixer",
    )(mix_p, mix_s, xn, p_p, p_s, w_o, b_o, g1, b1, w_gate, b_gate, w_ple, wr_hi, wr_lo, b_r)


def _row_copy(src, dst, sem):
    return pltpu.make_async_copy(src, dst, sem)


CHUNKS = D_MODEL // LANES
assert CHUNKS == SUBLANES


def _load_token_tiles(ref, r0, n):
    return jnp.concatenate(
        [ref[pl.ds(r0 * CHUNKS + c, n, stride=CHUNKS), :] for c in range(CHUNKS)], axis=1)


def _token_tiles(x):
    if x.ndim == 2:
        return x.reshape(x.shape[0] // CHUNKS, CHUNKS, LANES)
    return x.reshape(x.shape[0] * CHUNKS, LANES)


def _for_each_row_slot(groups, fn):
    def group(j, carry):
        for s in range(SUBLANES):
            for k in range(TOP_K):
                fn(k, j, s, s * TOP_K + k)
        return carry

    lax.fori_loop(0, groups, group, 0)


def _dispatch_kernel(dest_ref, x_ref, xs_hbm, sem):
    tm = x_ref.shape[0]

    def scatter_row(k, j, s, n):
        r = j * SUBLANES + s
        _row_copy(x_ref.at[r], xs_hbm.at[dest_ref[0, k * tm + r]], sem).start(priority=n % 2)

    _for_each_row_slot(tm // SUBLANES, scatter_row)
    for k in range(TOP_K):
        _row_copy(x_ref, xs_hbm.at[pl.ds(0, tm)], sem).wait()


def _dest_tiles(dest, tm):
    t = dest.shape[1]
    return dest.reshape(TOP_K, t // tm, tm).transpose(1, 0, 2).reshape(t // tm, 1, TOP_K * tm)


def _dispatch(dest, x1t):
    t = x1t.shape[0] // CHUNKS
    tm = TM_DISPATCH
    xs = pl.pallas_call(
        _dispatch_kernel,
        grid=(t // tm,),
        in_specs=[pl.BlockSpec((None, 1, TOP_K * tm), lambda i: (i, 0, 0), memory_space=pltpu.SMEM),
                  pl.BlockSpec((tm, CHUNKS, LANES), lambda i: (i, 0, 0))],
        out_specs=pl.BlockSpec(memory_space=pl.ANY),
        out_shape=jax.ShapeDtypeStruct((t * TOP_K, CHUNKS, LANES), F32),
        scratch_shapes=[pltpu.SemaphoreType.DMA],
        compiler_params=_cparams(("arbitrary",)), name="dispatch",
    )(_dest_tiles(dest, tm), _token_tiles(x1t))
    return _token_tiles(xs)


def _moe_kernel(tile_ref, exp_ref, lo_ref, hi_ref, first_ref, newexp_ref,
                xs_ref, w1_ref, b1_ref, w2_ref, b2_ref, ys_ref, act_ref, w1b_ref, w2b_ref):
    del tile_ref, exp_ref
    w = pl.program_id(0)
    lo, hi = lo_ref[w], hi_ref[w]

    @pl.when(newexp_ref[w] == 1)
    def _():
        w1b_ref[...] = w1_ref[...].astype(BF16)
        r_i = lax.broadcasted_iota(I32, (LANES, LANES), 0)
        c_i = lax.broadcasted_iota(I32, (LANES, LANES), 1)
        src = (LANES // 2) * (r_i & 1) + lax.shift_right_logical(r_i, 1)
        perm = jnp.where(c_i == src, 1.0, 0.0).astype(BF16)
        for m in range(D_FF // LANES):
            blk = slice(m * LANES, (m + 1) * LANES)
            w2b_ref[blk, :] = jnp.dot(perm, w2_ref[blk, :].astype(BF16),
                                      preferred_element_type=F32).astype(BF16)

    def expert_mlp(r0, nrows, first_visit):
        rows = slice(r0, r0 + nrows)
        xb = _load_token_tiles(xs_ref, r0, nrows).astype(BF16)
        even = (lax.broadcasted_iota(I32, (nrows, LANES), 1) & 1) == 0
        for m in range(D_FF // LANES):
            cols = slice(2 * m * LANES, 2 * (m + 1) * LANES)
            h = jnp.dot(xb, w1b_ref[:, cols], preferred_element_type=F32) + b1_ref[:, cols]
            ha, hb = h[:, :LANES], h[:, LANES:]
            glu = jnp.where(even, ha, pltpu.roll(hb, 1, 1))
            lin = jnp.where(even, pltpu.roll(ha, LANES - 1, 1), hb)
            glu = jnp.minimum(glu, SWIGLU_LIMIT)
            lin = jnp.clip(lin, -SWIGLU_LIMIT, SWIGLU_LIMIT)
            act = glu * jax.nn.sigmoid(SWIGLU_ALPHA * glu) * (lin + 1.0)
            act_ref[rows, m * LANES:(m + 1) * LANES] = act.astype(BF16)
        r = r0 + lax.broadcasted_iota(I32, (nrows, LANES), 0)
        mine = (r >= lo) & (r < hi)
        if first_visit and nrows < TM_MOE:
            ys_ref[pl.ds((nrows - r0) * CHUNKS, (TM_MOE - nrows) * CHUNKS), :] = jnp.zeros(
                ((TM_MOE - nrows) * CHUNKS, LANES), F32)
        width = 2 * LANES
        for jb in range(D_MODEL // width):
            cols = slice(jb * width, (jb + 1) * width)
            y = jnp.dot(act_ref[rows, :], w2b_ref[:, cols], preferred_element_type=F32) + b2_ref[:, cols]
            for cc in range(width // LANES):
                at = pl.ds(r0 * CHUNKS + jb * (width // LANES) + cc, nrows, stride=CHUNKS)
                old = 0.0 if first_visit else ys_ref[at, :]
                ys_ref[at, :] = jnp.where(mine, y[:, cc * LANES:(cc + 1) * LANES], old)

    half = TM_MOE // 2
    some = hi > lo
    for first_visit in (True, False):
        visit = some & (first_ref[w] == (1 if first_visit else 0))

        @pl.when(visit & (hi <= half))
        def _():
            expert_mlp(0, half, first_visit)

        @pl.when(visit & (lo >= half))
        def _():
            expert_mlp(half, half, first_visit)

        @pl.when(visit & (lo < half) & (hi > half))
        def _():
            expert_mlp(0, TM_MOE, first_visit)


def _moe(sched, xs, layer, w1, b1, w2, b2):
    a = xs.shape[0] // CHUNKS
    n_items = sched[0].shape[0]
    by_expert = lambda r, c: pl.BlockSpec((None, None, r, c),
                                          lambda w, tile, ex, *_: (layer, ex[w], 0, 0))
    rows = pl.BlockSpec((TM_MOE * CHUNKS, LANES), lambda w, tile, *_: (tile[w], 0))
    return pl.pallas_call(
        _moe_kernel,
        grid_spec=pltpu.PrefetchScalarGridSpec(
            num_scalar_prefetch=6, grid=(n_items,),
            in_specs=[rows, by_expert(D_MODEL, 2 * D_FF), by_expert(1, 2 * D_FF),
                      by_expert(D_FF, D_MODEL), by_expert(1, D_MODEL)],
            out_specs=rows,
            scratch_shapes=[pltpu.VMEM((TM_MOE, D_FF), BF16),
                            pltpu.VMEM((D_MODEL, 2 * D_FF), BF16),
                            pltpu.VMEM((D_FF, D_MODEL), BF16)]),
        out_shape=jax.ShapeDtypeStruct((a * CHUNKS, LANES), F32),
        compiler_params=pltpu.CompilerParams(dimension_semantics=("arbitrary",),
                                             vmem_limit_bytes=VMEM_LIMIT_MOE),
        name="moe",
    )(*sched, xs, w1, b1, w2, b2)


def _moe_schedule(counts, n_rows):
    n_tiles = n_rows // TM_MOE
    n_items = n_tiles + N_EXPERTS - 1
    pend = jnp.cumsum(counts)
    pstart = pend - counts
    first_tile = pstart // TM_MOE
    last_tile = jnp.maximum(pend - 1, 0) // TM_MOE
    ntile = jnp.where(counts > 0, last_tile - first_tile + 1, 0)
    wend = jnp.cumsum(ntile)
    wstart = wend - ntile
    total = wend[-1]
    w = jnp.arange(n_items, dtype=I32)
    wv = jnp.minimum(w, total - 1)
    ex = jnp.minimum(jnp.sum(wend[None, :] <= wv[:, None], 1), N_EXPERTS - 1).astype(I32)
    is_ex = ex[:, None] == jnp.arange(N_EXPERTS, dtype=I32)[None, :]
    of_ex = lambda per_expert: jnp.sum(jnp.where(is_ex, per_expert[None, :], 0), 1)
    tile = (of_ex(first_tile) + wv - of_ex(wstart)).astype(I32)
    valid = w < total
    lo = jnp.where(valid, jnp.clip(of_ex(pstart) - tile * TM_MOE, 0, TM_MOE), 0).astype(I32)
    hi = jnp.where(valid, jnp.clip(of_ex(pend) - tile * TM_MOE, 0, TM_MOE), 0).astype(I32)
    prev_tile = jnp.concatenate([jnp.full((1,), -1, I32), tile[:-1]])
    first = (valid & (tile != prev_tile)).astype(I32)
    prev_ex = jnp.concatenate([jnp.full((1,), -1, I32), ex[:-1]])
    newexp = (valid & (ex != prev_ex)).astype(I32)
    return (tile, ex, lo, hi, first, newexp), pstart


def _combine_kernel(dcur_ref, dnext_ref, c_ref, gate_ref, g2_ref, b2_ref, ys_hbm, ys_flat_hbm, *rest,
                    n_prompt):
    *outs, buf, sem = rest
    tm = c_ref.shape[0]
    i = pl.program_id(0)

    def gather_tile(dest_ref, slot):
        def gather_row(k, j, s, n):
            r = j * SUBLANES + s
            dst = buf.at[slot, k, pl.ds(pl.multiple_of(r * CHUNKS, CHUNKS), CHUNKS)]
            _row_copy(ys_hbm.at[dest_ref[0, k * tm + r]], dst, sem.at[slot]).start(priority=n % 2)

        _for_each_row_slot(tm // SUBLANES, gather_row)

    @pl.when(i == 0)
    def _():
        gather_tile(dcur_ref, 0)

    @pl.when(i + 1 < pl.num_programs(0))
    def _():
        gather_tile(dnext_ref, (i + 1) % 2)

    slot = i % 2
    for k in range(TOP_K):
        _row_copy(ys_flat_hbm.at[pl.ds(0, tm * CHUNKS)], buf.at[slot, k], sem.at[slot]).wait()
    gates = gate_ref[...]
    gate_cols = [jnp.broadcast_to(gates[:, k:k + 1], (tm, LANES)) for k in range(TOP_K)]
    parts = []
    for c in range(CHUNKS):
        part = c_ref[:, c * LANES:(c + 1) * LANES]
        for k in range(TOP_K):
            part = part + gate_cols[k] * buf[slot, k, pl.ds(c, tm, stride=CHUNKS), :]
        parts.append(part)
    acc = jnp.concatenate(parts, axis=1)
    y = _layer_norm(acc, g2_ref[...], b2_ref[...])
    if n_prompt is None:
        outs[0][...] = y
    else:
        @pl.when(i < n_prompt)
        def _():
            outs[0][...] = y

        @pl.when(i >= n_prompt)
        def _():
            outs[1][...] = y


def _combine(dest, c, gates_t, g2, b2, ys, t_prompt=None):
    t = c.shape[0]
    tm = TM_COMBINE
    n = t // tm
    vec = _full((1, D_MODEL))
    if t_prompt is None:
        n_p = None
        out_specs = pl.BlockSpec((tm, D_MODEL), lambda i: (i, 0))
        out_shape = jax.ShapeDtypeStruct((t, D_MODEL), F32)
    else:
        n_p = t_prompt // tm
        out_specs = _stream_specs(n_p, n - n_p, tm, D_MODEL)
        out_shape = [jax.ShapeDtypeStruct((t_prompt, D_MODEL), F32),
                     jax.ShapeDtypeStruct((t - t_prompt, D_MODEL), F32)]
    dest_spec = lambda f: pl.BlockSpec((None, 1, TOP_K * tm), lambda i: (f(i), 0, 0),
                                       memory_space=pltpu.SMEM)
    dest_tiles = _dest_tiles(dest, tm)
    return pl.pallas_call(
        functools.partial(_combine_kernel, n_prompt=n_p),
        grid=(n,),
        in_specs=[dest_spec(lambda i: i), dest_spec(lambda i: jnp.minimum(i + 1, n - 1)),
                  pl.BlockSpec((tm, D_MODEL), lambda i: (i, 0)),
                  pl.BlockSpec((tm, TOP_K), lambda i: (i, 0)),
                  vec, vec, pl.BlockSpec(memory_space=pl.ANY), pl.BlockSpec(memory_space=pl.ANY)],
        out_specs=out_specs,
        out_shape=out_shape,
        scratch_shapes=[pltpu.VMEM((2, TOP_K, tm * CHUNKS, LANES), F32),
                        pltpu.SemaphoreType.DMA((2,))],
        compiler_params=_cparams(("arbitrary",)), name="combine",
    )(dest_tiles, dest_tiles, c, gates_t, g2, b2, _token_tiles(ys), ys)


def kernel(x_prompt, x_sample, cache_k, cache_v, state_conv, p_prompt, p_sample, ln_emb_g, ln_emb_b,
           w_in, b_in, conv_w, sinks, w_o, b_o, ln1_g, ln1_b, w_gate, b_gate, w_ple, w_router,
           b_router, w1, b1, w2, b2, ln2_g, ln2_b):
    batch, seq, _ = x_prompt.shape
    nseq, dec_seq, _ = x_sample.shape
    t_p, t_s = batch * seq, nseq * dec_seq
    t = t_p + t_s
    w_cache = cache_k.shape[2]
    vec = lambda a: a.reshape(1, -1)

    x = (x_prompt.reshape(t_p, D_MODEL), x_sample.reshape(t_s, D_MODEL))
    state_rows = jnp.pad(state_conv, ((0, 0), (0, 0), (0, dec_seq - (CONV_K - 1)), (0, 0)))
    state_rows = state_rows.reshape(DEPTH, t_s, CONV_WIDTH)

    ks_p, vs_p, cs_p, ks_s, vs_s, cs_s = [], [], [], [], [], []
    for l in range(DEPTH):
        w_in_b = w_in[l].astype(BF16)
        xn, q, k, v, gb, u = _inproj(x, (vec(ln_emb_g), vec(ln_emb_b)), w_in_b, vec(b_in[l]))

        mix_p, nk_p, nv_p, nu_p = _prompt_mixer(sinks[l], q, k, v, u, gb, conv_w[l], batch, seq)
        mix_s, nk_s, nv_s = _sample_mixer(
            sinks[l], q, k, v, u, gb, conv_w[l], l,
            cache_k.reshape(DEPTH, nseq, w_cache, KV_WIDTH),
            cache_v.reshape(DEPTH, nseq, w_cache, KV_WIDTH), state_rows[l], t_p)

        wr_t = w_router[l].T
        wr_hi = wr_t.astype(BF16)
        wr_lo = (wr_t - wr_hi.astype(F32)).astype(BF16)
        x1, c, eid, gates, ranks, cnt = _post_mixer(
            mix_p, mix_s, xn, l, p_prompt.reshape(DEPTH, t_p, PLE_DIM),
            p_sample.reshape(DEPTH, t_s, PLE_DIM), w_o[l].astype(BF16), vec(b_o[l]), vec(ln1_g[l]), vec(ln1_b[l]),
            w_gate[l].astype(BF16), vec(b_gate[l]), w_ple[l].astype(BF16), wr_hi, wr_lo,
            b_router[l].reshape(N_EXPERTS, 1))

        counts = cnt[:, 0].astype(I32)
        sched, pstart = _moe_schedule(counts, t * TOP_K)
        onehot = eid[:, :, None] == jnp.arange(N_EXPERTS, dtype=I32)
        dest = ranks + jnp.sum(jnp.where(onehot, pstart, 0), -1)

        xs = _dispatch(dest, x1)
        ys = _moe(sched, xs, l, w1, b1[:, :, None, :], w2, b2[:, :, None, :])
        x = _combine(dest, c, gates.T, vec(ln2_g[l]), vec(ln2_b[l]), ys,
                     t_prompt=t_p if l == DEPTH - 1 else None)

        ks_p.append(nk_p.reshape(batch, WINDOW, N_KV_HEADS, HEAD_DIM))
        vs_p.append(nv_p.reshape(batch, WINDOW, N_KV_HEADS, HEAD_DIM))
        cs_p.append(nu_p[:, SUBLANES - (CONV_K - 1):])
        ks_s.append(nk_s.reshape(nseq, w_cache, N_KV_HEADS, HEAD_DIM))
        vs_s.append(nv_s.reshape(nseq, w_cache, N_KV_HEADS, HEAD_DIM))
        cs_s.append(u[t_p:].reshape(nseq, dec_seq, CONV_WIDTH)[:, dec_seq - (CONV_K - 1):])

    y_prompt = x[0].reshape(batch, seq, D_MODEL)
    y_sample = x[1].reshape(nseq, dec_seq, D_MODEL)
    return (y_prompt, y_sample, jnp.stack(ks_p), jnp.stack(vs_p), jnp.stack(cs_p),
            jnp.stack(ks_s), jnp.stack(vs_s), jnp.stack(cs_s))
```

```python
import functools

import jax
import jax.numpy as jnp
from jax import lax
from jax.experimental import pallas as pl
from jax.experimental.pallas import tpu as pltpu

F32 = jnp.float32
BF16 = jnp.bfloat16
I32 = jnp.int32

D_MODEL = 1024
DEPTH = 2
HEAD_DIM = 64
N_Q_HEADS = 8
N_KV_HEADS = 2
ATTN_WIDTH = N_Q_HEADS * HEAD_DIM
KV_WIDTH = N_KV_HEADS * HEAD_DIM
CONV_WIDTH = D_MODEL - ATTN_WIDTH
WINDOW = 128
ATTN_SCALE = HEAD_DIM ** -0.5
CONV_K = 3
N_EXPERTS = 32
TOP_K = 4
D_FF = D_MODEL
SWIGLU_LIMIT = 7.0
SWIGLU_ALPHA = 1.702
PLE_DIM = 256
LN_EPS = 1e-5
DN_ALPHA = (2.0 * DEPTH) ** 0.25
IN_COLS = ATTN_WIDTH + 2 * KV_WIDTH + 3 * CONV_WIDTH
Q0, K0, V0, GB0, GC0, H0 = 0, 512, 640, 768, 1280, 1792

LANES = 128
SUBLANES = 8
VMEM_LIMIT = 48 * 1024 * 1024
VMEM_LIMIT_MOE = 56 * 1024 * 1024

TM = 512
TQ = 512
SEQ_GROUP = 8
TM_DISPATCH = 1024
TM_COMBINE = 256
TM_MOE = 512


def _layer_norm(x, g, b):
    mu = jnp.mean(x, -1, keepdims=True)
    xc = x - mu
    var = jnp.mean(xc * xc, -1, keepdims=True)
    return xc * lax.rsqrt(var + LN_EPS) * g + b


def _div_pow2(x, n):
    assert n & (n - 1) == 0
    return lax.shift_right_arithmetic(x, n.bit_length() - 1)


def _mod_pow2(x, n):
    assert n & (n - 1) == 0
    return x & (n - 1)


def _cparams(sem):
    return pltpu.CompilerParams(dimension_semantics=sem, vmem_limit_bytes=VMEM_LIMIT)


def _full(shape):
    return pl.BlockSpec(shape, lambda *_: (0,) * len(shape))


def _stream_specs(n_p, n_s, tm, width, layer=None):
    prompt_tile = lambda i: jnp.minimum(i, n_p - 1)
    sample_tile = lambda i: jnp.clip(i - n_p, 0, n_s - 1)
    if layer is None:
        return [pl.BlockSpec((tm, width), lambda i: (prompt_tile(i), 0)),
                pl.BlockSpec((tm, width), lambda i: (sample_tile(i), 0))]
    return [pl.BlockSpec((None, tm, width), lambda i: (layer, prompt_tile(i), 0)),
            pl.BlockSpec((None, tm, width), lambda i: (layer, sample_tile(i), 0))]


def _stream_tile(n_p, prompt_ref, sample_ref):
    return jnp.where(pl.program_id(0) < n_p, prompt_ref[...], sample_ref[...])


def _inproj_kernel(*refs, n_prompt):
    if n_prompt is not None:
        xp_ref, xs_ref, g_ref, b_ref, w_ref, bi_ref, xn_ref, q_ref, k_ref, v_ref, gb_ref, u_ref = refs
        x = _layer_norm(_stream_tile(n_prompt, xp_ref, xs_ref), g_ref[...], b_ref[...])
        xn_ref[...] = x
    else:
        x_ref, w_ref, bi_ref, q_ref, k_ref, v_ref, gb_ref, u_ref = refs
        x = x_ref[...]
    xb = x.astype(BF16)

    def proj(lo, hi):
        return jnp.dot(xb, w_ref[:, lo:hi], preferred_element_type=F32) + bi_ref[:, lo:hi]

    q_ref[...] = (proj(Q0, K0) * ATTN_SCALE).astype(BF16)
    k_ref[...] = proj(K0, V0)
    v_ref[...] = proj(V0, GB0)
    gb_ref[...] = proj(GB0, GC0)
    u_ref[...] = proj(GC0, H0) * proj(H0, IN_COLS)


def _inproj(x, ln, w_in_b, b_in):
    apply_ln = isinstance(x, tuple)
    row = lambda w: pl.BlockSpec((TM, w), lambda i: (i, 0))
    n_p = None
    if apply_ln:
        n_p, n_s = x[0].shape[0] // TM, x[1].shape[0] // TM
        t = x[0].shape[0] + x[1].shape[0]
        in_specs = _stream_specs(n_p, n_s, TM, D_MODEL) + [_full((1, D_MODEL)), _full((1, D_MODEL))]
        args = [x[0], x[1], ln[0], ln[1]]
    else:
        t = x.shape[0]
        in_specs = [row(D_MODEL)]
        args = [x]
    in_specs += [_full((D_MODEL, IN_COLS)), _full((1, IN_COLS))]
    args += [w_in_b, b_in]
    out_shape, out_specs = [], []
    if apply_ln:
        out_shape.append(jax.ShapeDtypeStruct((t, D_MODEL), F32))
        out_specs.append(row(D_MODEL))
    out_shape += [jax.ShapeDtypeStruct((t, ATTN_WIDTH), BF16),
                  jax.ShapeDtypeStruct((t, KV_WIDTH), F32),
                  jax.ShapeDtypeStruct((t, KV_WIDTH), F32),
                  jax.ShapeDtypeStruct((t, CONV_WIDTH), F32),
                  jax.ShapeDtypeStruct((t, CONV_WIDTH), F32)]
    out_specs += [row(ATTN_WIDTH), row(KV_WIDTH), row(KV_WIDTH), row(CONV_WIDTH), row(CONV_WIDTH)]
    outs = pl.pallas_call(
        functools.partial(_inproj_kernel, n_prompt=n_p),
        grid=(t // TM,), in_specs=in_specs, out_specs=out_specs, out_shape=out_shape,
        compiler_params=_cparams(("arbitrary",)), name="inproj")(*args)
    if apply_ln:
        return outs
    return [x] + list(outs)


def _attend_column(qcol, kexp_b, vexp_b, mask, sink_a, sink_b, fold=None):
    lane = lax.broadcasted_iota(I32, qcol.shape, 1)
    outs = []
    for half, sink in ((0, sink_a), (1, sink_b)):
        keep = (lane < HEAD_DIM) if half == 0 else (lane >= HEAD_DIM)
        qm = jnp.where(keep, qcol, jnp.zeros_like(qcol))
        s = lax.dot_general(qm, kexp_b, (((1,), (1,)), ((), ())), preferred_element_type=F32)
        if fold is None:
            s = jnp.where(mask, s, -jnp.inf)
        else:
            upper, prev_ok = fold
            s_prev, s_cur = s[:, :WINDOW], s[:, WINDOW:]
            if prev_ok is not None:
                s_prev = jnp.where(prev_ok, s_prev, -jnp.inf)
            s = jnp.where(upper, s_prev, s_cur)
        m = jnp.maximum(jnp.max(s, -1, keepdims=True), sink)
        p = jnp.exp(s - m)
        denom = jnp.sum(p, -1, keepdims=True) + jnp.exp(sink - m)
        if fold is not None:
            zero = jnp.zeros_like(p)
            p = jnp.concatenate([jnp.where(upper, p, zero), jnp.where(upper, zero, p)], axis=1)
        o = jnp.dot(p.astype(BF16), vexp_b, preferred_element_type=F32)
        outs.append(o * (1.0 / denom))
    return jnp.where(lane < HEAD_DIM, outs[0], outs[1])


def _dup_heads(x):
    lane = lax.broadcasted_iota(I32, x.shape, 1)
    xr = pltpu.roll(x, HEAD_DIM, 1)
    low = lane < HEAD_DIM
    return (jnp.where(low, x, xr).astype(BF16), jnp.where(low, xr, x).astype(BF16))


def _prompt_mixer_kernel(sinks_ref, q_ref, kc_ref, kp_ref, vc_ref, vp_ref, uc_ref, up_ref,
                         gb_ref, cw_ref, o_ref, nk_ref, nv_ref, nu_ref):
    first = pl.program_id(1) == 0

    @pl.when(pl.program_id(1) == pl.num_programs(1) - 1)
    def _():
        nk_ref[...] = kc_ref[TQ - WINDOW:, :]
        nv_ref[...] = vc_ref[TQ - WINDOW:, :]
        nu_ref[...] = uc_ref[TQ - SUBLANES:, :]

    r_i = lax.broadcasted_iota(I32, (WINDOW, WINDOW), 0)
    c_i = lax.broadcasted_iota(I32, (WINDOW, WINDOW), 1)
    upper = c_i > r_i
    has_prev = jnp.logical_not(first)
    for s in range(TQ // WINDOW):
        rows = slice(s * WINDOW, (s + 1) * WINDOW)
        if s == 0:
            kprev, vprev = kp_ref[...], vp_ref[...]
        else:
            prev = slice((s - 1) * WINDOW, s * WINDOW)
            kprev, vprev = kc_ref[prev, :], vc_ref[prev, :]
        kk = jnp.concatenate([kprev, kc_ref[rows, :]], 0)
        vv = jnp.concatenate([vprev, vc_ref[rows, :]], 0)
        kexp = _dup_heads(kk)
        vexp = _dup_heads(vv)
        fold = (upper, has_prev if s == 0 else None)
        for col in range(ATTN_WIDTH // LANES):
            h = col // 2
            cols = slice(col * LANES, (col + 1) * LANES)
            out = _attend_column(q_ref[rows, cols], kexp[h], vexp[h], None,
                                 sinks_ref[2 * col], sinks_ref[2 * col + 1], fold=fold)
            o_ref[rows, cols] = out.astype(BF16)

    u = uc_ref[...]
    up = up_ref[...]
    zero = jnp.zeros((1, CONV_WIDTH), F32)
    p1 = jnp.where(first, zero, up[SUBLANES - 1:SUBLANES, :])
    p2 = jnp.where(first, zero, up[SUBLANES - 2:SUBLANES - 1, :])
    row = lax.broadcasted_iota(I32, u.shape, 0)
    u1 = jnp.where(row == 0, p1, pltpu.roll(u, 1, 0))
    u2 = jnp.where(row == 0, p2, jnp.where(row == 1, p1, pltpu.roll(u, 2, 0)))
    cw = cw_ref[...]
    y = u2 * cw[0:1, :] + u1 * cw[1:2, :] + u * cw[2:3, :]
    o_ref[:, ATTN_WIDTH:] = (gb_ref[...] * y).astype(BF16)


def _prompt_mixer(sinks, q, k, v, u, gb, conv_w, batch, seq):
    nj = seq // TQ
    tile = lambda b, j: b * nj + j
    cur = lambda w: pl.BlockSpec((TQ, w), lambda b, j: (tile(b, j), 0))
    prev_kv = pl.BlockSpec((WINDOW, KV_WIDTH),
                           lambda b, j: (jnp.maximum(tile(b, j) * (TQ // WINDOW) - 1, 0), 0))
    prev_u = pl.BlockSpec((SUBLANES, CONV_WIDTH),
                          lambda b, j: (jnp.maximum(tile(b, j) * (TQ // SUBLANES) - 1, 0), 0))
    return pl.pallas_call(
        _prompt_mixer_kernel,
        grid=(batch, nj),
        in_specs=[pl.BlockSpec(memory_space=pltpu.SMEM),
                  cur(ATTN_WIDTH), cur(KV_WIDTH), prev_kv, cur(KV_WIDTH), prev_kv,
                  cur(CONV_WIDTH), prev_u, cur(CONV_WIDTH), _full((CONV_K, CONV_WIDTH))],
        out_specs=[cur(D_MODEL),
                   pl.BlockSpec((None, WINDOW, KV_WIDTH), lambda b, j: (b, 0, 0)),
                   pl.BlockSpec((None, WINDOW, KV_WIDTH), lambda b, j: (b, 0, 0)),
                   pl.BlockSpec((None, SUBLANES, CONV_WIDTH), lambda b, j: (b, 0, 0))],
        out_shape=[jax.ShapeDtypeStruct((batch * seq, D_MODEL), BF16),
                   jax.ShapeDtypeStruct((batch, WINDOW, KV_WIDTH), F32),
                   jax.ShapeDtypeStruct((batch, WINDOW, KV_WIDTH), F32),
                   jax.ShapeDtypeStruct((batch, SUBLANES, CONV_WIDTH), F32)],
        compiler_params=_cparams(("arbitrary", "arbitrary")), name="prompt_mixer",
    )(sinks, q, k, k, v, v, u, u, gb, conv_w)


def _sample_mixer_kernel(sinks_ref, q_ref, kn_ref, vn_ref, kb_ref, vb_ref, u_ref, st_ref, gb_ref,
                         cw_ref, o_ref, nk_ref, nv_ref, *, dec_seq):
    g, w = kb_ref.shape[0], kb_ref.shape[1]
    rows = g * dec_seq
    n_cache = g * w
    n_keys = n_cache + 2 * rows
    kn, vn = kn_ref[...], vn_ref[...]
    pad = jnp.zeros((rows, KV_WIDTH), F32)
    kk = jnp.concatenate([kb_ref[...].reshape(n_cache, KV_WIDTH), kn, pad], 0)
    vv = jnp.concatenate([vb_ref[...].reshape(n_cache, KV_WIDTH), vn, pad], 0)
    kexp = _dup_heads(kk)
    vexp = _dup_heads(vv)

    r_i = lax.broadcasted_iota(I32, (rows, n_keys), 0)
    c_i = lax.broadcasted_iota(I32, (rows, n_keys), 1)
    r_seq, r_pos = _div_pow2(r_i, dec_seq), _mod_pow2(r_i, dec_seq)
    c_new = c_i - n_cache
    in_cache = ((c_i < n_cache) & (_div_pow2(c_i, w) == r_seq)
                & (_mod_pow2(c_i, w) > r_pos + (w - WINDOW)))
    in_new = ((c_new >= 0) & (c_new < rows) & (_div_pow2(c_new, dec_seq) == r_seq)
              & (_mod_pow2(c_new, dec_seq) <= r_pos))
    mask = in_cache | in_new
    for col in range(ATTN_WIDTH // LANES):
        h = col // 2
        cols = slice(col * LANES, (col + 1) * LANES)
        out = _attend_column(q_ref[:, cols], kexp[h], vexp[h], mask,
                             sinks_ref[2 * col], sinks_ref[2 * col + 1])
        o_ref[:, cols] = out.astype(BF16)

    nk_ref[:, 0:w - dec_seq, :] = kb_ref[:, dec_seq:w, :]
    nk_ref[:, w - dec_seq:w, :] = kn.reshape(g, dec_seq, KV_WIDTH)
    nv_ref[:, 0:w - dec_seq, :] = vb_ref[:, dec_seq:w, :]
    nv_ref[:, w - dec_seq:w, :] = vn.reshape(g, dec_seq, KV_WIDTH)

    u = u_ref[...]
    st = st_ref[...]
    pos = lax.broadcasted_iota(I32, u.shape, 0) % dec_seq
    u1 = jnp.where(pos == 0, pltpu.roll(st, rows - 1, 0), pltpu.roll(u, 1, 0))
    u2 = jnp.where(pos < 2, st, pltpu.roll(u, 2, 0))
    cw = cw_ref[...]
    y = u2 * cw[0:1, :] + u1 * cw[1:2, :] + u * cw[2:3, :]
    o_ref[:, ATTN_WIDTH:] = (gb_ref[...] * y).astype(BF16)


def _sample_mixer(sinks, q, k, v, u, gb, conv_w, layer, cache_k, cache_v, state_rows, t_prompt):
    nseq, w = cache_k.shape[1], cache_k.shape[2]
    dec_seq = (q.shape[0] - t_prompt) // nseq
    rows = SEQ_GROUP * dec_seq
    off = t_prompt // rows
    tok = lambda wd: pl.BlockSpec((rows, wd), lambda i: (off + i, 0))
    local = lambda wd: pl.BlockSpec((rows, wd), lambda i: (i, 0))
    cache_in = pl.BlockSpec((None, SEQ_GROUP, w, KV_WIDTH), lambda i: (layer, i, 0, 0))
    cache = pl.BlockSpec((SEQ_GROUP, w, KV_WIDTH), lambda i: (i, 0, 0))
    return pl.pallas_call(
        functools.partial(_sample_mixer_kernel, dec_seq=dec_seq),
        grid=(nseq // SEQ_GROUP,),
        in_specs=[pl.BlockSpec(memory_space=pltpu.SMEM),
                  tok(ATTN_WIDTH), tok(KV_WIDTH), tok(KV_WIDTH), cache_in, cache_in,
                  tok(CONV_WIDTH), local(CONV_WIDTH), tok(CONV_WIDTH),
                  _full((CONV_K, CONV_WIDTH))],
        out_specs=[local(D_MODEL), cache, cache],
        out_shape=[jax.ShapeDtypeStruct((nseq * dec_seq, D_MODEL), BF16),
                   jax.ShapeDtypeStruct((nseq, w, KV_WIDTH), F32),
                   jax.ShapeDtypeStruct((nseq, w, KV_WIDTH), F32)],
        compiler_params=_cparams(("parallel",)), name="sample_mixer",
    )(sinks, q, k, v, cache_k, cache_v, u, state_rows, gb, conv_w)


def _post_mixer_kernel(mixp_ref, mixs_ref, xn_ref, pp_ref, ps_ref, wo_ref, bo_ref, g1_ref, b1_ref,
                       wg_ref, bg_ref, wp_ref, wrh_ref, wrl_ref, br_ref,
                       x1_ref, c_ref, eid_ref, gate_ref, rank_ref, cnt_ref, carry_ref, *, n_prompt):
    @pl.when(pl.program_id(0) == 0)
    def _():
        carry_ref[...] = jnp.zeros_like(carry_ref)

    mixed = _stream_tile(n_prompt, mixp_ref, mixs_ref)
    mix = jnp.dot(mixed, wo_ref[...], preferred_element_type=F32) + bo_ref[...]
    x1 = _layer_norm(DN_ALPHA * xn_ref[...] + mix, g1_ref[...], b1_ref[...])
    for c in range(CHUNKS):
        x1_ref[pl.ds(c, x1.shape[0], stride=CHUNKS), :] = x1[:, c * LANES:(c + 1) * LANES]
    x1h = x1.astype(BF16)
    x1l = (x1 - x1h.astype(F32)).astype(BF16)
    gate = jax.nn.sigmoid(jnp.dot(x1h, wg_ref[...], preferred_element_type=F32) + bg_ref[...])
    p_tile = _stream_tile(n_prompt, pp_ref, ps_ref).astype(BF16)
    ple = jnp.dot(p_tile, wp_ref[...], preferred_element_type=F32)
    c_ref[...] = DN_ALPHA * x1 + gate * ple

    nt = (((1,), (1,)), ((), ()))
    logits = (lax.dot_general(wrh_ref[...], x1h, nt, preferred_element_type=F32)
              + lax.dot_general(wrh_ref[...], x1l, nt, preferred_element_type=F32)
              + lax.dot_general(wrl_ref[...], x1h, nt, preferred_element_type=F32)
              + br_ref[...])
    tm = logits.shape[1]
    e_i = lax.broadcasted_iota(I32, logits.shape, 0).astype(F32)
    work = logits
    vals, sels = [], []
    for k in range(TOP_K):
        m = jnp.max(work, 0, keepdims=True)
        idx = jnp.min(jnp.where(work == m, e_i, float(N_EXPERTS)), 0, keepdims=True)
        sel = e_i == idx
        vals.append(m)
        sels.append(sel)
        eid_ref[k:k + 1, :] = idx.astype(I32)
        work = jnp.where(sel, -jnp.inf, work)
    exps = [jnp.exp(v - vals[0]) for v in vals]
    denom = exps[0] + exps[1] + exps[2] + exps[3]
    for k in range(TOP_K):
        gate_ref[k:k + 1, :] = exps[k] / denom

    chosen = jnp.where(sels[0] | sels[1] | sels[2] | sels[3], 1.0, 0.0)
    s_i = lax.broadcasted_iota(I32, (tm, tm), 0)
    t_i = lax.broadcasted_iota(I32, (tm, tm), 1)
    before = jnp.where(s_i < t_i, 1.0, 0.0).astype(BF16)
    pos = carry_ref[:, 0:1] + jnp.dot(chosen.astype(BF16), before, preferred_element_type=F32)
    for k in range(TOP_K):
        rank_ref[k:k + 1, :] = jnp.sum(jnp.where(sels[k], pos, 0.0), 0, keepdims=True).astype(I32)
    carry = carry_ref[...] + jnp.sum(chosen, 1, keepdims=True)
    carry_ref[...] = carry
    cnt_ref[...] = carry


def _post_mixer(mix_p, mix_s, xn, layer, p_p, p_s, w_o, b_o, g1, b1, w_gate, b_gate, w_ple, wr_hi,
                wr_lo, b_r):
    t = xn.shape[0]
    n_p, n_s = mix_p.shape[0] // TM, mix_s.shape[0] // TM
    row = lambda w: pl.BlockSpec((TM, w), lambda i: (i, 0))
    meta = pl.BlockSpec((TOP_K, TM), lambda i: (0, i))
    vec = _full((1, D_MODEL))
    return pl.pallas_call(
        functools.partial(_post_mixer_kernel, n_prompt=n_p),
        grid=(t // TM,),
        in_specs=_stream_specs(n_p, n_s, TM, D_MODEL) + [row(D_MODEL)]
        + _stream_specs(n_p, n_s, TM, PLE_DIM, layer)
        + [_full((D_MODEL, D_MODEL)), vec, vec, vec,
           _full((D_MODEL, D_MODEL)), vec, _full((PLE_DIM, D_MODEL)),
           _full((N_EXPERTS, D_MODEL)), _full((N_EXPERTS, D_MODEL)), _full((N_EXPERTS, 1))],
        out_specs=[pl.BlockSpec((TM * CHUNKS, LANES), lambda i: (i, 0)), row(D_MODEL),
                   meta, meta, meta, _full((N_EXPERTS, LANES))],
        out_shape=[jax.ShapeDtypeStruct((t * CHUNKS, LANES), F32),
                   jax.ShapeDtypeStruct((t, D_MODEL), F32),
                   jax.ShapeDtypeStruct((TOP_K, t), I32),
                   jax.ShapeDtypeStruct((TOP_K, t), F32),
                   jax.ShapeDtypeStruct((TOP_K, t), I32),
                   jax.ShapeDtypeStruct((N_EXPERTS, LANES), F32)],
        scratch_shapes=[pltpu.VMEM((N_EXPERTS, LANES), F32)],
        compiler_params=_cparams(("arbitrary",)), name="post_mixer",
    )(mix_p, mix_s, xn, p_p, p_s, w_o, b_o, g1, b1, w_gate, b_gate, w_ple, wr_hi, wr_lo, b_r)


def _row_copy(src, dst, sem):
    return pltpu.make_async_copy(src, dst, sem)


CHUNKS = D_MODEL // LANES
assert CHUNKS == SUBLANES


def _load_token_tiles(ref, r0, n):
    return jnp.concatenate(
        [ref[pl.ds(r0 * CHUNKS + c, n, stride=CHUNKS), :] for c in range(CHUNKS)], axis=1)


def _token_tiles(x):
    if x.ndim == 2:
        return x.reshape(x.shape[0] // CHUNKS, CHUNKS, LANES)
    return x.reshape(x.shape[0] * CHUNKS, LANES)


def _for_each_row_slot(groups, fn):
    def group(j, carry):
        for s in range(SUBLANES):
            for k in range(TOP_K):
                fn(k, j, s, s * TOP_K + k)
        return carry

    lax.fori_loop(0, groups, group, 0)


def _dispatch_kernel(dest_ref, x_ref, xs_hbm, sem):
    tm = x_ref.shape[0]

    def scatter_row(k, j, s, n):
        r = j * SUBLANES + s
        _row_copy(x_ref.at[r], xs_hbm.at[dest_ref[0, k * tm + r]], sem).start(priority=n % 2)

    _for_each_row_slot(tm // SUBLANES, scatter_row)
    for k in range(TOP_K):
        _row_copy(x_ref, xs_hbm.at[pl.ds(0, tm)], sem).wait()


def _dest_tiles(dest, tm):
    t = dest.shape[1]
    return dest.reshape(TOP_K, t // tm, tm).transpose(1, 0, 2).reshape(t // tm, 1, TOP_K * tm)


def _dispatch(dest, x1t):
    t = x1t.shape[0] // CHUNKS
    tm = TM_DISPATCH
    xs = pl.pallas_call(
        _dispatch_kernel,
        grid=(t // tm,),
        in_specs=[pl.BlockSpec((None, 1, TOP_K * tm), lambda i: (i, 0, 0), memory_space=pltpu.SMEM),
                  pl.BlockSpec((tm, CHUNKS, LANES), lambda i: (i, 0, 0))],
        out_specs=pl.BlockSpec(memory_space=pl.ANY),
        out_shape=jax.ShapeDtypeStruct((t * TOP_K, CHUNKS, LANES), F32),
        scratch_shapes=[pltpu.SemaphoreType.DMA],
        compiler_params=_cparams(("arbitrary",)), name="dispatch",
    )(_dest_tiles(dest, tm), _token_tiles(x1t))
    return _token_tiles(xs)


def _moe_kernel(tile_ref, exp_ref, lo_ref, hi_ref, first_ref, newexp_ref,
                xs_ref, w1_ref, b1_ref, w2_ref, b2_ref, ys_ref, act_ref, w1b_ref, w2b_ref):
    del tile_ref, exp_ref
    w = pl.program_id(0)
    lo, hi = lo_ref[w], hi_ref[w]

    @pl.when(newexp_ref[w] == 1)
    def _():
        w1b_ref[...] = w1_ref[...].astype(BF16)
        r_i = lax.broadcasted_iota(I32, (LANES, LANES), 0)
        c_i = lax.broadcasted_iota(I32, (LANES, LANES), 1)
        src = (LANES // 2) * (r_i & 1) + lax.shift_right_logical(r_i, 1)
        perm = jnp.where(c_i == src, 1.0, 0.0).astype(BF16)
        for m in range(D_FF // LANES):
            blk = slice(m * LANES, (m + 1) * LANES)
            w2b_ref[blk, :] = jnp.dot(perm, w2_ref[blk, :].astype(BF16),
                                      preferred_element_type=F32).astype(BF16)

    def expert_mlp(r0, nrows, first_visit):
        rows = slice(r0, r0 + nrows)
        xb = _load_token_tiles(xs_ref, r0, nrows).astype(BF16)
        even = (lax.broadcasted_iota(I32, (nrows, LANES), 1) & 1) == 0
        for m in range(D_FF // LANES):
            cols = slice(2 * m * LANES, 2 * (m + 1) * LANES)
            h = jnp.dot(xb, w1b_ref[:, cols], preferred_element_type=F32) + b1_ref[:, cols]
            ha, hb = h[:, :LANES], h[:, LANES:]
            glu = jnp.where(even, ha, pltpu.roll(hb, 1, 1))
            lin = jnp.where(even, pltpu.roll(ha, LANES - 1, 1), hb)
            glu = jnp.minimum(glu, SWIGLU_LIMIT)
            lin = jnp.clip(lin, -SWIGLU_LIMIT, SWIGLU_LIMIT)
            act = glu * jax.nn.sigmoid(SWIGLU_ALPHA * glu) * (lin + 1.0)
            act_ref[rows, m * LANES:(m + 1) * LANES] = act.astype(BF16)
        r = r0 + lax.broadcasted_iota(I32, (nrows, LANES), 0)
        mine = (r >= lo) & (r < hi)
        if first_visit and nrows < TM_MOE:
            ys_ref[pl.ds((nrows - r0) * CHUNKS, (TM_MOE - nrows) * CHUNKS), :] = jnp.zeros(
                ((TM_MOE - nrows) * CHUNKS, LANES), F32)
        width = 2 * LANES
        for jb in range(D_MODEL // width):
            cols = slice(jb * width, (jb + 1) * width)
            y = jnp.dot(act_ref[rows, :], w2b_ref[:, cols], preferred_element_type=F32) + b2_ref[:, cols]
            for cc in range(width // LANES):
                at = pl.ds(r0 * CHUNKS + jb * (width // LANES) + cc, nrows, stride=CHUNKS)
                old = 0.0 if first_visit else ys_ref[at, :]
                ys_ref[at, :] = jnp.where(mine, y[:, cc * LANES:(cc + 1) * LANES], old)

    half = TM_MOE // 2
    some = hi > lo
    for first_visit in (True, False):
        visit = some & (first_ref[w] == (1 if first_visit else 0))

        @pl.when(visit & (hi <= half))
        def _():
            expert_mlp(0, half, first_visit)

        @pl.when(visit & (lo >= half))
        def _():
            expert_mlp(half, half, first_visit)

        @pl.when(visit & (lo < half) & (hi > half))
        def _():
            expert_mlp(0, TM_MOE, first_visit)


def _moe(sched, xs, layer, w1, b1, w2, b2):
    a = xs.shape[0] // CHUNKS
    n_items = sched[0].shape[0]
    by_expert = lambda r, c: pl.BlockSpec((None, None, r, c),
                                          lambda w, tile, ex, *_: (layer, ex[w], 0, 0))
    rows = pl.BlockSpec((TM_MOE * CHUNKS, LANES), lambda w, tile, *_: (tile[w], 0))
    return pl.pallas_call(
        _moe_kernel,
        grid_spec=pltpu.PrefetchScalarGridSpec(
            num_scalar_prefetch=6, grid=(n_items,),
            in_specs=[rows, by_expert(D_MODEL, 2 * D_FF), by_expert(1, 2 * D_FF),
                      by_expert(D_FF, D_MODEL), by_expert(1, D_MODEL)],
            out_specs=rows,
            scratch_shapes=[pltpu.VMEM((TM_MOE, D_FF), BF16),
                            pltpu.VMEM((D_MODEL, 2 * D_FF), BF16),
                            pltpu.VMEM((D_FF, D_MODEL), BF16)]),
        out_shape=jax.ShapeDtypeStruct((a * CHUNKS, LANES), F32),
        compiler_params=pltpu.CompilerParams(dimension_semantics=("arbitrary",),
                                             vmem_limit_bytes=VMEM_LIMIT_MOE),
        name="moe",
    )(*sched, xs, w1, b1, w2, b2)


def _moe_schedule(counts, n_rows):
    n_tiles = n_rows // TM_MOE
    n_items = n_tiles + N_EXPERTS - 1
    pend = jnp.cumsum(counts)
    pstart = pend - counts
    first_tile = pstart // TM_MOE
    last_tile = jnp.maximum(pend - 1, 0) // TM_MOE
    ntile = jnp.where(counts > 0, last_tile - first_tile + 1, 0)
    wend = jnp.cumsum(ntile)
    wstart = wend - ntile
    total = wend[-1]
    w = jnp.arange(n_items, dtype=I32)
    wv = jnp.minimum(w, total - 1)
    ex = jnp.minimum(jnp.sum(wend[None, :] <= wv[:, None], 1), N_EXPERTS - 1).astype(I32)
    is_ex = ex[:, None] == jnp.arange(N_EXPERTS, dtype=I32)[None, :]
    of_ex = lambda per_expert: jnp.sum(jnp.where(is_ex, per_expert[None, :], 0), 1)
    tile = (of_ex(first_tile) + wv - of_ex(wstart)).astype(I32)
    valid = w < total
    lo = jnp.where(valid, jnp.clip(of_ex(pstart) - tile * TM_MOE, 0, TM_MOE), 0).astype(I32)
    hi = jnp.where(valid, jnp.clip(of_ex(pend) - tile * TM_MOE, 0, TM_MOE), 0).astype(I32)
    prev_tile = jnp.concatenate([jnp.full((1,), -1, I32), tile[:-1]])
    first = (valid & (tile != prev_tile)).astype(I32)
    prev_ex = jnp.concatenate([jnp.full((1,), -1, I32), ex[:-1]])
    newexp = (valid & (ex != prev_ex)).astype(I32)
    return (tile, ex, lo, hi, first, newexp), pstart


def _combine_kernel(dcur_ref, dnext_ref, c_ref, gate_ref, g2_ref, b2_ref, ys_hbm, ys_flat_hbm, *rest,
                    n_prompt):
    *outs, buf, sem = rest
    tm = c_ref.shape[0]
    i = pl.program_id(0)

    def gather_tile(dest_ref, slot):
        def gather_row(k, j, s, n):
            r = j * SUBLANES + s
            dst = buf.at[slot, k, pl.ds(pl.multiple_of(r * CHUNKS, CHUNKS), CHUNKS)]
            _row_copy(ys_hbm.at[dest_ref[0, k * tm + r]], dst, sem.at[slot]).start(priority=n % 2)

        _for_each_row_slot(tm // SUBLANES, gather_row)

    @pl.when(i == 0)
    def _():
        gather_tile(dcur_ref, 0)

    @pl.when(i + 1 < pl.num_programs(0))
    def _():
        gather_tile(dnext_ref, (i + 1) % 2)

    slot = i % 2
    for k in range(TOP_K):
        _row_copy(ys_flat_hbm.at[pl.ds(0, tm * CHUNKS)], buf.at[slot, k], sem.at[slot]).wait()
    gates = gate_ref[...]
    gate_cols = [jnp.broadcast_to(gates[:, k:k + 1], (tm, LANES)) for k in range(TOP_K)]
    parts = []
    for c in range(CHUNKS):
        part = c_ref[:, c * LANES:(c + 1) * LANES]
        for k in range(TOP_K):
            part = part + gate_cols[k] * buf[slot, k, pl.ds(c, tm, stride=CHUNKS), :]
        parts.append(part)
    acc = jnp.concatenate(parts, axis=1)
    y = _layer_norm(acc, g2_ref[...], b2_ref[...])
    if n_prompt is None:
        outs[0][...] = y
    else:
        @pl.when(i < n_prompt)
        def _():
            outs[0][...] = y

        @pl.when(i >= n_prompt)
        def _():
            outs[1][...] = y


def _combine(dest, c, gates_t, g2, b2, ys, t_prompt=None):
    t = c.shape[0]
    tm = TM_COMBINE
    n = t // tm
    vec = _full((1, D_MODEL))
    if t_prompt is None:
        n_p = None
        out_specs = pl.BlockSpec((tm, D_MODEL), lambda i: (i, 0))
        out_shape = jax.ShapeDtypeStruct((t, D_MODEL), F32)
    else:
        n_p = t_prompt // tm
        out_specs = _stream_specs(n_p, n - n_p, tm, D_MODEL)
        out_shape = [jax.ShapeDtypeStruct((t_prompt, D_MODEL), F32),
                     jax.ShapeDtypeStruct((t - t_prompt, D_MODEL), F32)]
    dest_spec = lambda f: pl.BlockSpec((None, 1, TOP_K * tm), lambda i: (f(i), 0, 0),
                                       memory_space=pltpu.SMEM)
    dest_tiles = _dest_tiles(dest, tm)
    return pl.pallas_call(
        functools.partial(_combine_kernel, n_prompt=n_p),
        grid=(n,),
        in_specs=[dest_spec(lambda i: i), dest_spec(lambda i: jnp.minimum(i + 1, n - 1)),
                  pl.BlockSpec((tm, D_MODEL), lambda i: (i, 0)),
                  pl.BlockSpec((tm, TOP_K), lambda i: (i, 0)),
                  vec, vec, pl.BlockSpec(memory_space=pl.ANY), pl.BlockSpec(memory_space=pl.ANY)],
        out_specs=out_specs,
        out_shape=out_shape,
        scratch_shapes=[pltpu.VMEM((2, TOP_K, tm * CHUNKS, LANES), F32),
                        pltpu.SemaphoreType.DMA((2,))],
        compiler_params=_cparams(("arbitrary",)), name="combine",
    )(dest_tiles, dest_tiles, c, gates_t, g2, b2, _token_tiles(ys), ys)


def kernel(x_prompt, x_sample, cache_k, cache_v, state_conv, p_prompt, p_sample, ln_emb_g, ln_emb_b,
           w_in, b_in, conv_w, sinks, w_o, b_o, ln1_g, ln1_b, w_gate, b_gate, w_ple, w_router,
           b_router, w1, b1, w2, b2, ln2_g, ln2_b):
    batch, seq, _ = x_prompt.shape
    nseq, dec_seq, _ = x_sample.shape
    t_p, t_s = batch * seq, nseq * dec_seq
    t = t_p + t_s
    w_cache = cache_k.shape[2]
    vec = lambda a: a.reshape(1, -1)

    x = (x_prompt.reshape(t_p, D_MODEL), x_sample.reshape(t_s, D_MODEL))
    state_rows = jnp.pad(state_conv, ((0, 0), (0, 0), (0, dec_seq - (CONV_K - 1)), (0, 0)))
    state_rows = state_rows.reshape(DEPTH, t_s, CONV_WIDTH)

    ks_p, vs_p, cs_p, ks_s, vs_s, cs_s = [], [], [], [], [], []
    for l in range(DEPTH):
        w_in_b = w_in[l].astype(BF16)
        xn, q, k, v, gb, u = _inproj(x, (vec(ln_emb_g), vec(ln_emb_b)), w_in_b, vec(b_in[l]))

        mix_p, nk_p, nv_p, nu_p = _prompt_mixer(sinks[l], q, k, v, u, gb, conv_w[l], batch, seq)
        mix_s, nk_s, nv_s = _sample_mixer(
            sinks[l], q, k, v, u, gb, conv_w[l], l,
            cache_k.reshape(DEPTH, nseq, w_cache, KV_WIDTH),
            cache_v.reshape(DEPTH, nseq, w_cache, KV_WIDTH), state_rows[l], t_p)

        wr_t = w_router[l].T
        wr_hi = wr_t.astype(BF16)
        wr_lo = (wr_t - wr_hi.astype(F32)).astype(BF16)
        x1, c, eid, gates, ranks, cnt = _post_mixer(
            mix_p, mix_s, xn, l, p_prompt.reshape(DEPTH, t_p, PLE_DIM),
            p_sample.reshape(DEPTH, t_s, PLE_DIM), w_o[l].astype(BF16), vec(b_o[l]), vec(ln1_g[l]), vec(ln1_b[l]),
            w_gate[l].astype(BF16), vec(b_gate[l]), w_ple[l].astype(BF16), wr_hi, wr_lo,
            b_router[l].reshape(N_EXPERTS, 1))

        counts = cnt[:, 0].astype(I32)
        sched, pstart = _moe_schedule(counts, t * TOP_K)
        onehot = eid[:, :, None] == jnp.arange(N_EXPERTS, dtype=I32)
        dest = ranks + jnp.sum(jnp.where(onehot, pstart, 0), -1)

        xs = _dispatch(dest, x1)
        ys = _moe(sched, xs, l, w1, b1[:, :, None, :], w2, b2[:, :, None, :])
        x = _combine(dest, c, gates.T, vec(ln2_g[l]), vec(ln2_b[l]), ys,
                     t_prompt=t_p if l == DEPTH - 1 else None)

        ks_p.append(nk_p.reshape(batch, WINDOW, N_KV_HEADS, HEAD_DIM))
        vs_p.append(nv_p.reshape(batch, WINDOW, N_KV_HEADS, HEAD_DIM))
        cs_p.append(nu_p[:, SUBLANES - (CONV_K - 1):])
        ks_s.append(nk_s.reshape(nseq, w_cache, N_KV_HEADS, HEAD_DIM))
        vs_s.append(nv_s.reshape(nseq, w_cache, N_KV_HEADS, HEAD_DIM))
        cs_s.append(u[t_p:].reshape(nseq, dec_seq, CONV_WIDTH)[:, dec_seq - (CONV_K - 1):])

    y_prompt = x[0].reshape(batch, seq, D_MODEL)
    y_sample = x[1].reshape(nseq, dec_seq, D_MODEL)
    return (y_prompt, y_sample, jnp.stack(ks_p), jnp.stack(vs_p), jnp.stack(cs_p),
            jnp.stack(ks_s), jnp.stack(vs_s), jnp.stack(cs_s))
```

```python
import functools

import jax
import jax.numpy as jnp
from jax import lax
from jax.experimental import pallas as pl
from jax.experimental.pallas import tpu as pltpu

F32 = jnp.float32
BF16 = jnp.bfloat16
I32 = jnp.int32

D_MODEL = 1024
DEPTH = 2
HEAD_DIM = 64
N_Q_HEADS = 8
N_KV_HEADS = 2
ATTN_WIDTH = N_Q_HEADS * HEAD_DIM
KV_WIDTH = N_KV_HEADS * HEAD_DIM
CONV_WIDTH = D_MODEL - ATTN_WIDTH
WINDOW = 128
ATTN_SCALE = HEAD_DIM ** -0.5
CONV_K = 3
N_EXPERTS = 32
TOP_K = 4
D_FF = D_MODEL
SWIGLU_LIMIT = 7.0
SWIGLU_ALPHA = 1.702
PLE_DIM = 256
LN_EPS = 1e-5
DN_ALPHA = (2.0 * DEPTH) ** 0.25
IN_COLS = ATTN_WIDTH + 2 * KV_WIDTH + 3 * CONV_WIDTH
Q0, K0, V0, GB0, GC0, H0 = 0, 512, 640, 768, 1280, 1792

LANES = 128
SUBLANES = 8
VMEM_LIMIT = 48 * 1024 * 1024
VMEM_LIMIT_MOE = 56 * 1024 * 1024

TM = 512
TM_IN = 1024
TQ = 1024
SEQ_GROUP = 8
TM_DISPATCH = 1024
TM_COMBINE = 256
TM_MOE = 512


def _layer_norm(x, g, b):
    mu = jnp.mean(x, -1, keepdims=True)
    xc = x - mu
    var = jnp.mean(xc * xc, -1, keepdims=True)
    return xc * lax.rsqrt(var + LN_EPS) * g + b


def _div_pow2(x, n):
    assert n & (n - 1) == 0
    return lax.shift_right_arithmetic(x, n.bit_length() - 1)


def _mod_pow2(x, n):
    assert n & (n - 1) == 0
    return x & (n - 1)


def _cparams(sem):
    return pltpu.CompilerParams(dimension_semantics=sem, vmem_limit_bytes=VMEM_LIMIT)


def _full(shape):
    return pl.BlockSpec(shape, lambda *_: (0,) * len(shape))


def _stream_specs(n_p, n_s, tm, width, layer=None):
    prompt_tile = lambda i: jnp.minimum(i, n_p - 1)
    sample_tile = lambda i: jnp.clip(i - n_p, 0, n_s - 1)
    if layer is None:
        return [pl.BlockSpec((tm, width), lambda i: (prompt_tile(i), 0)),
                pl.BlockSpec((tm, width), lambda i: (sample_tile(i), 0))]
    return [pl.BlockSpec((None, tm, width), lambda i: (layer, prompt_tile(i), 0)),
            pl.BlockSpec((None, tm, width), lambda i: (layer, sample_tile(i), 0))]


def _stream_tile(n_p, prompt_ref, sample_ref):
    return jnp.where(pl.program_id(0) < n_p, prompt_ref[...], sample_ref[...])


def _inproj_kernel(*refs, n_prompt):
    if n_prompt is not None:
        xp_ref, xs_ref, g_ref, b_ref, w_ref, bi_ref, xn_ref, q_ref, k_ref, v_ref, gb_ref, u_ref = refs
        x = _layer_norm(_stream_tile(n_prompt, xp_ref, xs_ref), g_ref[...], b_ref[...])
        xn_ref[...] = x
    else:
        x_ref, w_ref, bi_ref, q_ref, k_ref, v_ref, gb_ref, u_ref = refs
        x = x_ref[...]
    xb = x.astype(BF16)

    def proj(lo, hi):
        return jnp.dot(xb, w_ref[:, lo:hi], preferred_element_type=F32) + bi_ref[:, lo:hi]

    q_ref[...] = (proj(Q0, K0) * ATTN_SCALE).astype(BF16)
    k_ref[...] = proj(K0, V0)
    v_ref[...] = proj(V0, GB0)
    gb_ref[...] = proj(GB0, GC0)
    u_ref[...] = proj(GC0, H0) * proj(H0, IN_COLS)


def _inproj(x, ln, w_in_b, b_in):
    apply_ln = isinstance(x, tuple)
    row = lambda w: pl.BlockSpec((TM_IN, w), lambda i: (i, 0))
    n_p = None
    if apply_ln:
        n_p, n_s = x[0].shape[0] // TM_IN, x[1].shape[0] // TM_IN
        t = x[0].shape[0] + x[1].shape[0]
        in_specs = (_stream_specs(n_p, n_s, TM_IN, D_MODEL)
                    + [_full((1, D_MODEL)), _full((1, D_MODEL))])
        args = [x[0], x[1], ln[0], ln[1]]
    else:
        t = x.shape[0]
        in_specs = [row(D_MODEL)]
        args = [x]
    in_specs += [_full((D_MODEL, IN_COLS)), _full((1, IN_COLS))]
    args += [w_in_b, b_in]
    out_shape, out_specs = [], []
    if apply_ln:
        out_shape.append(jax.ShapeDtypeStruct((t, D_MODEL), F32))
        out_specs.append(row(D_MODEL))
    out_shape += [jax.ShapeDtypeStruct((t, ATTN_WIDTH), BF16),
                  jax.ShapeDtypeStruct((t, KV_WIDTH), F32),
                  jax.ShapeDtypeStruct((t, KV_WIDTH), F32),
                  jax.ShapeDtypeStruct((t, CONV_WIDTH), F32),
                  jax.ShapeDtypeStruct((t, CONV_WIDTH), F32)]
    out_specs += [row(ATTN_WIDTH), row(KV_WIDTH), row(KV_WIDTH), row(CONV_WIDTH), row(CONV_WIDTH)]
    outs = pl.pallas_call(
        functools.partial(_inproj_kernel, n_prompt=n_p),
        grid=(t // TM_IN,), in_specs=in_specs, out_specs=out_specs, out_shape=out_shape,
        compiler_params=_cparams(("arbitrary",)), name="inproj")(*args)
    if apply_ln:
        return outs
    return [x] + list(outs)


def _attend_column(qcol, kexp_b, vexp_b, mask, sink_a, sink_b, fold=None):
    lane = lax.broadcasted_iota(I32, qcol.shape, 1)
    outs = []
    for half, sink in ((0, sink_a), (1, sink_b)):
        keep = (lane < HEAD_DIM) if half == 0 else (lane >= HEAD_DIM)
        qm = jnp.where(keep, qcol, jnp.zeros_like(qcol))
        s = lax.dot_general(qm, kexp_b, (((1,), (1,)), ((), ())), preferred_element_type=F32)
        if fold is None:
            s = jnp.where(mask, s, -jnp.inf)
        else:
            upper, prev_ok = fold
            s_prev, s_cur = s[:, :WINDOW], s[:, WINDOW:]
            if prev_ok is not None:
                s_prev = jnp.where(prev_ok, s_prev, -jnp.inf)
            s = jnp.where(upper, s_prev, s_cur)
        m = jnp.maximum(jnp.max(s, -1, keepdims=True), sink)
        p = jnp.exp(s - m)
        denom = jnp.sum(p, -1, keepdims=True) + jnp.exp(sink - m)
        if fold is not None:
            zero = jnp.zeros_like(p)
            p = jnp.concatenate([jnp.where(upper, p, zero), jnp.where(upper, zero, p)], axis=1)
        o = jnp.dot(p.astype(BF16), vexp_b, preferred_element_type=F32)
        outs.append(o * (1.0 / denom))
    return jnp.where(lane < HEAD_DIM, outs[0], outs[1])


def _dup_heads(x):
    lane = lax.broadcasted_iota(I32, x.shape, 1)
    xr = pltpu.roll(x, HEAD_DIM, 1)
    low = lane < HEAD_DIM
    return (jnp.where(low, x, xr).astype(BF16), jnp.where(low, xr, x).astype(BF16))


def _prompt_mixer_kernel(sinks_ref, q_ref, kc_ref, kp_ref, vc_ref, vp_ref, uc_ref, up_ref,
                         gb_ref, cw_ref, o_ref, nk_ref, nv_ref, nu_ref):
    first = pl.program_id(1) == 0

    @pl.when(pl.program_id(1) == pl.num_programs(1) - 1)
    def _():
        nk_ref[...] = kc_ref[TQ - WINDOW:, :]
        nv_ref[...] = vc_ref[TQ - WINDOW:, :]
        nu_ref[...] = uc_ref[TQ - SUBLANES:, :]

    r_i = lax.broadcasted_iota(I32, (WINDOW, WINDOW), 0)
    c_i = lax.broadcasted_iota(I32, (WINDOW, WINDOW), 1)
    upper = c_i > r_i
    has_prev = jnp.logical_not(first)
    for s in range(TQ // WINDOW):
        rows = slice(s * WINDOW, (s + 1) * WINDOW)
        if s == 0:
            kprev, vprev = kp_ref[...], vp_ref[...]
        else:
            prev = slice((s - 1) * WINDOW, s * WINDOW)
            kprev, vprev = kc_ref[prev, :], vc_ref[prev, :]
        kk = jnp.concatenate([kprev, kc_ref[rows, :]], 0)
        vv = jnp.concatenate([vprev, vc_ref[rows, :]], 0)
        kexp = _dup_heads(kk)
        vexp = _dup_heads(vv)
        fold = (upper, has_prev if s == 0 else None)
        for col in range(ATTN_WIDTH // LANES):
            h = col // 2
            cols = slice(col * LANES, (col + 1) * LANES)
            out = _attend_column(q_ref[rows, cols], kexp[h], vexp[h], None,
                                 sinks_ref[2 * col], sinks_ref[2 * col + 1], fold=fold)
            o_ref[rows, cols] = out.astype(BF16)

    u = uc_ref[...]
    up = up_ref[...]
    zero = jnp.zeros((1, CONV_WIDTH), F32)
    p1 = jnp.where(first, zero, up[SUBLANES - 1:SUBLANES, :])
    p2 = jnp.where(first, zero, up[SUBLANES - 2:SUBLANES - 1, :])
    row = lax.broadcasted_iota(I32, u.shape, 0)
    u1 = jnp.where(row == 0, p1, pltpu.roll(u, 1, 0))
    u2 = jnp.where(row == 0, p2, jnp.where(row == 1, p1, pltpu.roll(u, 2, 0)))
    cw = cw_ref[...]
    y = u2 * cw[0:1, :] + u1 * cw[1:2, :] + u * cw[2:3, :]
    o_ref[:, ATTN_WIDTH:] = (gb_ref[...] * y).astype(BF16)


def _prompt_mixer(sinks, q, k, v, u, gb, conv_w, batch, seq):
    nj = seq // TQ
    tile = lambda b, j: b * nj + j
    cur = lambda w: pl.BlockSpec((TQ, w), lambda b, j: (tile(b, j), 0))
    prev_kv = pl.BlockSpec((WINDOW, KV_WIDTH),
                           lambda b, j: (jnp.maximum(tile(b, j) * (TQ // WINDOW) - 1, 0), 0))
    prev_u = pl.BlockSpec((SUBLANES, CONV_WIDTH),
                          lambda b, j: (jnp.maximum(tile(b, j) * (TQ // SUBLANES) - 1, 0), 0))
    return pl.pallas_call(
        _prompt_mixer_kernel,
        grid=(batch, nj),
        in_specs=[pl.BlockSpec(memory_space=pltpu.SMEM),
                  cur(ATTN_WIDTH), cur(KV_WIDTH), prev_kv, cur(KV_WIDTH), prev_kv,
                  cur(CONV_WIDTH), prev_u, cur(CONV_WIDTH), _full((CONV_K, CONV_WIDTH))],
        out_specs=[cur(D_MODEL),
                   pl.BlockSpec((None, WINDOW, KV_WIDTH), lambda b, j: (b, 0, 0)),
                   pl.BlockSpec((None, WINDOW, KV_WIDTH), lambda b, j: (b, 0, 0)),
                   pl.BlockSpec((None, SUBLANES, CONV_WIDTH), lambda b, j: (b, 0, 0))],
        out_shape=[jax.ShapeDtypeStruct((batch * seq, D_MODEL), BF16),
                   jax.ShapeDtypeStruct((batch, WINDOW, KV_WIDTH), F32),
                   jax.ShapeDtypeStruct((batch, WINDOW, KV_WIDTH), F32),
                   jax.ShapeDtypeStruct((batch, SUBLANES, CONV_WIDTH), F32)],
        compiler_params=_cparams(("arbitrary", "arbitrary")), name="prompt_mixer",
    )(sinks, q, k, k, v, v, u, u, gb, conv_w)


def _sample_mixer_kernel(sinks_ref, q_ref, kn_ref, vn_ref, kb_ref, vb_ref, u_ref, st_ref, gb_ref,
                         cw_ref, o_ref, nk_ref, nv_ref, *, dec_seq):
    g, w = kb_ref.shape[0], kb_ref.shape[1]
    rows = g * dec_seq
    n_cache = g * w
    n_keys = n_cache + 2 * rows
    kn, vn = kn_ref[...], vn_ref[...]
    pad = jnp.zeros((rows, KV_WIDTH), F32)
    kk = jnp.concatenate([kb_ref[...].reshape(n_cache, KV_WIDTH), kn, pad], 0)
    vv = jnp.concatenate([vb_ref[...].reshape(n_cache, KV_WIDTH), vn, pad], 0)
    kexp = _dup_heads(kk)
    vexp = _dup_heads(vv)

    r_i = lax.broadcasted_iota(I32, (rows, n_keys), 0)
    c_i = lax.broadcasted_iota(I32, (rows, n_keys), 1)
    r_seq, r_pos = _div_pow2(r_i, dec_seq), _mod_pow2(r_i, dec_seq)
    c_new = c_i - n_cache
    in_cache = ((c_i < n_cache) & (_div_pow2(c_i, w) == r_seq)
                & (_mod_pow2(c_i, w) > r_pos + (w - WINDOW)))
    in_new = ((c_new >= 0) & (c_new < rows) & (_div_pow2(c_new, dec_seq) == r_seq)
              & (_mod_pow2(c_new, dec_seq) <= r_pos))
    mask = in_cache | in_new
    for col in range(ATTN_WIDTH // LANES):
        h = col // 2
        cols = slice(col * LANES, (col + 1) * LANES)
        out = _attend_column(q_ref[:, cols], kexp[h], vexp[h], mask,
                             sinks_ref[2 * col], sinks_ref[2 * col + 1])
        o_ref[:, cols] = out.astype(BF16)

    nk_ref[:, 0:w - dec_seq, :] = kb_ref[:, dec_seq:w, :]
    nk_ref[:, w - dec_seq:w, :] = kn.reshape(g, dec_seq, KV_WIDTH)
    nv_ref[:, 0:w - dec_seq, :] = vb_ref[:, dec_seq:w, :]
    nv_ref[:, w - dec_seq:w, :] = vn.reshape(g, dec_seq, KV_WIDTH)

    u = u_ref[...]
    st = st_ref[...]
    pos = lax.broadcasted_iota(I32, u.shape, 0) % dec_seq
    u1 = jnp.where(pos == 0, pltpu.roll(st, rows - 1, 0), pltpu.roll(u, 1, 0))
    u2 = jnp.where(pos < 2, st, pltpu.roll(u, 2, 0))
    cw = cw_ref[...]
    y = u2 * cw[0:1, :] + u1 * cw[1:2, :] + u * cw[2:3, :]
    o_ref[:, ATTN_WIDTH:] = (gb_ref[...] * y).astype(BF16)


def _sample_mixer(sinks, q, k, v, u, gb, conv_w, layer, cache_k, cache_v, state_rows, t_prompt):
    nseq, w = cache_k.shape[1], cache_k.shape[2]
    dec_seq = (q.shape[0] - t_prompt) // nseq
    rows = SEQ_GROUP * dec_seq
    off = t_prompt // rows
    tok = lambda wd: pl.BlockSpec((rows, wd), lambda i: (off + i, 0))
    local = lambda wd: pl.BlockSpec((rows, wd), lambda i: (i, 0))
    cache_in = pl.BlockSpec((None, SEQ_GROUP, w, KV_WIDTH), lambda i: (layer, i, 0, 0))
    cache = pl.BlockSpec((SEQ_GROUP, w, KV_WIDTH), lambda i: (i, 0, 0))
    return pl.pallas_call(
        functools.partial(_sample_mixer_kernel, dec_seq=dec_seq),
        grid=(nseq // SEQ_GROUP,),
        in_specs=[pl.BlockSpec(memory_space=pltpu.SMEM),
                  tok(ATTN_WIDTH), tok(KV_WIDTH), tok(KV_WIDTH), cache_in, cache_in,
                  tok(CONV_WIDTH), local(CONV_WIDTH), tok(CONV_WIDTH),
                  _full((CONV_K, CONV_WIDTH))],
        out_specs=[local(D_MODEL), cache, cache],
        out_shape=[jax.ShapeDtypeStruct((nseq * dec_seq, D_MODEL), BF16),
                   jax.ShapeDtypeStruct((nseq, w, KV_WIDTH), F32),
                   jax.ShapeDtypeStruct((nseq, w, KV_WIDTH), F32)],
        compiler_params=_cparams(("parallel",)), name="sample_mixer",
    )(sinks, q, k, v, cache_k, cache_v, u, state_rows, gb, conv_w)


def _post_mixer_kernel(mixp_ref, mixs_ref, xn_ref, pp_ref, ps_ref, wo_ref, bo_ref, g1_ref, b1_ref,
                       wg_ref, bg_ref, wp_ref, wrh_ref, wrl_ref, br_ref,
                       x1_ref, c_ref, eid_ref, gate_ref, rank_ref, cnt_ref, carry_ref, *, n_prompt):
    @pl.when(pl.program_id(0) == 0)
    def _():
        carry_ref[...] = jnp.zeros_like(carry_ref)

    mixed = _stream_tile(n_prompt, mixp_ref, mixs_ref)
    mix = jnp.dot(mixed, wo_ref[...], preferred_element_type=F32) + bo_ref[...]
    x1 = _layer_norm(DN_ALPHA * xn_ref[...] + mix, g1_ref[...], b1_ref[...])
    for c in range(CHUNKS):
        x1_ref[pl.ds(c, x1.shape[0], stride=CHUNKS), :] = x1[:, c * LANES:(c + 1) * LANES]
    x1h = x1.astype(BF16)
    x1l = (x1 - x1h.astype(F32)).astype(BF16)
    gate = jax.nn.sigmoid(jnp.dot(x1h, wg_ref[...], preferred_element_type=F32) + bg_ref[...])
    p_tile = _stream_tile(n_prompt, pp_ref, ps_ref).astype(BF16)
    ple = jnp.dot(p_tile, wp_ref[...], preferred_element_type=F32)
    c_ref[...] = DN_ALPHA * x1 + gate * ple

    nt = (((1,), (1,)), ((), ()))
    logits = (lax.dot_general(wrh_ref[...], x1h, nt, preferred_element_type=F32)
              + lax.dot_general(wrh_ref[...], x1l, nt, preferred_element_type=F32)
              + lax.dot_general(wrl_ref[...], x1h, nt, preferred_element_type=F32)
              + br_ref[...])
    tm = logits.shape[1]
    e_i = lax.broadcasted_iota(I32, logits.shape, 0).astype(F32)
    work = logits
    vals, sels = [], []
    for k in range(TOP_K):
        m = jnp.max(work, 0, keepdims=True)
        idx = jnp.min(jnp.where(work == m, e_i, float(N_EXPERTS)), 0, keepdims=True)
        sel = e_i == idx
        vals.append(m)
        sels.append(sel)
        eid_ref[k:k + 1, :] = idx.astype(I32)
        work = jnp.where(sel, -jnp.inf, work)
    exps = [jnp.exp(v - vals[0]) for v in vals]
    denom = exps[0] + exps[1] + exps[2] + exps[3]
    for k in range(TOP_K):
        gate_ref[k:k + 1, :] = exps[k] / denom

    chosen = jnp.where(sels[0] | sels[1] | sels[2] | sels[3], 1.0, 0.0)
    s_i = lax.broadcasted_iota(I32, (tm, tm), 0)
    t_i = lax.broadcasted_iota(I32, (tm, tm), 1)
    before = jnp.where(s_i < t_i, 1.0, 0.0).astype(BF16)
    pos = carry_ref[:, 0:1] + jnp.dot(chosen.astype(BF16), before, preferred_element_type=F32)
    for k in range(TOP_K):
        rank_ref[k:k + 1, :] = jnp.sum(jnp.where(sels[k], pos, 0.0), 0, keepdims=True).astype(I32)
    carry = carry_ref[...] + jnp.sum(chosen, 1, keepdims=True)
    carry_ref[...] = carry
    cnt_ref[...] = carry


def _post_mixer(mix_p, mix_s, xn, layer, p_p, p_s, w_o, b_o, g1, b1, w_gate, b_gate, w_ple, wr_hi,
                wr_lo, b_r):
    t = xn.shape[0]
    n_p, n_s = mix_p.shape[0] // TM, mix_s.shape[0] // TM
    row = lambda w: pl.BlockSpec((TM, w), lambda i: (i, 0))
    meta = pl.BlockSpec((TOP_K, TM), lambda i: (0, i))
    vec = _full((1, D_MODEL))
    return pl.pallas_call(
        functools.partial(_post_mixer_kernel, n_prompt=n_p),
        grid=(t // TM,),
        in_specs=_stream_specs(n_p, n_s, TM, D_MODEL) + [row(D_MODEL)]
        + _stream_specs(n_p, n_s, TM, PLE_DIM, layer)
        + [_full((D_MODEL, D_MODEL)), vec, vec, vec,
           _full((D_MODEL, D_MODEL)), vec, _full((PLE_DIM, D_MODEL)),
           _full((N_EXPERTS, D_MODEL)), _full((N_EXPERTS, D_MODEL)), _full((N_EXPERTS, 1))],
        out_specs=[pl.BlockSpec((TM * CHUNKS, LANES), lambda i: (i, 0)), row(D_MODEL),
                   meta, meta, meta, _full((N_EXPERTS, LANES))],
        out_shape=[jax.ShapeDtypeStruct((t * CHUNKS, LANES), F32),
                   jax.ShapeDtypeStruct((t, D_MODEL), F32),
                   jax.ShapeDtypeStruct((TOP_K, t), I32),
                   jax.ShapeDtypeStruct((TOP_K, t), F32),
                   jax.ShapeDtypeStruct((TOP_K, t), I32),
                   jax.ShapeDtypeStruct((N_EXPERTS, LANES), F32)],
        scratch_shapes=[pltpu.VMEM((N_EXPERTS, LANES), F32)],
        compiler_params=_cparams(("arbitrary",)), name="post_mixer",
    )(mix_p, mix_s, xn, p_p, p_s, w_o, b_o, g1, b1, w_gate, b_gate, w_ple, wr_hi, wr_lo, b_r)


def _row_copy(src, dst, sem):
    return pltpu.make_async_copy(src, dst, sem)


CHUNKS = D_MODEL // LANES
assert CHUNKS == SUBLANES


def _load_token_tiles(ref, r0, n):
    return jnp.concatenate(
        [ref[pl.ds(r0 * CHUNKS + c, n, stride=CHUNKS), :] for c in range(CHUNKS)], axis=1)


def _token_tiles(x):
    if x.ndim == 2:
        return x.reshape(x.shape[0] // CHUNKS, CHUNKS, LANES)
    return x.reshape(x.shape[0] * CHUNKS, LANES)


def _for_each_row_slot(groups, fn):
    def group(j, carry):
        for s in range(SUBLANES):
            for k in range(TOP_K):
                fn(k, j, s, s * TOP_K + k)
        return carry

    lax.fori_loop(0, groups, group, 0)


def _dispatch_kernel(dest_ref, x_ref, xs_hbm, sem):
    tm = x_ref.shape[0]

    def scatter_row(k, j, s, n):
        r = j * SUBLANES + s
        _row_copy(x_ref.at[r], xs_hbm.at[dest_ref[0, k * tm + r]], sem).start(priority=n % 2)

    _for_each_row_slot(tm // SUBLANES, scatter_row)
    for k in range(TOP_K):
        _row_copy(x_ref, xs_hbm.at[pl.ds(0, tm)], sem).wait()


def _dest_tiles(dest, tm):
    t = dest.shape[1]
    return dest.reshape(TOP_K, t // tm, tm).transpose(1, 0, 2).reshape(t // tm, 1, TOP_K * tm)


def _dispatch(dest, x1t):
    t = x1t.shape[0] // CHUNKS
    tm = TM_DISPATCH
    xs = pl.pallas_call(
        _dispatch_kernel,
        grid=(t // tm,),
        in_specs=[pl.BlockSpec((None, 1, TOP_K * tm), lambda i: (i, 0, 0), memory_space=pltpu.SMEM),
                  pl.BlockSpec((tm, CHUNKS, LANES), lambda i: (i, 0, 0))],
        out_specs=pl.BlockSpec(memory_space=pl.ANY),
        out_shape=jax.ShapeDtypeStruct((t * TOP_K, CHUNKS, LANES), F32),
        scratch_shapes=[pltpu.SemaphoreType.DMA],
        compiler_params=_cparams(("arbitrary",)), name="dispatch",
    )(_dest_tiles(dest, tm), _token_tiles(x1t))
    return _token_tiles(xs)


def _moe_kernel(tile_ref, exp_ref, lo_ref, hi_ref, first_ref, newexp_ref,
                xs_ref, w1_ref, b1_ref, w2_ref, b2_ref, ys_ref, act_ref, w1b_ref, w2b_ref):
    del tile_ref, exp_ref
    w = pl.program_id(0)
    lo, hi = lo_ref[w], hi_ref[w]

    @pl.when(newexp_ref[w] == 1)
    def _():
        w1b_ref[...] = w1_ref[...].astype(BF16)
        r_i = lax.broadcasted_iota(I32, (LANES, LANES), 0)
        c_i = lax.broadcasted_iota(I32, (LANES, LANES), 1)
        src = (LANES // 2) * (r_i & 1) + lax.shift_right_logical(r_i, 1)
        perm = jnp.where(c_i == src, 1.0, 0.0).astype(BF16)
        for m in range(D_FF // LANES):
            blk = slice(m * LANES, (m + 1) * LANES)
            w2b_ref[blk, :] = jnp.dot(perm, w2_ref[blk, :].astype(BF16),
                                      preferred_element_type=F32).astype(BF16)

    def expert_mlp(r0, nrows, first_visit):
        rows = slice(r0, r0 + nrows)
        xb = _load_token_tiles(xs_ref, r0, nrows).astype(BF16)
        even = (lax.broadcasted_iota(I32, (nrows, LANES), 1) & 1) == 0
        for m in range(D_FF // LANES):
            cols = slice(2 * m * LANES, 2 * (m + 1) * LANES)
            h = jnp.dot(xb, w1b_ref[:, cols], preferred_element_type=F32) + b1_ref[:, cols]
            ha, hb = h[:, :LANES], h[:, LANES:]
            glu = jnp.where(even, ha, pltpu.roll(hb, 1, 1))
            lin = jnp.where(even, pltpu.roll(ha, LANES - 1, 1), hb)
            glu = jnp.minimum(glu, SWIGLU_LIMIT)
            lin = jnp.clip(lin, -SWIGLU_LIMIT, SWIGLU_LIMIT)
            act = glu * jax.nn.sigmoid(SWIGLU_ALPHA * glu) * (lin + 1.0)
            act_ref[rows, m * LANES:(m + 1) * LANES] = act.astype(BF16)
        r = r0 + lax.broadcasted_iota(I32, (nrows, LANES), 0)
        mine = (r >= lo) & (r < hi)
        if first_visit and nrows < TM_MOE:
            ys_ref[pl.ds((nrows - r0) * CHUNKS, (TM_MOE - nrows) * CHUNKS), :] = jnp.zeros(
                ((TM_MOE - nrows) * CHUNKS, LANES), F32)
        width = 2 * LANES
        for jb in range(D_MODEL // width):
            cols = slice(jb * width, (jb + 1) * width)
            y = jnp.dot(act_ref[rows, :], w2b_ref[:, cols], preferred_element_type=F32) + b2_ref[:, cols]
            for cc in range(width // LANES):
                at = pl.ds(r0 * CHUNKS + jb * (width // LANES) + cc, nrows, stride=CHUNKS)
                old = 0.0 if first_visit else ys_ref[at, :]
                ys_ref[at, :] = jnp.where(mine, y[:, cc * LANES:(cc + 1) * LANES], old)

    half = TM_MOE // 2
    some = hi > lo
    for first_visit in (True, False):
        visit = some & (first_ref[w] == (1 if first_visit else 0))

        @pl.when(visit & (hi <= half))
        def _():
            expert_mlp(0, half, first_visit)

        @pl.when(visit & (lo >= half))
        def _():
            expert_mlp(half, half, first_visit)

        @pl.when(visit & (lo < half) & (hi > half))
        def _():
            expert_mlp(0, TM_MOE, first_visit)


def _moe(sched, xs, layer, w1, b1, w2, b2):
    a = xs.shape[0] // CHUNKS
    n_items = sched[0].shape[0]
    by_expert = lambda r, c: pl.BlockSpec((None, None, r, c),
                                          lambda w, tile, ex, *_: (layer, ex[w], 0, 0))
    rows = pl.BlockSpec((TM_MOE * CHUNKS, LANES), lambda w, tile, *_: (tile[w], 0))
    return pl.pallas_call(
        _moe_kernel,
        grid_spec=pltpu.PrefetchScalarGridSpec(
            num_scalar_prefetch=6, grid=(n_items,),
            in_specs=[rows, by_expert(D_MODEL, 2 * D_FF), by_expert(1, 2 * D_FF),
                      by_expert(D_FF, D_MODEL), by_expert(1, D_MODEL)],
            out_specs=rows,
            scratch_shapes=[pltpu.VMEM((TM_MOE, D_FF), BF16),
                            pltpu.VMEM((D_MODEL, 2 * D_FF), BF16),
                            pltpu.VMEM((D_FF, D_MODEL), BF16)]),
        out_shape=jax.ShapeDtypeStruct((a * CHUNKS, LANES), F32),
        compiler_params=pltpu.CompilerParams(dimension_semantics=("arbitrary",),
                                             vmem_limit_bytes=VMEM_LIMIT_MOE),
        name="moe",
    )(*sched, xs, w1, b1, w2, b2)


def _moe_schedule(counts, n_rows):
    n_tiles = n_rows // TM_MOE
    n_items = n_tiles + N_EXPERTS - 1
    pend = jnp.cumsum(counts)
    pstart = pend - counts
    first_tile = pstart // TM_MOE
    last_tile = jnp.maximum(pend - 1, 0) // TM_MOE
    ntile = jnp.where(counts > 0, last_tile - first_tile + 1, 0)
    wend = jnp.cumsum(ntile)
    wstart = wend - ntile
    total = wend[-1]
    w = jnp.arange(n_items, dtype=I32)
    wv = jnp.minimum(w, total - 1)
    ex = jnp.minimum(jnp.sum(wend[None, :] <= wv[:, None], 1), N_EXPERTS - 1).astype(I32)
    is_ex = ex[:, None] == jnp.arange(N_EXPERTS, dtype=I32)[None, :]
    of_ex = lambda per_expert: jnp.sum(jnp.where(is_ex, per_expert[None, :], 0), 1)
    tile = (of_ex(first_tile) + wv - of_ex(wstart)).astype(I32)
    valid = w < total
    lo = jnp.where(valid, jnp.clip(of_ex(pstart) - tile * TM_MOE, 0, TM_MOE), 0).astype(I32)
    hi = jnp.where(valid, jnp.clip(of_ex(pend) - tile * TM_MOE, 0, TM_MOE), 0).astype(I32)
    prev_tile = jnp.concatenate([jnp.full((1,), -1, I32), tile[:-1]])
    first = (valid & (tile != prev_tile)).astype(I32)
    prev_ex = jnp.concatenate([jnp.full((1,), -1, I32), ex[:-1]])
    newexp = (valid & (ex != prev_ex)).astype(I32)
    return (tile, ex, lo, hi, first, newexp), pstart


def _combine_kernel(dcur_ref, dnext_ref, c_ref, gate_ref, g2_ref, b2_ref, ys_hbm, ys_flat_hbm, *rest,
                    n_prompt):
    *outs, buf, sem = rest
    tm = c_ref.shape[0]
    i = pl.program_id(0)

    def gather_tile(dest_ref, slot):
        def gather_row(k, j, s, n):
            r = j * SUBLANES + s
            dst = buf.at[slot, k, pl.ds(pl.multiple_of(r * CHUNKS, CHUNKS), CHUNKS)]
            _row_copy(ys_hbm.at[dest_ref[0, k * tm + r]], dst, sem.at[slot]).start(priority=n % 2)

        _for_each_row_slot(tm // SUBLANES, gather_row)

    @pl.when(i == 0)
    def _():
        gather_tile(dcur_ref, 0)

    @pl.when(i + 1 < pl.num_programs(0))
    def _():
        gather_tile(dnext_ref, (i + 1) % 2)

    slot = i % 2
    for k in range(TOP_K):
        _row_copy(ys_flat_hbm.at[pl.ds(0, tm * CHUNKS)], buf.at[slot, k], sem.at[slot]).wait()
    gates = gate_ref[...]
    gate_cols = [jnp.broadcast_to(gates[:, k:k + 1], (tm, LANES)) for k in range(TOP_K)]
    parts = []
    for c in range(CHUNKS):
        part = c_ref[:, c * LANES:(c + 1) * LANES]
        for k in range(TOP_K):
            part = part + gate_cols[k] * buf[slot, k, pl.ds(c, tm, stride=CHUNKS), :]
        parts.append(part)
    acc = jnp.concatenate(parts, axis=1)
    y = _layer_norm(acc, g2_ref[...], b2_ref[...])
    if n_prompt is None:
        outs[0][...] = y
    else:
        @pl.when(i < n_prompt)
        def _():
            outs[0][...] = y

        @pl.when(i >= n_prompt)
        def _():
            outs[1][...] = y


def _combine(dest, c, gates_t, g2, b2, ys, t_prompt=None):
    t = c.shape[0]
    tm = TM_COMBINE
    n = t // tm
    vec = _full((1, D_MODEL))
    if t_prompt is None:
        n_p = None
        out_specs = pl.BlockSpec((tm, D_MODEL), lambda i: (i, 0))
        out_shape = jax.ShapeDtypeStruct((t, D_MODEL), F32)
    else:
        n_p = t_prompt // tm
        out_specs = _stream_specs(n_p, n - n_p, tm, D_MODEL)
        out_shape = [jax.ShapeDtypeStruct((t_prompt, D_MODEL), F32),
                     jax.ShapeDtypeStruct((t - t_prompt, D_MODEL), F32)]
    dest_spec = lambda f: pl.BlockSpec((None, 1, TOP_K * tm), lambda i: (f(i), 0, 0),
                                       memory_space=pltpu.SMEM)
    dest_tiles = _dest_tiles(dest, tm)
    return pl.pallas_call(
        functools.partial(_combine_kernel, n_prompt=n_p),
        grid=(n,),
        in_specs=[dest_spec(lambda i: i), dest_spec(lambda i: jnp.minimum(i + 1, n - 1)),
                  pl.BlockSpec((tm, D_MODEL), lambda i: (i, 0)),
                  pl.BlockSpec((tm, TOP_K), lambda i: (i, 0)),
                  vec, vec, pl.BlockSpec(memory_space=pl.ANY), pl.BlockSpec(memory_space=pl.ANY)],
        out_specs=out_specs,
        out_shape=out_shape,
        scratch_shapes=[pltpu.VMEM((2, TOP_K, tm * CHUNKS, LANES), F32),
                        pltpu.SemaphoreType.DMA((2,))],
        compiler_params=_cparams(("arbitrary",)), name="combine",
    )(dest_tiles, dest_tiles, c, gates_t, g2, b2, _token_tiles(ys), ys)


def kernel(x_prompt, x_sample, cache_k, cache_v, state_conv, p_prompt, p_sample, ln_emb_g, ln_emb_b,
           w_in, b_in, conv_w, sinks, w_o, b_o, ln1_g, ln1_b, w_gate, b_gate, w_ple, w_router,
           b_router, w1, b1, w2, b2, ln2_g, ln2_b):
    batch, seq, _ = x_prompt.shape
    nseq, dec_seq, _ = x_sample.shape
    t_p, t_s = batch * seq, nseq * dec_seq
    t = t_p + t_s
    w_cache = cache_k.shape[2]
    vec = lambda a: a.reshape(1, -1)

    x = (x_prompt.reshape(t_p, D_MODEL), x_sample.reshape(t_s, D_MODEL))
    state_rows = jnp.pad(state_conv, ((0, 0), (0, 0), (0, dec_seq - (CONV_K - 1)), (0, 0)))
    state_rows = state_rows.reshape(DEPTH, t_s, CONV_WIDTH)

    ks_p, vs_p, cs_p, ks_s, vs_s, cs_s = [], [], [], [], [], []
    for l in range(DEPTH):
        w_in_b = w_in[l].astype(BF16)
        xn, q, k, v, gb, u = _inproj(x, (vec(ln_emb_g), vec(ln_emb_b)), w_in_b, vec(b_in[l]))

        mix_p, nk_p, nv_p, nu_p = _prompt_mixer(sinks[l], q, k, v, u, gb, conv_w[l], batch, seq)
        mix_s, nk_s, nv_s = _sample_mixer(
            sinks[l], q, k, v, u, gb, conv_w[l], l,
            cache_k.reshape(DEPTH, nseq, w_cache, KV_WIDTH),
            cache_v.reshape(DEPTH, nseq, w_cache, KV_WIDTH), state_rows[l], t_p)

        wr_t = w_router[l].T
        wr_hi = wr_t.astype(BF16)
        wr_lo = (wr_t - wr_hi.astype(F32)).astype(BF16)
        x1, c, eid, gates, ranks, cnt = _post_mixer(
            mix_p, mix_s, xn, l, p_prompt.reshape(DEPTH, t_p, PLE_DIM),
            p_sample.reshape(DEPTH, t_s, PLE_DIM), w_o[l].astype(BF16), vec(b_o[l]), vec(ln1_g[l]), vec(ln1_b[l]),
            w_gate[l].astype(BF16), vec(b_gate[l]), w_ple[l].astype(BF16), wr_hi, wr_lo,
            b_router[l].reshape(N_EXPERTS, 1))

        counts = cnt[:, 0].astype(I32)
        sched, pstart = _moe_schedule(counts, t * TOP_K)
        onehot = eid[:, :, None] == jnp.arange(N_EXPERTS, dtype=I32)
        dest = ranks + jnp.sum(jnp.where(onehot, pstart, 0), -1)

        xs = _dispatch(dest, x1)
        ys = _moe(sched, xs, l, w1, b1[:, :, None, :], w2, b2[:, :, None, :])
        x = _combine(dest, c, gates.T, vec(ln2_g[l]), vec(ln2_b[l]), ys,
                     t_prompt=t_p if l == DEPTH - 1 else None)

        ks_p.append(nk_p.reshape(batch, WINDOW, N_KV_HEADS, HEAD_DIM))
        vs_p.append(nv_p.reshape(batch, WINDOW, N_KV_HEADS, HEAD_DIM))
        cs_p.append(nu_p[:, SUBLANES - (CONV_K - 1):])
        ks_s.append(nk_s.reshape(nseq, w_cache, N_KV_HEADS, HEAD_DIM))
        vs_s.append(nv_s.reshape(nseq, w_cache, N_KV_HEADS, HEAD_DIM))
        cs_s.append(u[t_p:].reshape(nseq, dec_seq, CONV_WIDTH)[:, dec_seq - (CONV_K - 1):])

    y_prompt = x[0].reshape(batch, seq, D_MODEL)
    y_sample = x[1].reshape(nseq, dec_seq, D_MODEL)
    return (y_prompt, y_sample, jnp.stack(ks_p), jnp.stack(vs_p), jnp.stack(cs_p),
            jnp.stack(ks_s), jnp.stack(vs_s), jnp.stack(cs_s))
```

```python
import functools

import jax
import jax.numpy as jnp
from jax import lax
from jax.experimental import pallas as pl
from jax.experimental.pallas import tpu as pltpu

F32 = jnp.float32
BF16 = jnp.bfloat16
I32 = jnp.int32

D_MODEL = 1024
DEPTH = 2
HEAD_DIM = 64
N_Q_HEADS = 8
N_KV_HEADS = 2
ATTN_WIDTH = N_Q_HEADS * HEAD_DIM
KV_WIDTH = N_KV_HEADS * HEAD_DIM
CONV_WIDTH = D_MODEL - ATTN_WIDTH
WINDOW = 128
ATTN_SCALE = HEAD_DIM ** -0.5
CONV_K = 3
N_EXPERTS = 32
TOP_K = 4
D_FF = D_MODEL
SWIGLU_LIMIT = 7.0
SWIGLU_ALPHA = 1.702
PLE_DIM = 256
LN_EPS = 1e-5
DN_ALPHA = (2.0 * DEPTH) ** 0.25
IN_COLS = ATTN_WIDTH + 2 * KV_WIDTH + 3 * CONV_WIDTH
Q0, K0, V0, GB0, GC0, H0 = 0, 512, 640, 768, 1280, 1792

LANES = 128
SUBLANES = 8
VMEM_LIMIT = 48 * 1024 * 1024
VMEM_LIMIT_MOE = 56 * 1024 * 1024

TM = 512
TM_IN = 1024
TQ = 1024
SEQ_GROUP = 8
TM_DISPATCH = 1024
TM_COMBINE = 256
TM_MOE = 512


def _layer_norm(x, g, b):
    mu = jnp.mean(x, -1, keepdims=True)
    xc = x - mu
    var = jnp.mean(xc * xc, -1, keepdims=True)
    return xc * lax.rsqrt(var + LN_EPS) * g + b


def _div_pow2(x, n):
    assert n & (n - 1) == 0
    return lax.shift_right_arithmetic(x, n.bit_length() - 1)


def _mod_pow2(x, n):
    assert n & (n - 1) == 0
    return x & (n - 1)


def _cparams(sem):
    return pltpu.CompilerParams(dimension_semantics=sem, vmem_limit_bytes=VMEM_LIMIT)


def _full(shape):
    return pl.BlockSpec(shape, lambda *_: (0,) * len(shape))


def _stream_specs(n_p, n_s, tm, width, layer=None):
    prompt_tile = lambda i: jnp.minimum(i, n_p - 1)
    sample_tile = lambda i: jnp.clip(i - n_p, 0, n_s - 1)
    if layer is None:
        return [pl.BlockSpec((tm, width), lambda i: (prompt_tile(i), 0)),
                pl.BlockSpec((tm, width), lambda i: (sample_tile(i), 0))]
    return [pl.BlockSpec((None, tm, width), lambda i: (layer, prompt_tile(i), 0)),
            pl.BlockSpec((None, tm, width), lambda i: (layer, sample_tile(i), 0))]


def _stream_tile(n_p, prompt_ref, sample_ref):
    return jnp.where(pl.program_id(0) < n_p, prompt_ref[...], sample_ref[...])


def _inproj_kernel(*refs, n_prompt):
    if n_prompt is not None:
        xp_ref, xs_ref, g_ref, b_ref, w_ref, bi_ref, xn_ref, q_ref, k_ref, v_ref, gb_ref, u_ref = refs
        x = _layer_norm(_stream_tile(n_prompt, xp_ref, xs_ref), g_ref[...], b_ref[...])
        xn_ref[...] = x
    else:
        x_ref, w_ref, bi_ref, q_ref, k_ref, v_ref, gb_ref, u_ref = refs
        x = x_ref[...]
    xb = x.astype(BF16)

    def proj(lo, hi):
        return jnp.dot(xb, w_ref[:, lo:hi], preferred_element_type=F32) + bi_ref[:, lo:hi]

    q_ref[...] = (proj(Q0, K0) * ATTN_SCALE).astype(BF16)
    k_ref[...] = proj(K0, V0)
    v_ref[...] = proj(V0, GB0)
    gb_ref[...] = proj(GB0, GC0)
    u_ref[...] = proj(GC0, H0) * proj(H0, IN_COLS)


def _inproj(x, ln, w_in_b, b_in):
    apply_ln = isinstance(x, tuple)
    row = lambda w: pl.BlockSpec((TM_IN, w), lambda i: (i, 0))
    n_p = None
    if apply_ln:
        n_p, n_s = x[0].shape[0] // TM_IN, x[1].shape[0] // TM_IN
        t = x[0].shape[0] + x[1].shape[0]
        in_specs = (_stream_specs(n_p, n_s, TM_IN, D_MODEL)
                    + [_full((1, D_MODEL)), _full((1, D_MODEL))])
        args = [x[0], x[1], ln[0], ln[1]]
    else:
        t = x.shape[0]
        in_specs = [row(D_MODEL)]
        args = [x]
    in_specs += [_full((D_MODEL, IN_COLS)), _full((1, IN_COLS))]
    args += [w_in_b, b_in]
    out_shape, out_specs = [], []
    if apply_ln:
        out_shape.append(jax.ShapeDtypeStruct((t, D_MODEL), F32))
        out_specs.append(row(D_MODEL))
    out_shape += [jax.ShapeDtypeStruct((t, ATTN_WIDTH), BF16),
                  jax.ShapeDtypeStruct((t, KV_WIDTH), F32),
                  jax.ShapeDtypeStruct((t, KV_WIDTH), F32),
                  jax.ShapeDtypeStruct((t, CONV_WIDTH), F32),
                  jax.ShapeDtypeStruct((t, CONV_WIDTH), F32)]
    out_specs += [row(ATTN_WIDTH), row(KV_WIDTH), row(KV_WIDTH), row(CONV_WIDTH), row(CONV_WIDTH)]
    outs = pl.pallas_call(
        functools.partial(_inproj_kernel, n_prompt=n_p),
        grid=(t // TM_IN,), in_specs=in_specs, out_specs=out_specs, out_shape=out_shape,
        compiler_params=_cparams(("arbitrary",)), name="inproj")(*args)
    if apply_ln:
        return outs
    return [x] + list(outs)


def _attend_column(qcol, kexp_b, vexp_b, mask, sink_a, sink_b, fold=None):
    lane = lax.broadcasted_iota(I32, qcol.shape, 1)
    outs = []
    for half, sink in ((0, sink_a), (1, sink_b)):
        keep = (lane < HEAD_DIM) if half == 0 else (lane >= HEAD_DIM)
        qm = jnp.where(keep, qcol, jnp.zeros_like(qcol))
        s = lax.dot_general(qm, kexp_b, (((1,), (1,)), ((), ())), preferred_element_type=F32)
        if fold is None:
            s = jnp.where(mask, s, -jnp.inf)
        else:
            upper, prev_ok = fold
            s_prev, s_cur = s[:, :WINDOW], s[:, WINDOW:]
            if prev_ok is not None:
                s_prev = jnp.where(prev_ok, s_prev, -jnp.inf)
            s = jnp.where(upper, s_prev, s_cur)
        m = jnp.maximum(jnp.max(s, -1, keepdims=True), sink)
        p = jnp.exp(s - m)
        denom = jnp.sum(p, -1, keepdims=True) + jnp.exp(sink - m)
        if fold is not None:
            zero = jnp.zeros_like(p)
            p = jnp.concatenate([jnp.where(upper, p, zero), jnp.where(upper, zero, p)], axis=1)
        o = jnp.dot(p.astype(BF16), vexp_b, preferred_element_type=F32)
        outs.append(o * (1.0 / denom))
    return jnp.where(lane < HEAD_DIM, outs[0], outs[1])


def _dup_heads(x):
    lane = lax.broadcasted_iota(I32, x.shape, 1)
    xr = pltpu.roll(x, HEAD_DIM, 1)
    low = lane < HEAD_DIM
    return (jnp.where(low, x, xr).astype(BF16), jnp.where(low, xr, x).astype(BF16))


def _prompt_mixer_kernel(sinks_ref, q_ref, kc_ref, kp_ref, vc_ref, vp_ref, uc_ref, up_ref,
                         gb_ref, cw_ref, o_ref, nk_ref, nv_ref, nu_ref):
    first = pl.program_id(1) == 0

    @pl.when(pl.program_id(1) == pl.num_programs(1) - 1)
    def _():
        nk_ref[...] = kc_ref[TQ - WINDOW:, :]
        nv_ref[...] = vc_ref[TQ - WINDOW:, :]
        nu_ref[...] = uc_ref[TQ - SUBLANES:, :]

    r_i = lax.broadcasted_iota(I32, (WINDOW, WINDOW), 0)
    c_i = lax.broadcasted_iota(I32, (WINDOW, WINDOW), 1)
    upper = c_i > r_i
    has_prev = jnp.logical_not(first)
    for s in range(TQ // WINDOW):
        rows = slice(s * WINDOW, (s + 1) * WINDOW)
        if s == 0:
            kprev, vprev = kp_ref[...], vp_ref[...]
        else:
            prev = slice((s - 1) * WINDOW, s * WINDOW)
            kprev, vprev = kc_ref[prev, :], vc_ref[prev, :]
        kk = jnp.concatenate([kprev, kc_ref[rows, :]], 0)
        vv = jnp.concatenate([vprev, vc_ref[rows, :]], 0)
        kexp = _dup_heads(kk)
        vexp = _dup_heads(vv)
        fold = (upper, has_prev if s == 0 else None)
        for col in range(ATTN_WIDTH // LANES):
            h = col // 2
            cols = slice(col * LANES, (col + 1) * LANES)
            out = _attend_column(q_ref[rows, cols], kexp[h], vexp[h], None,
                                 sinks_ref[2 * col], sinks_ref[2 * col + 1], fold=fold)
            o_ref[rows, cols] = out.astype(BF16)

    u = uc_ref[...]
    up = up_ref[...]
    zero = jnp.zeros((1, CONV_WIDTH), F32)
    p1 = jnp.where(first, zero, up[SUBLANES - 1:SUBLANES, :])
    p2 = jnp.where(first, zero, up[SUBLANES - 2:SUBLANES - 1, :])
    row = lax.broadcasted_iota(I32, u.shape, 0)
    u1 = jnp.where(row == 0, p1, pltpu.roll(u, 1, 0))
    u2 = jnp.where(row == 0, p2, jnp.where(row == 1, p1, pltpu.roll(u, 2, 0)))
    cw = cw_ref[...]
    y = u2 * cw[0:1, :] + u1 * cw[1:2, :] + u * cw[2:3, :]
    o_ref[:, ATTN_WIDTH:] = (gb_ref[...] * y).astype(BF16)


def _prompt_mixer(sinks, q, k, v, u, gb, conv_w, batch, seq):
    nj = seq // TQ
    tile = lambda b, j: b * nj + j
    cur = lambda w: pl.BlockSpec((TQ, w), lambda b, j: (tile(b, j), 0))
    prev_kv = pl.BlockSpec((WINDOW, KV_WIDTH),
                           lambda b, j: (jnp.maximum(tile(b, j) * (TQ // WINDOW) - 1, 0), 0))
    prev_u = pl.BlockSpec((SUBLANES, CONV_WIDTH),
                          lambda b, j: (jnp.maximum(tile(b, j) * (TQ // SUBLANES) - 1, 0), 0))
    return pl.pallas_call(
        _prompt_mixer_kernel,
        grid=(batch, nj),
        in_specs=[pl.BlockSpec(memory_space=pltpu.SMEM),
                  cur(ATTN_WIDTH), cur(KV_WIDTH), prev_kv, cur(KV_WIDTH), prev_kv,
                  cur(CONV_WIDTH), prev_u, cur(CONV_WIDTH), _full((CONV_K, CONV_WIDTH))],
        out_specs=[cur(D_MODEL),
                   pl.BlockSpec((None, WINDOW, KV_WIDTH), lambda b, j: (b, 0, 0)),
                   pl.BlockSpec((None, WINDOW, KV_WIDTH), lambda b, j: (b, 0, 0)),
                   pl.BlockSpec((None, SUBLANES, CONV_WIDTH), lambda b, j: (b, 0, 0))],
        out_shape=[jax.ShapeDtypeStruct((batch * seq, D_MODEL), BF16),
                   jax.ShapeDtypeStruct((batch, WINDOW, KV_WIDTH), F32),
                   jax.ShapeDtypeStruct((batch, WINDOW, KV_WIDTH), F32),
                   jax.ShapeDtypeStruct((batch, SUBLANES, CONV_WIDTH), F32)],
        compiler_params=_cparams(("arbitrary", "arbitrary")), name="prompt_mixer",
    )(sinks, q, k, k, v, v, u, u, gb, conv_w)


def _sample_mixer_kernel(sinks_ref, q_ref, kn_ref, vn_ref, kb_ref, vb_ref, u_ref, st_ref, gb_ref,
                         cw_ref, o_ref, nk_ref, nv_ref, *, dec_seq):
    g, w = kb_ref.shape[0], kb_ref.shape[1]
    rows = g * dec_seq
    n_cache = g * w
    n_keys = n_cache + 2 * rows
    kn, vn = kn_ref[...], vn_ref[...]
    pad = jnp.zeros((rows, KV_WIDTH), F32)
    kk = jnp.concatenate([kb_ref[...].reshape(n_cache, KV_WIDTH), kn, pad], 0)
    vv = jnp.concatenate([vb_ref[...].reshape(n_cache, KV_WIDTH), vn, pad], 0)
    kexp = _dup_heads(kk)
    vexp = _dup_heads(vv)

    r_i = lax.broadcasted_iota(I32, (rows, n_keys), 0)
    c_i = lax.broadcasted_iota(I32, (rows, n_keys), 1)
    r_seq, r_pos = _div_pow2(r_i, dec_seq), _mod_pow2(r_i, dec_seq)
    c_new = c_i - n_cache
    in_cache = ((c_i < n_cache) & (_div_pow2(c_i, w) == r_seq)
                & (_mod_pow2(c_i, w) > r_pos + (w - WINDOW)))
    in_new = ((c_new >= 0) & (c_new < rows) & (_div_pow2(c_new, dec_seq) == r_seq)
              & (_mod_pow2(c_new, dec_seq) <= r_pos))
    mask = in_cache | in_new
    for col in range(ATTN_WIDTH // LANES):
        h = col // 2
        cols = slice(col * LANES, (col + 1) * LANES)
        out = _attend_column(q_ref[:, cols], kexp[h], vexp[h], mask,
                             sinks_ref[2 * col], sinks_ref[2 * col + 1])
        o_ref[:, cols] = out.astype(BF16)

    nk_ref[:, 0:w - dec_seq, :] = kb_ref[:, dec_seq:w, :]
    nk_ref[:, w - dec_seq:w, :] = kn.reshape(g, dec_seq, KV_WIDTH)
    nv_ref[:, 0:w - dec_seq, :] = vb_ref[:, dec_seq:w, :]
    nv_ref[:, w - dec_seq:w, :] = vn.reshape(g, dec_seq, KV_WIDTH)

    u = u_ref[...]
    st = st_ref[...]
    pos = lax.broadcasted_iota(I32, u.shape, 0) % dec_seq
    u1 = jnp.where(pos == 0, pltpu.roll(st, rows - 1, 0), pltpu.roll(u, 1, 0))
    u2 = jnp.where(pos < 2, st, pltpu.roll(u, 2, 0))
    cw = cw_ref[...]
    y = u2 * cw[0:1, :] + u1 * cw[1:2, :] + u * cw[2:3, :]
    o_ref[:, ATTN_WIDTH:] = (gb_ref[...] * y).astype(BF16)


def _sample_mixer(sinks, q, k, v, u, gb, conv_w, layer, cache_k, cache_v, state_rows, t_prompt):
    nseq, w = cache_k.shape[1], cache_k.shape[2]
    dec_seq = (q.shape[0] - t_prompt) // nseq
    rows = SEQ_GROUP * dec_seq
    off = t_prompt // rows
    tok = lambda wd: pl.BlockSpec((rows, wd), lambda i: (off + i, 0))
    local = lambda wd: pl.BlockSpec((rows, wd), lambda i: (i, 0))
    cache_in = pl.BlockSpec((None, SEQ_GROUP, w, KV_WIDTH), lambda i: (layer, i, 0, 0))
    cache = pl.BlockSpec((SEQ_GROUP, w, KV_WIDTH), lambda i: (i, 0, 0))
    return pl.pallas_call(
        functools.partial(_sample_mixer_kernel, dec_seq=dec_seq),
        grid=(nseq // SEQ_GROUP,),
        in_specs=[pl.BlockSpec(memory_space=pltpu.SMEM),
                  tok(ATTN_WIDTH), tok(KV_WIDTH), tok(KV_WIDTH), cache_in, cache_in,
                  tok(CONV_WIDTH), local(CONV_WIDTH), tok(CONV_WIDTH),
                  _full((CONV_K, CONV_WIDTH))],
        out_specs=[local(D_MODEL), cache, cache],
        out_shape=[jax.ShapeDtypeStruct((nseq * dec_seq, D_MODEL), BF16),
                   jax.ShapeDtypeStruct((nseq, w, KV_WIDTH), F32),
                   jax.ShapeDtypeStruct((nseq, w, KV_WIDTH), F32)],
        compiler_params=_cparams(("parallel",)), name="sample_mixer",
    )(sinks, q, k, v, cache_k, cache_v, u, state_rows, gb, conv_w)


def _post_mixer_kernel(mixp_ref, mixs_ref, xn_ref, pp_ref, ps_ref, wo_ref, bo_ref, g1_ref, b1_ref,
                       wg_ref, bg_ref, wp_ref, wrh_ref, wrl_ref, br_ref,
                       x1_ref, c_ref, eid_ref, gate_ref, rank_ref, cnt_ref, carry_ref, *, n_prompt):
    @pl.when(pl.program_id(0) == 0)
    def _():
        carry_ref[...] = jnp.zeros_like(carry_ref)

    mixed = _stream_tile(n_prompt, mixp_ref, mixs_ref)
    mix = jnp.dot(mixed, wo_ref[...], preferred_element_type=F32) + bo_ref[...]
    x1 = _layer_norm(DN_ALPHA * xn_ref[...] + mix, g1_ref[...], b1_ref[...])
    for c in range(CHUNKS):
        x1_ref[pl.ds(c, x1.shape[0], stride=CHUNKS), :] = x1[:, c * LANES:(c + 1) * LANES]
    x1h = x1.astype(BF16)
    x1l = (x1 - x1h.astype(F32)).astype(BF16)
    p_tile = _stream_tile(n_prompt, pp_ref, ps_ref).astype(BF16)
    width = 2 * LANES
    for jb in range(D_MODEL // width):
        cols = slice(jb * width, (jb + 1) * width)
        gate = jax.nn.sigmoid(
            jnp.dot(x1h, wg_ref[:, cols], preferred_element_type=F32) + bg_ref[:, cols])
        ple = jnp.dot(p_tile, wp_ref[:, cols], preferred_element_type=F32)
        c_ref[:, cols] = DN_ALPHA * x1[:, cols] + gate * ple

    nt = (((1,), (1,)), ((), ()))
    logits = (lax.dot_general(wrh_ref[...], x1h, nt, preferred_element_type=F32)
              + lax.dot_general(wrh_ref[...], x1l, nt, preferred_element_type=F32)
              + lax.dot_general(wrl_ref[...], x1h, nt, preferred_element_type=F32)
              + br_ref[...])
    tm = logits.shape[1]
    e_i = lax.broadcasted_iota(I32, logits.shape, 0).astype(F32)
    work = logits
    vals, sels = [], []
    for k in range(TOP_K):
        m = jnp.max(work, 0, keepdims=True)
        idx = jnp.min(jnp.where(work == m, e_i, float(N_EXPERTS)), 0, keepdims=True)
        sel = e_i == idx
        vals.append(m)
        sels.append(sel)
        eid_ref[k:k + 1, :] = idx.astype(I32)
        work = jnp.where(sel, -jnp.inf, work)
    exps = [jnp.exp(v - vals[0]) for v in vals]
    denom = exps[0] + exps[1] + exps[2] + exps[3]
    for k in range(TOP_K):
        gate_ref[k:k + 1, :] = exps[k] / denom

    chosen = jnp.where(sels[0] | sels[1] | sels[2] | sels[3], 1.0, 0.0)
    s_i = lax.broadcasted_iota(I32, (tm, tm), 0)
    t_i = lax.broadcasted_iota(I32, (tm, tm), 1)
    before = jnp.where(s_i < t_i, 1.0, 0.0).astype(BF16)
    pos = carry_ref[:, 0:1] + jnp.dot(chosen.astype(BF16), before, preferred_element_type=F32)
    for k in range(TOP_K):
        rank_ref[k:k + 1, :] = jnp.sum(jnp.where(sels[k], pos, 0.0), 0, keepdims=True).astype(I32)
    carry = carry_ref[...] + jnp.sum(chosen, 1, keepdims=True)
    carry_ref[...] = carry
    cnt_ref[...] = carry


def _post_mixer(mix_p, mix_s, xn, layer, p_p, p_s, w_o, b_o, g1, b1, w_gate, b_gate, w_ple, wr_hi,
                wr_lo, b_r):
    t = xn.shape[0]
    n_p, n_s = mix_p.shape[0] // TM, mix_s.shape[0] // TM
    row = lambda w: pl.BlockSpec((TM, w), lambda i: (i, 0))
    meta = pl.BlockSpec((TOP_K, TM), lambda i: (0, i))
    vec = _full((1, D_MODEL))
    return pl.pallas_call(
        functools.partial(_post_mixer_kernel, n_prompt=n_p),
        grid=(t // TM,),
        in_specs=_stream_specs(n_p, n_s, TM, D_MODEL) + [row(D_MODEL)]
        + _stream_specs(n_p, n_s, TM, PLE_DIM, layer)
        + [_full((D_MODEL, D_MODEL)), vec, vec, vec,
           _full((D_MODEL, D_MODEL)), vec, _full((PLE_DIM, D_MODEL)),
           _full((N_EXPERTS, D_MODEL)), _full((N_EXPERTS, D_MODEL)), _full((N_EXPERTS, 1))],
        out_specs=[pl.BlockSpec((TM * CHUNKS, LANES), lambda i: (i, 0)), row(D_MODEL),
                   meta, meta, meta, _full((N_EXPERTS, LANES))],
        out_shape=[jax.ShapeDtypeStruct((t * CHUNKS, LANES), F32),
                   jax.ShapeDtypeStruct((t, D_MODEL), F32),
                   jax.ShapeDtypeStruct((TOP_K, t), I32),
                   jax.ShapeDtypeStruct((TOP_K, t), F32),
                   jax.ShapeDtypeStruct((TOP_K, t), I32),
                   jax.ShapeDtypeStruct((N_EXPERTS, LANES), F32)],
        scratch_shapes=[pltpu.VMEM((N_EXPERTS, LANES), F32)],
        compiler_params=_cparams(("arbitrary",)), name="post_mixer",
    )(mix_p, mix_s, xn, p_p, p_s, w_o, b_o, g1, b1, w_gate, b_gate, w_ple, wr_hi, wr_lo, b_r)


def _row_copy(src, dst, sem):
    return pltpu.make_async_copy(src, dst, sem)


CHUNKS = D_MODEL // LANES
assert CHUNKS == SUBLANES


def _load_token_tiles(ref, r0, n):
    return jnp.concatenate(
        [ref[pl.ds(r0 * CHUNKS + c, n, stride=CHUNKS), :] for c in range(CHUNKS)], axis=1)


def _token_tiles(x):
    if x.ndim == 2:
        return x.reshape(x.shape[0] // CHUNKS, CHUNKS, LANES)
    return x.reshape(x.shape[0] * CHUNKS, LANES)


def _for_each_row_slot(groups, fn):
    def group(j, carry):
        for s in range(SUBLANES):
            for k in range(TOP_K):
                fn(k, j, s, s * TOP_K + k)
        return carry

    lax.fori_loop(0, groups, group, 0)


def _dispatch_kernel(dest_ref, x_ref, xs_hbm, sem):
    tm = x_ref.shape[0]

    def scatter_row(k, j, s, n):
        r = j * SUBLANES + s
        _row_copy(x_ref.at[r], xs_hbm.at[dest_ref[0, k * tm + r]], sem).start(priority=n % 2)

    _for_each_row_slot(tm // SUBLANES, scatter_row)
    for k in range(TOP_K):
        _row_copy(x_ref, xs_hbm.at[pl.ds(0, tm)], sem).wait()


def _dest_tiles(dest, tm):
    t = dest.shape[1]
    return dest.reshape(TOP_K, t // tm, tm).transpose(1, 0, 2).reshape(t // tm, 1, TOP_K * tm)


def _dispatch(dest, x1t):
    t = x1t.shape[0] // CHUNKS
    tm = TM_DISPATCH
    xs = pl.pallas_call(
        _dispatch_kernel,
        grid=(t // tm,),
        in_specs=[pl.BlockSpec((None, 1, TOP_K * tm), lambda i: (i, 0, 0), memory_space=pltpu.SMEM),
                  pl.BlockSpec((tm, CHUNKS, LANES), lambda i: (i, 0, 0))],
        out_specs=pl.BlockSpec(memory_space=pl.ANY),
        out_shape=jax.ShapeDtypeStruct((t * TOP_K, CHUNKS, LANES), F32),
        scratch_shapes=[pltpu.SemaphoreType.DMA],
        compiler_params=_cparams(("arbitrary",)), name="dispatch",
    )(_dest_tiles(dest, tm), _token_tiles(x1t))
    return _token_tiles(xs)


def _moe_kernel(tile_ref, exp_ref, lo_ref, hi_ref, first_ref, newexp_ref,
                xs_ref, w1_ref, b1_ref, w2_ref, b2_ref, ys_ref, act_ref, w1b_ref, w2b_ref):
    del tile_ref, exp_ref
    w = pl.program_id(0)
    lo, hi = lo_ref[w], hi_ref[w]

    @pl.when(newexp_ref[w] == 1)
    def _():
        w1b_ref[...] = w1_ref[...].astype(BF16)
        r_i = lax.broadcasted_iota(I32, (LANES, LANES), 0)
        c_i = lax.broadcasted_iota(I32, (LANES, LANES), 1)
        src = (LANES // 2) * (r_i & 1) + lax.shift_right_logical(r_i, 1)
        perm = jnp.where(c_i == src, 1.0, 0.0).astype(BF16)
        for m in range(D_FF // LANES):
            blk = slice(m * LANES, (m + 1) * LANES)
            w2b_ref[blk, :] = jnp.dot(perm, w2_ref[blk, :].astype(BF16),
                                      preferred_element_type=F32).astype(BF16)

    def expert_mlp(r0, nrows, first_visit):
        rows = slice(r0, r0 + nrows)
        xb = _load_token_tiles(xs_ref, r0, nrows).astype(BF16)
        even = (lax.broadcasted_iota(I32, (nrows, LANES), 1) & 1) == 0
        for m in range(D_FF // LANES):
            cols = slice(2 * m * LANES, 2 * (m + 1) * LANES)
            h = jnp.dot(xb, w1b_ref[:, cols], preferred_element_type=F32) + b1_ref[:, cols]
            ha, hb = h[:, :LANES], h[:, LANES:]
            glu = jnp.where(even, ha, pltpu.roll(hb, 1, 1))
            lin = jnp.where(even, pltpu.roll(ha, LANES - 1, 1), hb)
            glu = jnp.minimum(glu, SWIGLU_LIMIT)
            lin = jnp.clip(lin, -SWIGLU_LIMIT, SWIGLU_LIMIT)
            act = glu * jax.nn.sigmoid(SWIGLU_ALPHA * glu) * (lin + 1.0)
            act_ref[rows, m * LANES:(m + 1) * LANES] = act.astype(BF16)
        r = r0 + lax.broadcasted_iota(I32, (nrows, LANES), 0)
        mine = (r >= lo) & (r < hi)
        if first_visit and nrows < TM_MOE:
            ys_ref[pl.ds((nrows - r0) * CHUNKS, (TM_MOE - nrows) * CHUNKS), :] = jnp.zeros(
                ((TM_MOE - nrows) * CHUNKS, LANES), F32)
        width = 2 * LANES
        for jb in range(D_MODEL // width):
            cols = slice(jb * width, (jb + 1) * width)
            y = jnp.dot(act_ref[rows, :], w2b_ref[:, cols], preferred_element_type=F32) + b2_ref[:, cols]
            for cc in range(width // LANES):
                at = pl.ds(r0 * CHUNKS + jb * (width // LANES) + cc, nrows, stride=CHUNKS)
                old = 0.0 if first_visit else ys_ref[at, :]
                ys_ref[at, :] = jnp.where(mine, y[:, cc * LANES:(cc + 1) * LANES], old)

    half = TM_MOE // 2
    some = hi > lo
    for first_visit in (True, False):
        visit = some & (first_ref[w] == (1 if first_visit else 0))

        @pl.when(visit & (hi <= half))
        def _():
            expert_mlp(0, half, first_visit)

        @pl.when(visit & (lo >= half))
        def _():
            expert_mlp(half, half, first_visit)

        @pl.when(visit & (lo < half) & (hi > half))
        def _():
            expert_mlp(0, TM_MOE, first_visit)


def _moe(sched, xs, layer, w1, b1, w2, b2):
    a = xs.shape[0] // CHUNKS
    n_items = sched[0].shape[0]
    by_expert = lambda r, c: pl.BlockSpec((None, None, r, c),
                                          lambda w, tile, ex, *_: (layer, ex[w], 0, 0))
    rows = pl.BlockSpec((TM_MOE * CHUNKS, LANES), lambda w, tile, *_: (tile[w], 0))
    return pl.pallas_call(
        _moe_kernel,
        grid_spec=pltpu.PrefetchScalarGridSpec(
            num_scalar_prefetch=6, grid=(n_items,),
            in_specs=[rows, by_expert(D_MODEL, 2 * D_FF), by_expert(1, 2 * D_FF),
                      by_expert(D_FF, D_MODEL), by_expert(1, D_MODEL)],
            out_specs=rows,
            scratch_shapes=[pltpu.VMEM((TM_MOE, D_FF), BF16),
                            pltpu.VMEM((D_MODEL, 2 * D_FF), BF16),
                            pltpu.VMEM((D_FF, D_MODEL), BF16)]),
        out_shape=jax.ShapeDtypeStruct((a * CHUNKS, LANES), F32),
        compiler_params=pltpu.CompilerParams(dimension_semantics=("arbitrary",),
                                             vmem_limit_bytes=VMEM_LIMIT_MOE),
        name="moe",
    )(*sched, xs, w1, b1, w2, b2)


def _moe_schedule(counts, n_rows):
    n_tiles = n_rows // TM_MOE
    n_items = n_tiles + N_EXPERTS - 1
    pend = jnp.cumsum(counts)
    pstart = pend - counts
    first_tile = pstart // TM_MOE
    last_tile = jnp.maximum(pend - 1, 0) // TM_MOE
    ntile = jnp.where(counts > 0, last_tile - first_tile + 1, 0)
    wend = jnp.cumsum(ntile)
    wstart = wend - ntile
    total = wend[-1]
    w = jnp.arange(n_items, dtype=I32)
    wv = jnp.minimum(w, total - 1)
    ex = jnp.minimum(jnp.sum(wend[None, :] <= wv[:, None], 1), N_EXPERTS - 1).astype(I32)
    is_ex = ex[:, None] == jnp.arange(N_EXPERTS, dtype=I32)[None, :]
    of_ex = lambda per_expert: jnp.sum(jnp.where(is_ex, per_expert[None, :], 0), 1)
    tile = (of_ex(first_tile) + wv - of_ex(wstart)).astype(I32)
    valid = w < total
    lo = jnp.where(valid, jnp.clip(of_ex(pstart) - tile * TM_MOE, 0, TM_MOE), 0).astype(I32)
    hi = jnp.where(valid, jnp.clip(of_ex(pend) - tile * TM_MOE, 0, TM_MOE), 0).astype(I32)
    prev_tile = jnp.concatenate([jnp.full((1,), -1, I32), tile[:-1]])
    first = (valid & (tile != prev_tile)).astype(I32)
    prev_ex = jnp.concatenate([jnp.full((1,), -1, I32), ex[:-1]])
    newexp = (valid & (ex != prev_ex)).astype(I32)
    return (tile, ex, lo, hi, first, newexp), pstart


def _combine_kernel(dcur_ref, dnext_ref, c_ref, gate_ref, g2_ref, b2_ref, ys_hbm, ys_flat_hbm, *rest,
                    n_prompt):
    *outs, buf, sem = rest
    tm = c_ref.shape[0]
    i = pl.program_id(0)

    def gather_tile(dest_ref, slot):
        def gather_row(k, j, s, n):
            r = j * SUBLANES + s
            dst = buf.at[slot, k, pl.ds(pl.multiple_of(r * CHUNKS, CHUNKS), CHUNKS)]
            _row_copy(ys_hbm.at[dest_ref[0, k * tm + r]], dst, sem.at[slot]).start(priority=n % 2)

        _for_each_row_slot(tm // SUBLANES, gather_row)

    @pl.when(i == 0)
    def _():
        gather_tile(dcur_ref, 0)

    @pl.when(i + 1 < pl.num_programs(0))
    def _():
        gather_tile(dnext_ref, (i + 1) % 2)

    slot = i % 2
    for k in range(TOP_K):
        _row_copy(ys_flat_hbm.at[pl.ds(0, tm * CHUNKS)], buf.at[slot, k], sem.at[slot]).wait()
    gates = gate_ref[...]
    gate_cols = [jnp.broadcast_to(gates[:, k:k + 1], (tm, LANES)) for k in range(TOP_K)]
    parts = []
    for c in range(CHUNKS):
        part = c_ref[:, c * LANES:(c + 1) * LANES]
        for k in range(TOP_K):
            part = part + gate_cols[k] * buf[slot, k, pl.ds(c, tm, stride=CHUNKS), :]
        parts.append(part)
    acc = jnp.concatenate(parts, axis=1)
    y = _layer_norm(acc, g2_ref[...], b2_ref[...])
    if n_prompt is None:
        outs[0][...] = y
    else:
        @pl.when(i < n_prompt)
        def _():
            outs[0][...] = y

        @pl.when(i >= n_prompt)
        def _():
            outs[1][...] = y


def _combine(dest, c, gates_t, g2, b2, ys, t_prompt=None):
    t = c.shape[0]
    tm = TM_COMBINE
    n = t // tm
    vec = _full((1, D_MODEL))
    if t_prompt is None:
        n_p = None
        out_specs = pl.BlockSpec((tm, D_MODEL), lambda i: (i, 0))
        out_shape = jax.ShapeDtypeStruct((t, D_MODEL), F32)
    else:
        n_p = t_prompt // tm
        out_specs = _stream_specs(n_p, n - n_p, tm, D_MODEL)
        out_shape = [jax.ShapeDtypeStruct((t_prompt, D_MODEL), F32),
                     jax.ShapeDtypeStruct((t - t_prompt, D_MODEL), F32)]
    dest_spec = lambda f: pl.BlockSpec((None, 1, TOP_K * tm), lambda i: (f(i), 0, 0),
                                       memory_space=pltpu.SMEM)
    dest_tiles = _dest_tiles(dest, tm)
    return pl.pallas_call(
        functools.partial(_combine_kernel, n_prompt=n_p),
        grid=(n,),
        in_specs=[dest_spec(lambda i: i), dest_spec(lambda i: jnp.minimum(i + 1, n - 1)),
                  pl.BlockSpec((tm, D_MODEL), lambda i: (i, 0)),
                  pl.BlockSpec((tm, TOP_K), lambda i: (i, 0)),
                  vec, vec, pl.BlockSpec(memory_space=pl.ANY), pl.BlockSpec(memory_space=pl.ANY)],
        out_specs=out_specs,
        out_shape=out_shape,
        scratch_shapes=[pltpu.VMEM((2, TOP_K, tm * CHUNKS, LANES), F32),
                        pltpu.SemaphoreType.DMA((2,))],
        compiler_params=_cparams(("arbitrary",)), name="combine",
    )(dest_tiles, dest_tiles, c, gates_t, g2, b2, _token_tiles(ys), ys)


def kernel(x_prompt, x_sample, cache_k, cache_v, state_conv, p_prompt, p_sample, ln_emb_g, ln_emb_b,
           w_in, b_in, conv_w, sinks, w_o, b_o, ln1_g, ln1_b, w_gate, b_gate, w_ple, w_router,
           b_router, w1, b1, w2, b2, ln2_g, ln2_b):
    batch, seq, _ = x_prompt.shape
    nseq, dec_seq, _ = x_sample.shape
    t_p, t_s = batch * seq, nseq * dec_seq
    t = t_p + t_s
    w_cache = cache_k.shape[2]
    vec = lambda a: a.reshape(1, -1)

    x = (x_prompt.reshape(t_p, D_MODEL), x_sample.reshape(t_s, D_MODEL))
    state_rows = jnp.pad(state_conv, ((0, 0), (0, 0), (0, dec_seq - (CONV_K - 1)), (0, 0)))
    state_rows = state_rows.reshape(DEPTH, t_s, CONV_WIDTH)

    ks_p, vs_p, cs_p, ks_s, vs_s, cs_s = [], [], [], [], [], []
    for l in range(DEPTH):
        w_in_b = w_in[l].astype(BF16)
        xn, q, k, v, gb, u = _inproj(x, (vec(ln_emb_g), vec(ln_emb_b)), w_in_b, vec(b_in[l]))

        mix_p, nk_p, nv_p, nu_p = _prompt_mixer(sinks[l], q, k, v, u, gb, conv_w[l], batch, seq)
        mix_s, nk_s, nv_s = _sample_mixer(
            sinks[l], q, k, v, u, gb, conv_w[l], l,
            cache_k.reshape(DEPTH, nseq, w_cache, KV_WIDTH),
            cache_v.reshape(DEPTH, nseq, w_cache, KV_WIDTH), state_rows[l], t_p)

        wr_t = w_router[l].T
        wr_hi = wr_t.astype(BF16)
        wr_lo = (wr_t - wr_hi.astype(F32)).astype(BF16)
        x1, c, eid, gates, ranks, cnt = _post_mixer(
            mix_p, mix_s, xn, l, p_prompt.reshape(DEPTH, t_p, PLE_DIM),
            p_sample.reshape(DEPTH, t_s, PLE_DIM), w_o[l].astype(BF16), vec(b_o[l]), vec(ln1_g[l]), vec(ln1_b[l]),
            w_gate[l].astype(BF16), vec(b_gate[l]), w_ple[l].astype(BF16), wr_hi, wr_lo,
            b_router[l].reshape(N_EXPERTS, 1))

        counts = cnt[:, 0].astype(I32)
        sched, pstart = _moe_schedule(counts, t * TOP_K)
        onehot = eid[:, :, None] == jnp.arange(N_EXPERTS, dtype=I32)
        dest = ranks + jnp.sum(jnp.where(onehot, pstart, 0), -1)

        xs = _dispatch(dest, x1)
        ys = _moe(sched, xs, l, w1, b1[:, :, None, :], w2, b2[:, :, None, :])
        x = _combine(dest, c, gates.T, vec(ln2_g[l]), vec(ln2_b[l]), ys,
                     t_prompt=t_p if l == DEPTH - 1 else None)

        ks_p.append(nk_p.reshape(batch, WINDOW, N_KV_HEADS, HEAD_DIM))
        vs_p.append(nv_p.reshape(batch, WINDOW, N_KV_HEADS, HEAD_DIM))
        cs_p.append(nu_p[:, SUBLANES - (CONV_K - 1):])
        ks_s.append(nk_s.reshape(nseq, w_cache, N_KV_HEADS, HEAD_DIM))
        vs_s.append(nv_s.reshape(nseq, w_cache, N_KV_HEADS, HEAD_DIM))
        cs_s.append(u[t_p:].reshape(nseq, dec_seq, CONV_WIDTH)[:, dec_seq - (CONV_K - 1):])

    y_prompt = x[0].reshape(batch, seq, D_MODEL)
    y_sample = x[1].reshape(nseq, dec_seq, D_MODEL)
    return (y_prompt, y_sample, jnp.stack(ks_p), jnp.stack(vs_p), jnp.stack(cs_p),
            jnp.stack(ks_s), jnp.stack(vs_s), jnp.stack(cs_s))
```
